```python
import jax, jax.numpy as jnp
from jax import lax
import numpy as np

D_MODEL = 4096
BATCH = 4
SEQ = 2048
DEPTH = 2
DEC_BATCH = 32
DEC_SEQ = 8
PAST_LEN = 16384
PAGE_SIZE = 128

MIX_WIDTH = D_MODEL
ATT_WIDTH = MIX_WIDTH // 2
GM_WIDTH = MIX_WIDTH - ATT_WIDTH
HEAD_DIM = 64
N_HEADS = ATT_WIDTH // HEAD_DIM
N_KV_HEADS = 4
GQA_GROUP = N_HEADS // N_KV_HEADS
WINDOW = 128
CHUNK = 128
GM_HEAD_DIM = 128
GM_HEADS = GM_WIDTH // GM_HEAD_DIM
D_FF = ((8 * D_MODEL // 3 + 255) // 256) * 256
CONV_W = 3
EPS = 1e-6
Q_COLS = N_HEADS * HEAD_DIM
KV_COLS = N_KV_HEADS * HEAD_DIM
IN_COLS = Q_COLS + 2 * KV_COLS + 2 * GM_WIDTH

kernel_name = "hymba_style_swa_sink_gmlp_convffn_step"


def rmsnorm(x, g):
    xf = x.astype(jnp.float32)
    y = xf * lax.rsqrt(jnp.mean(xf * xf, axis=-1, keepdims=True) + EPS)
    return (y * g.astype(jnp.float32)).astype(x.dtype)


def sink_attention(q, k, v, q_pos, k_pos, sinks):
    B, N, Tq, H, D = q.shape
    qg = q.reshape(B, N, Tq, N_KV_HEADS, GQA_GROUP, D)
    s = jnp.einsum("bnqkgd,bnskd->bnkgqs", qg, k,
                   preferred_element_type=jnp.float32) * (D ** -0.5)
    rel = q_pos[:, :, None] - k_pos[:, None, :]
    mask = (rel >= 0) & (rel < WINDOW) & (k_pos[:, None, :] >= 0)
    s = jnp.where(mask[None, :, None, None], s, -jnp.inf)
    sink = sinks.astype(jnp.float32).reshape(N_KV_HEADS, GQA_GROUP)[None, None, :, :, None, None]
    full = jnp.concatenate([s, jnp.broadcast_to(sink, s.shape[:-1] + (1,))], axis=-1)
    p = jax.nn.softmax(full, axis=-1)[..., :-1]
    o = jnp.einsum("bnkgqs,bnskd->bnqkgd", p.astype(v.dtype), v)
    return o.reshape(B, N, Tq, H * D)


def swa_prompt(q, k, v, sinks):
    B, S = q.shape[:2]
    N = S // WINDOW
    qb = q.reshape(B, N, WINDOW, N_HEADS, HEAD_DIM)
    kb = k.reshape(B, N, WINDOW, N_KV_HEADS, HEAD_DIM)
    vb = v.reshape(B, N, WINDOW, N_KV_HEADS, HEAD_DIM)
    pad = ((0, 0), (1, 0), (0, 0), (0, 0), (0, 0))
    kk = jnp.concatenate([jnp.pad(kb, pad)[:, :-1], kb], axis=2)
    vv = jnp.concatenate([jnp.pad(vb, pad)[:, :-1], vb], axis=2)
    start = jnp.arange(N)[:, None] * WINDOW
    q_pos = start + jnp.arange(WINDOW)[None]
    k_pos = start - WINDOW + jnp.arange(2 * WINDOW)[None]
    o = sink_attention(qb, kk, vv, q_pos, k_pos, sinks)
    return o.reshape(B, S, ATT_WIDTH)


def swa_sample(q, k, v, k_buf, v_buf, sinks):
    B, T = q.shape[:2]
    L = k_buf.shape[1]
    kk = jnp.concatenate([k_buf, k], axis=1)[:, None]
    vv = jnp.concatenate([v_buf, v], axis=1)[:, None]
    q_pos = (PAST_LEN + jnp.arange(T))[None]
    k_pos = (PAST_LEN - L + jnp.arange(L + T))[None]
    o = sink_attention(q[:, None], kk, vv, q_pos, k_pos, sinks)
    return o.reshape(B, T, ATT_WIDTH)


def chunk_sgu(u, v, w_s, b_s):
    B, T = u.shape[:2]
    n = -(-T // CHUNK)
    vp = jnp.pad(v, ((0, 0), (0, n * CHUNK - T), (0, 0), (0, 0)))
    vp = vp.reshape(B, n, CHUNK, GM_HEADS, GM_HEAD_DIM)
    causal = jnp.tril(jnp.ones((CHUNK, CHUNK), dtype=bool))
    w = jnp.where(causal[None], w_s, jnp.zeros_like(w_s))
    mixed = jnp.einsum("hts,bnshe->bnthe", w, vp) + b_s.T[None, None, :, :, None]
    mixed = mixed.reshape(B, n * CHUNK, GM_HEADS, GM_HEAD_DIM)[:, :T]
    return (u * mixed).reshape(B, T, GM_WIDTH)


def token_mixer(h, k_buf, v_buf, w_in, w_out, sinks, w_s, b_s, g_gm):
    B, T, _ = h.shape
    z = jnp.einsum("btd,dc->btc", h, w_in)
    q = z[..., :Q_COLS].reshape(B, T, N_HEADS, HEAD_DIM)
    k = z[..., Q_COLS:Q_COLS + KV_COLS].reshape(B, T, N_KV_HEADS, HEAD_DIM)
    v = z[..., Q_COLS + KV_COLS:Q_COLS + 2 * KV_COLS].reshape(B, T, N_KV_HEADS, HEAD_DIM)
    gm = jax.nn.gelu(z[..., Q_COLS + 2 * KV_COLS:], approximate=True)
    u = gm[..., :GM_WIDTH].reshape(B, T, GM_HEADS, GM_HEAD_DIM)
    gv = rmsnorm(gm[..., GM_WIDTH:], g_gm).reshape(B, T, GM_HEADS, GM_HEAD_DIM)
    if k_buf is None:
        att = swa_prompt(q, k, v, sinks)
        keep = min(WINDOW, T)
        new_k, new_v = k[:, T - keep:], v[:, T - keep:]
    else:
        att = swa_sample(q, k, v, k_buf, v_buf, sinks)
        new_k, new_v = k, v
    gmo = chunk_sgu(u, gv, w_s, b_s)
    y = jnp.einsum("btc,cd->btd", jnp.concatenate([att, gmo], axis=-1), w_out)
    return y, new_k, new_v, gv


def conv_ffn(h, conv_buf, w_gate, w_up, w_down, conv_w, conv_b):
    T = h.shape[1]
    g = jnp.einsum("btd,df->btf", h, w_gate)
    up = jnp.einsum("btd,df->btf", h, w_up)
    ext = jnp.concatenate([conv_buf.astype(g.dtype), g], axis=1)
    c = conv_b
    for j in range(CONV_W):
        c = c + conv_w[j] * ext[:, j:j + T]
    a = jax.nn.gelu(c, approximate=True) * up
    y = jnp.einsum("btf,fd->btd", a, w_down)
    return y, ext[:, ext.shape[1] - (CONV_W - 1):]


def decoder_layer(x, k_buf, v_buf, conv_buf, w_in, w_out, sinks, w_s, b_s, g_gm,
                  g_pre_mix, g_post_mix, g_pre_ffn, g_post_ffn,
                  w_gate, w_up, w_down, conv_w, conv_b):
    m, new_k, new_v, gv = token_mixer(rmsnorm(x, g_pre_mix), k_buf, v_buf,
                                      w_in, w_out, sinks, w_s, b_s, g_gm)
    x = x + rmsnorm(m, g_post_mix)
    f, new_conv = conv_ffn(rmsnorm(x, g_pre_ffn), conv_buf, w_gate, w_up, w_down, conv_w, conv_b)
    x = x + rmsnorm(f, g_post_ffn)
    return x, new_k, new_v, new_conv, gv


def setup_inputs(seed: int = 0) -> dict:
    key = jax.random.key(seed)
    ks = jax.random.split(key, 24)
    f32 = jnp.float32
    nrm = lambda k, shape, scale: (jax.random.normal(k, shape, f32) * scale)
    buf = min(WINDOW, PAST_LEN)
    return {
        "x_prompt": nrm(ks[0], (BATCH, SEQ, D_MODEL), 1.0),
        "x_sample": nrm(ks[1], (DEC_BATCH, DEC_SEQ, D_MODEL), 1.0),
        "state_swa_k": nrm(ks[2], (DEPTH, DEC_BATCH, buf, N_KV_HEADS, HEAD_DIM), 1.0),
        "state_swa_v": nrm(ks[3], (DEPTH, DEC_BATCH, buf, N_KV_HEADS, HEAD_DIM), 1.0),
        "state_conv": nrm(ks[4], (DEPTH, DEC_BATCH, CONV_W - 1, D_FF), 1.0),
        "w_in": nrm(ks[5], (DEPTH, D_MODEL, IN_COLS), D_MODEL ** -0.5),
        "w_out": nrm(ks[6], (DEPTH, MIX_WIDTH, D_MODEL), MIX_WIDTH ** -0.5),
        "attn_sinks": nrm(ks[7], (DEPTH, N_HEADS), 0.5),
        "gm_spatial": nrm(ks[8], (DEPTH, GM_HEADS, CHUNK, CHUNK), CHUNK ** -0.5),
        "gm_bias": 1.0 + nrm(ks[9], (DEPTH, GM_HEADS, CHUNK), 0.1),
        "gm_norm": 1.0 + nrm(ks[10], (DEPTH, GM_WIDTH), 0.05),
        "norm_pre_mix": 1.0 + nrm(ks[11], (DEPTH, D_MODEL), 0.05),
        "norm_post_mix": 1.0 + nrm(ks[12], (DEPTH, D_MODEL), 0.05),
        "norm_pre_ffn": 1.0 + nrm(ks[13], (DEPTH, D_MODEL), 0.05),
        "norm_post_ffn": 1.0 + nrm(ks[14], (DEPTH, D_MODEL), 0.05),
        "w_ffn_gate": nrm(ks[15], (DEPTH, D_MODEL, D_FF), D_MODEL ** -0.5),
        "w_ffn_up": nrm(ks[16], (DEPTH, D_MODEL, D_FF), D_MODEL ** -0.5),
        "w_ffn_down": nrm(ks[17], (DEPTH, D_FF, D_MODEL), D_FF ** -0.5),
        "conv_w": nrm(ks[18], (DEPTH, CONV_W, D_FF), CONV_W ** -0.5),
        "conv_b": nrm(ks[19], (DEPTH, D_FF), 0.02),
    }


def reference(x_prompt, x_sample, state_swa_k, state_swa_v, state_conv,
              w_in, w_out, attn_sinks, gm_spatial, gm_bias, gm_norm,
              norm_pre_mix, norm_post_mix, norm_pre_ffn, norm_post_ffn,
              w_ffn_gate, w_ffn_up, w_ffn_down, conv_w, conv_b):
    xp, xs = x_prompt, x_sample
    p_k, p_v, p_c, s_k, s_v, s_c, s_g = [], [], [], [], [], [], []
    zero_conv = jnp.zeros((xp.shape[0], CONV_W - 1, D_FF), xp.dtype)
    for l in range(DEPTH):
        w = (w_in[l], w_out[l], attn_sinks[l], gm_spatial[l], gm_bias[l], gm_norm[l],
             norm_pre_mix[l], norm_post_mix[l], norm_pre_ffn[l], norm_post_ffn[l],
             w_ffn_gate[l], w_ffn_up[l], w_ffn_down[l], conv_w[l], conv_b[l])
        xp, nk, nv, nc, _ = decoder_layer(xp, None, None, zero_conv, *w)
        p_k.append(nk); p_v.append(nv); p_c.append(nc)
        xs, nk, nv, nc, gv = decoder_layer(xs, state_swa_k[l], state_swa_v[l], state_conv[l], *w)
        s_k.append(nk); s_v.append(nv); s_c.append(nc); s_g.append(gv)
    return (xp, xs,
            jnp.stack(p_k), jnp.stack(p_v), jnp.stack(p_c),
            jnp.stack(s_k), jnp.stack(s_v), jnp.stack(s_c), jnp.stack(s_g))
```

```python
import functools

import jax
import jax.numpy as jnp
from jax import lax
from jax.experimental import pallas as pl
from jax.experimental.pallas import tpu as pltpu

F32 = jnp.float32
BF16 = jnp.bfloat16

D_MODEL = 4096
HEAD_DIM = 64
N_HEADS = 32
N_KV_HEADS = 4
GQA_GROUP = 8
WINDOW = 128
CHUNK = 128
GM_HEAD_DIM = 128
GM_HEADS = 16
GM_WIDTH = 2048
Q_COLS = 2048
KV_COLS = 256
CONV_W = 3
EPS = 1e-6

MIB = 1024 * 1024


def _params(semantics, vmem_mib):
    return pltpu.CompilerParams(
        dimension_semantics=semantics, vmem_limit_bytes=vmem_mib * MIB
    )


def _rms(x, g):
    ms = jnp.mean(x * x, axis=-1, keepdims=True)
    return x * lax.rsqrt(ms + EPS) * g


def _rms_kernel(x_ref, g_ref, o_ref):
    o_ref[...] = _rms(x_ref[...], g_ref[...]).astype(o_ref.dtype)


def rms_rows(x, g, *, tm):
    m, d = x.shape
    return pl.pallas_call(
        _rms_kernel,
        grid=(m // tm,),
        in_specs=[
            pl.BlockSpec((tm, d), lambda i: (i, 0)),
            pl.BlockSpec((1, d), lambda i: (0, 0)),
        ],
        out_specs=pl.BlockSpec((tm, d), lambda i: (i, 0)),
        out_shape=jax.ShapeDtypeStruct((m, d), BF16),
        compiler_params=_params(("parallel",), 40),
        name="rms_rows",
    )(x, g)


def _post_kernel(m_ref, x_ref, gpost_ref, gnext_ref, xo_ref, ho_ref):
    xn = x_ref[...] + _rms(m_ref[...].astype(F32), gpost_ref[...])
    xo_ref[...] = xn
    ho_ref[...] = _rms(xn, gnext_ref[...]).astype(ho_ref.dtype)


def _post_last_kernel(m_ref, x_ref, gpost_ref, xo_ref):
    xo_ref[...] = x_ref[...] + _rms(m_ref[...].astype(F32), gpost_ref[...])


def post_rows(m_in, x, g_post, g_next, *, tm):
    m, d = x.shape
    row = pl.BlockSpec((tm, d), lambda i: (i, 0))
    vec = pl.BlockSpec((1, d), lambda i: (0, 0))
    if g_next is None:
        return pl.pallas_call(
            _post_last_kernel,
            grid=(m // tm,),
            in_specs=[row, row, vec],
            out_specs=row,
            out_shape=jax.ShapeDtypeStruct((m, d), F32),
            compiler_params=_params(("parallel",), 40),
            name="post_last",
        )(m_in, x, g_post)
    return pl.pallas_call(
        _post_kernel,
        grid=(m // tm,),
        in_specs=[row, row, vec, vec],
        out_specs=[row, row],
        out_shape=[
            jax.ShapeDtypeStruct((m, d), F32),
            jax.ShapeDtypeStruct((m, d), BF16),
        ],
        compiler_params=_params(("parallel",), 40),
        name="post_rows",
    )(m_in, x, g_post, g_next)


def _mm_kernel(a_ref, b_ref, o_ref, *, epilogue):
    acc = jnp.dot(a_ref[...], b_ref[...], preferred_element_type=F32)
    o_ref[...] = epilogue(acc).astype(o_ref.dtype)


def _mm2_kernel(a1_ref, a2_ref, b1_ref, b2_ref, o_ref):
    acc = jnp.dot(a1_ref[...], b1_ref[...], preferred_element_type=F32)
    acc = acc + jnp.dot(a2_ref[...], b2_ref[...], preferred_element_type=F32)
    o_ref[...] = acc.astype(o_ref.dtype)


def matmul_cols(a, w, layer, *, col_block0, n, tm, tn, out_dtype, epilogue, name):
    m, k = a.shape
    return pl.pallas_call(
        functools.partial(_mm_kernel, epilogue=epilogue),
        grid=(m // tm, n // tn),
        in_specs=[
            pl.BlockSpec((tm, k), lambda i, j: (i, 0)),
            pl.BlockSpec((None, k, tn), lambda i, j: (layer, 0, j + col_block0)),
        ],
        out_specs=pl.BlockSpec((tm, tn), lambda i, j: (i, j)),
        out_shape=jax.ShapeDtypeStruct((m, n), out_dtype),
        compiler_params=_params(("parallel", "arbitrary"), 56),
        name=name,
    )(a, w)


def matmul_two(a1, a2, w, layer, *, tm, tn, out_dtype, name):
    m, k1 = a1.shape
    k2 = a2.shape[1]
    assert k1 == k2
    n = w.shape[-1]
    return pl.pallas_call(
        _mm2_kernel,
        grid=(m // tm, n // tn),
        in_specs=[
            pl.BlockSpec((tm, k1), lambda i, j: (i, 0)),
            pl.BlockSpec((tm, k2), lambda i, j: (i, 0)),
            pl.BlockSpec((None, None, k1, tn), lambda i, j: (layer, 0, 0, j)),
            pl.BlockSpec((None, None, k2, tn), lambda i, j: (layer, 1, 0, j)),
        ],
        out_specs=pl.BlockSpec((tm, tn), lambda i, j: (i, j)),
        out_shape=jax.ShapeDtypeStruct((m, n), out_dtype),
        compiler_params=_params(("parallel", "arbitrary"), 56),
        name=name,
    )(a1, a2, w, w)


def _identity(x):
    return x


def _scale_q(x):
    return x * (HEAD_DIM ** -0.5)


def _gelu(x):
    return jax.nn.gelu(x, approximate=True)


def _softmax_with_sink(s, sink):
    m = jnp.maximum(jnp.max(s, axis=-1, keepdims=True), sink)
    e = jnp.exp(s - m)
    den = jnp.sum(e, axis=-1, keepdims=True) + jnp.exp(sink - m)
    return e / den


def _attn_prompt_kernel(sink_ref, q_ref, kvc_ref, kvp_ref, o_ref, *, blocks_per_seq):
    n = pl.program_id(0) % blocks_per_seq
    w = WINDOW
    row = lax.broadcasted_iota(jnp.int32, (w, 2 * w), 0)
    col = lax.broadcasted_iota(jnp.int32, (w, 2 * w), 1)
    prev_ok = jnp.logical_and(jnp.logical_and(col < w, col > row), n > 0)
    own_ok = jnp.logical_and(col >= w, col - w <= row)
    valid = jnp.logical_or(prev_ok, own_ok)
    for kvh in range(N_KV_HEADS):
        ks = slice(kvh * HEAD_DIM, (kvh + 1) * HEAD_DIM)
        vs = slice(KV_COLS + kvh * HEAD_DIM, KV_COLS + (kvh + 1) * HEAD_DIM)
        k = jnp.concatenate([kvp_ref[:, ks], kvc_ref[:, ks]], axis=0).astype(BF16)
        v = jnp.concatenate([kvp_ref[:, vs], kvc_ref[:, vs]], axis=0).astype(BF16)
        heads = [kvh * GQA_GROUP + g for g in range(GQA_GROUP)]
        qs = jnp.concatenate(
            [q_ref[:, h * HEAD_DIM:(h + 1) * HEAD_DIM] for h in heads], axis=0
        )
        s = lax.dot_general(
            qs, k, (((1,), (1,)), ((), ())), preferred_element_type=F32
        )
        ps = []
        for g, h in enumerate(heads):
            sg = jnp.where(valid, s[g * w:(g + 1) * w], -jnp.inf)
            ps.append(_softmax_with_sink(sg, sink_ref[h]).astype(BF16))
        o = jnp.dot(jnp.concatenate(ps, axis=0), v, preferred_element_type=F32)
        for g, h in enumerate(heads):
            o_ref[:, h * HEAD_DIM:(h + 1) * HEAD_DIM] = o[g * w:(g + 1) * w].astype(
                o_ref.dtype
            )


def attn_prompt(q, kv, sinks, *, m_total, m_prompt, seq):
    w = WINDOW
    nblk = m_prompt // w
    return pl.pallas_call(
        functools.partial(_attn_prompt_kernel, blocks_per_seq=seq // w),
        grid=(nblk,),
        in_specs=[
            pl.BlockSpec(memory_space=pltpu.SMEM),
            pl.BlockSpec((w, Q_COLS), lambda r: (r, 0)),
            pl.BlockSpec((w, 2 * KV_COLS), lambda r: (r, 0)),
            pl.BlockSpec((w, 2 * KV_COLS), lambda r: (jnp.maximum(r - 1, 0), 0)),
        ],
        out_specs=pl.BlockSpec((w, Q_COLS), lambda r: (r, 0)),
        out_shape=jax.ShapeDtypeStruct((m_total, Q_COLS), BF16),
        compiler_params=_params(("parallel",), 40),
        name="attn_prompt",
    )(sinks, q, kv, kv)


def _attn_sample_kernel(sink_ref, q_ref, k_ref, v_ref, o_ref, *, t_new):
    nb, nkv, rows, d = q_ref.shape
    keys = k_ref.shape[2]
    q = q_ref[...].reshape(nb * nkv, rows, d)
    k = k_ref[...].reshape(nb * nkv, keys, d)
    v = v_ref[...].reshape(nb * nkv, keys, d)
    s = jnp.einsum("bqd,bkd->bqk", q, k, preferred_element_type=F32)
    t = lax.broadcasted_iota(jnp.int32, (rows, keys), 0) // GQA_GROUP
    col = lax.broadcasted_iota(jnp.int32, (rows, keys), 1)
    valid = jnp.logical_and(col > t, col <= WINDOW + t)
    s = jnp.where(valid[None], s, -jnp.inf)
    sink = jnp.broadcast_to(sink_ref[...][None], (nb, nkv, rows, 1)).reshape(
        nb * nkv, rows, 1
    )
    p = _softmax_with_sink(s, sink).astype(BF16)
    o = jnp.einsum("bqk,bkd->bqd", p, v, preferred_element_type=F32)
    o_ref[...] = o.reshape(nb, nkv, rows, d).astype(o_ref.dtype)


def attn_sample(q_s, k_new, v_new, k_buf, v_buf, sinks, *, nb):
    b, l = k_buf.shape[0], k_buf.shape[1]
    t_new = q_s.shape[0] // b
    rows = t_new * GQA_GROUP
    keys = 2 * WINDOW
    q4 = q_s.reshape(b, t_new, N_KV_HEADS, GQA_GROUP, HEAD_DIM)
    q4 = q4.transpose(0, 2, 1, 3, 4).reshape(b, N_KV_HEADS, rows, HEAD_DIM)

    def keys_layout(buf, new):
        cat = jnp.concatenate(
            [buf, new.reshape(b, t_new, N_KV_HEADS, HEAD_DIM)], axis=1
        )
        cat = jnp.pad(cat, ((0, 0), (0, keys - l - t_new), (0, 0), (0, 0)))
        return cat.transpose(0, 2, 1, 3).astype(BF16)

    k4 = keys_layout(k_buf, k_new)
    v4 = keys_layout(v_buf, v_new)
    sink_rows = jnp.broadcast_to(
        sinks.reshape(N_KV_HEADS, 1, GQA_GROUP), (N_KV_HEADS, t_new, GQA_GROUP)
    ).reshape(N_KV_HEADS, rows, 1)
    o4 = pl.pallas_call(
        functools.partial(_attn_sample_kernel, t_new=t_new),
        grid=(b // nb,),
        in_specs=[
            pl.BlockSpec((N_KV_HEADS, rows, 1), lambda i: (0, 0, 0)),
            pl.BlockSpec((nb, N_KV_HEADS, rows, HEAD_DIM), lambda i: (i, 0, 0, 0)),
            pl.BlockSpec((nb, N_KV_HEADS, keys, HEAD_DIM), lambda i: (i, 0, 0, 0)),
            pl.BlockSpec((nb, N_KV_HEADS, keys, HEAD_DIM), lambda i: (i, 0, 0, 0)),
        ],
        out_specs=pl.BlockSpec(
            (nb, N_KV_HEADS, rows, HEAD_DIM), lambda i: (i, 0, 0, 0)
        ),
        out_shape=jax.ShapeDtypeStruct((b, N_KV_HEADS, rows, HEAD_DIM), BF16),
        compiler_params=_params(("parallel",), 40),
        name="attn_sample",
    )(sink_rows, q4, k4, v4)
    o = o4.reshape(b, N_KV_HEADS, t_new, GQA_GROUP, HEAD_DIM).transpose(0, 2, 1, 3, 4)
    return o.reshape(b * t_new, Q_COLS)


def _gmlp_prompt_kernel(u_ref, v_ref, w_ref, bt_ref, gn_ref, o_ref):
    gv = _rms(v_ref[...].astype(F32), gn_ref[...])
    c = CHUNK
    row = lax.broadcasted_iota(jnp.int32, (c, c), 0)
    col = lax.broadcasted_iota(jnp.int32, (c, c), 1)
    causal = col <= row
    for h in range(GM_HEADS):
        cs = slice(h * GM_HEAD_DIM, (h + 1) * GM_HEAD_DIM)
        w = jnp.where(causal, w_ref[h], 0.0).astype(BF16)
        mixed = jnp.dot(w, gv[:, cs].astype(BF16), preferred_element_type=F32)
        mixed = mixed + bt_ref[:, h:h + 1]
        o_ref[:, cs] = (u_ref[:, cs].astype(F32) * mixed).astype(o_ref.dtype)


def gmlp_prompt(gm, w_s, b_t, g_norm, *, m_total, m_prompt):
    c = CHUNK
    return pl.pallas_call(
        _gmlp_prompt_kernel,
        grid=(m_prompt // c,),
        in_specs=[
            pl.BlockSpec((c, GM_WIDTH), lambda r: (r, 0)),
            pl.BlockSpec((c, GM_WIDTH), lambda r: (r, 1)),
            pl.BlockSpec((GM_HEADS, c, c), lambda r: (0, 0, 0)),
            pl.BlockSpec((c, GM_HEADS), lambda r: (0, 0)),
            pl.BlockSpec((1, GM_WIDTH), lambda r: (0, 0)),
        ],
        out_specs=pl.BlockSpec((c, GM_WIDTH), lambda r: (r, 0)),
        out_shape=jax.ShapeDtypeStruct((m_total, GM_WIDTH), BF16),
        compiler_params=_params(("parallel",), 40),
        name="gmlp_prompt",
    )(gm, gm, w_s, b_t, g_norm)


def _gmlp_sample_kernel(u_ref, v_ref, wc_ref, bc_ref, gn_ref, o_ref, gv_ref, *, t_new):
    rows = u_ref.shape[0]
    nseq = rows // t_new
    gv = _rms(v_ref[...].astype(F32), gn_ref[...])
    gv_ref[...] = gv
    t = lax.broadcasted_iota(jnp.int32, (t_new, GM_HEAD_DIM), 0)
    for h in range(GM_HEADS):
        cs = slice(h * GM_HEAD_DIM, (h + 1) * GM_HEAD_DIM)
        x = gv[:, cs].reshape(nseq, t_new, GM_HEAD_DIM)
        acc = jnp.zeros((nseq, t_new, GM_HEAD_DIM), F32)
        for s in range(t_new):
            coef = jnp.where(t >= s, wc_ref[h, s], 0.0)
            acc = acc + coef[None] * x[:, s:s + 1, :]
        mixed = acc + bc_ref[h][None]
        u = u_ref[:, cs].astype(F32).reshape(nseq, t_new, GM_HEAD_DIM)
        o_ref[:, cs] = (u * mixed).reshape(rows, GM_HEAD_DIM).astype(o_ref.dtype)


def gmlp_sample(gm, w_s, b_s, g_norm, *, m_prompt, m_sample, t_new):
    lanes = GM_HEAD_DIM
    w8 = w_s[:, :t_new, :t_new]
    wc = jnp.broadcast_to(
        w8.transpose(0, 2, 1)[..., None], (GM_HEADS, t_new, t_new, lanes)
    )
    bc = jnp.broadcast_to(b_s[:, :t_new, None], (GM_HEADS, t_new, lanes))
    blk = m_prompt // m_sample
    return pl.pallas_call(
        functools.partial(_gmlp_sample_kernel, t_new=t_new),
        grid=(1,),
        in_specs=[
            pl.BlockSpec((m_sample, GM_WIDTH), lambda i: (blk, 0)),
            pl.BlockSpec((m_sample, GM_WIDTH), lambda i: (blk, 1)),
            pl.BlockSpec((GM_HEADS, t_new, t_new, lanes), lambda i: (0, 0, 0, 0)),
            pl.BlockSpec((GM_HEADS, t_new, lanes), lambda i: (0, 0, 0)),
            pl.BlockSpec((1, GM_WIDTH), lambda i: (0, 0)),
        ],
        out_specs=[
            pl.BlockSpec((m_sample, GM_WIDTH), lambda i: (0, 0)),
            pl.BlockSpec((m_sample, GM_WIDTH), lambda i: (0, 0)),
        ],
        out_shape=[
            jax.ShapeDtypeStruct((m_sample, GM_WIDTH), BF16),
            jax.ShapeDtypeStruct((m_sample, GM_WIDTH), F32),
        ],
        compiler_params=_params(("arbitrary",), 40),
        name="gmlp_sample",
    )(gm, gm, wc, bc, g_norm)


def _conv_gelu_gate(g, g1, g2, up, cw_ref, cb_ref):
    c = cb_ref[...] + cw_ref[0:1, :] * g2
    c = c + cw_ref[1:2, :] * g1
    c = c + cw_ref[2:3, :] * g
    return _gelu(c) * up


def _ffn_a_prompt_kernel(h_ref, wg_ref, wu_ref, cw_ref, cb_ref, a_ref, tail_ref,
                         gpad_ref, carry_ref, *, tiles_per_seq):
    i = pl.program_id(0)
    j = pl.program_id(1)
    tm = h_ref.shape[0]
    h = h_ref[...]
    g = jnp.dot(h, wg_ref[...], preferred_element_type=F32)
    up = jnp.dot(h, wu_ref[...], preferred_element_type=F32)

    @pl.when(i % tiles_per_seq == 0)
    def _():
        gpad_ref[0:8, :] = jnp.zeros((8, g.shape[1]), F32)

    @pl.when(i % tiles_per_seq != 0)
    def _():
        gpad_ref[0:8, :] = carry_ref[j]

    gpad_ref[8:, :] = g
    last = g[tm - 8:, :]
    carry_ref[j] = last
    tail_ref[...] = last
    g1 = gpad_ref[7:7 + tm, :]
    g2 = gpad_ref[6:6 + tm, :]
    a_ref[...] = _conv_gelu_gate(g, g1, g2, up, cw_ref, cb_ref).astype(a_ref.dtype)


def ffn_a_prompt(h, w_gate, w_up, conv_w, conv_b, layer, *, m_prompt, seq, tm, tf):
    m_total, d = h.shape
    f = w_gate.shape[-1]
    nf = f // tf
    ni = m_prompt // tm
    return pl.pallas_call(
        functools.partial(_ffn_a_prompt_kernel, tiles_per_seq=seq // tm),
        grid=(ni, nf),
        in_specs=[
            pl.BlockSpec((tm, d), lambda i, j: (i, 0)),
            pl.BlockSpec((None, d, tf), lambda i, j: (layer, 0, j)),
            pl.BlockSpec((None, d, tf), lambda i, j: (layer, 0, j)),
            pl.BlockSpec((None, CONV_W, tf), lambda i, j: (layer, 0, j)),
            pl.BlockSpec((None, 1, tf), lambda i, j: (layer, 0, j)),
        ],
        out_specs=[
            pl.BlockSpec((tm, tf), lambda i, j: (i, j)),
            pl.BlockSpec((None, 8, tf), lambda i, j: (i, 0, j)),
        ],
        out_shape=[
            jax.ShapeDtypeStruct((m_total, f), BF16),
            jax.ShapeDtypeStruct((ni, 8, f), F32),
        ],
        scratch_shapes=[
            pltpu.VMEM((8 + tm, tf), F32),
            pltpu.VMEM((nf, 8, tf), F32),
        ],
        compiler_params=_params(("arbitrary", "arbitrary"), 56),
        name="ffn_a_prompt",
    )(h, w_gate, w_up, conv_w, conv_b)


def _ffn_a_sample_kernel(h_ref, wg_ref, wu_ref, cw_ref, cb_ref, st1_ref, st2_ref,
                         abuf_ref, a_ref, g_ref, gpad_ref, *, t_new):
    del abuf_ref
    tm = h_ref.shape[0]
    h = h_ref[...]
    g = jnp.dot(h, wg_ref[...], preferred_element_type=F32)
    up = jnp.dot(h, wu_ref[...], preferred_element_type=F32)
    g_ref[...] = g
    gpad_ref[0:8, :] = jnp.zeros((8, g.shape[1]), F32)
    gpad_ref[8:, :] = g
    t = lax.broadcasted_iota(jnp.int32, g.shape, 0) % t_new
    g1 = jnp.where(t >= 1, gpad_ref[7:7 + tm, :], st1_ref[...])
    g2 = jnp.where(t >= 2, gpad_ref[6:6 + tm, :], st2_ref[...])
    a_ref[...] = _conv_gelu_gate(g, g1, g2, up, cw_ref, cb_ref).astype(a_ref.dtype)


def ffn_a_sample(h, a_buf, w_gate, w_up, conv_w, conv_b, conv_state, layer, *,
                 m_prompt, m_sample, t_new, tf):
    m_total, d = h.shape
    f = w_gate.shape[-1]
    b = conv_state.shape[0]
    blk = m_prompt // m_sample
    zeros = jnp.zeros((b, t_new, f), F32)
    st1 = zeros.at[:, 0].set(conv_state[:, 1]).reshape(m_sample, f)
    st2 = zeros.at[:, 0].set(conv_state[:, 0]).at[:, 1].set(conv_state[:, 1])
    st2 = st2.reshape(m_sample, f)
    return pl.pallas_call(
        functools.partial(_ffn_a_sample_kernel, t_new=t_new),
        grid=(f // tf,),
        in_specs=[
            pl.BlockSpec((m_sample, d), lambda j: (blk, 0)),
            pl.BlockSpec((None, d, tf), lambda j: (layer, 0, j)),
            pl.BlockSpec((None, d, tf), lambda j: (layer, 0, j)),
            pl.BlockSpec((None, CONV_W, tf), lambda j: (layer, 0, j)),
            pl.BlockSpec((None, 1, tf), lambda j: (layer, 0, j)),
            pl.BlockSpec((m_sample, tf), lambda j: (0, j)),
            pl.BlockSpec((m_sample, tf), lambda j: (0, j)),
            pl.BlockSpec(memory_space=pl.ANY),
        ],
        out_specs=[
            pl.BlockSpec((m_sample, tf), lambda j: (blk, j)),
            pl.BlockSpec((m_sample, tf), lambda j: (0, j)),
        ],
        out_shape=[
            jax.ShapeDtypeStruct((m_total, f), BF16),
            jax.ShapeDtypeStruct((m_sample, f), F32),
        ],
        scratch_shapes=[pltpu.VMEM((8 + m_sample, tf), F32)],
        input_output_aliases={7: 0},
        compiler_params=_params(("arbitrary",), 40),
        name="ffn_a_sample",
    )(h, w_gate, w_up, conv_w, conv_b, st1, st2, a_buf)


def kernel(x_prompt, x_sample, state_swa_k, state_swa_v, state_conv, w_in, w_out,
           attn_sinks, gm_spatial, gm_bias, gm_norm, norm_pre_mix, norm_post_mix,
           norm_pre_ffn, norm_post_ffn, w_ffn_gate, w_ffn_up, w_ffn_down, conv_w,
           conv_b):
    batch, seq, d = x_prompt.shape
    dec_batch, dec_seq, _ = x_sample.shape
    depth = w_in.shape[0]
    f = w_ffn_gate.shape[-1]
    mp = batch * seq
    ms = dec_batch * dec_seq
    m = mp + ms
    keep = min(WINDOW, seq)

    tm_mm = m // 8
    tm_down = m // 11
    tm_ffn = 1024
    tf = 256
    tm_rows = m // 44

    w_in_b = w_in.astype(BF16)
    w_out_b = w_out.astype(BF16).reshape(depth, 2, Q_COLS, d)
    w_gate_b = w_ffn_gate.astype(BF16)
    w_up_b = w_ffn_up.astype(BF16)
    w_down_b = w_ffn_down.astype(BF16)
    conv_b3 = conv_b.reshape(depth, 1, f)

    x = jnp.concatenate([x_prompt.reshape(mp, d), x_sample.reshape(ms, d)], axis=0)
    h = rms_rows(x, norm_pre_mix[0].reshape(1, d), tm=tm_rows)

    p_k, p_v, p_c, s_k, s_v, s_c, s_g = [], [], [], [], [], [], []
    for l in range(depth):
        tn = 512
        q = matmul_cols(h, w_in_b, l, col_block0=0, n=Q_COLS, tm=tm_mm, tn=tn,
                        out_dtype=BF16, epilogue=_scale_q, name="mm_q")
        kv = matmul_cols(h, w_in_b, l, col_block0=Q_COLS // tn, n=2 * KV_COLS,
                         tm=tm_mm, tn=tn, out_dtype=F32, epilogue=_identity,
                         name="mm_kv")
        gm = matmul_cols(h, w_in_b, l, col_block0=(Q_COLS + 2 * KV_COLS) // tn,
                         n=2 * GM_WIDTH, tm=tm_mm, tn=tn, out_dtype=BF16,
                         epilogue=_gelu, name="mm_gm")

        k_all, v_all = kv[:, :KV_COLS], kv[:, KV_COLS:]
        p_k.append(k_all[:mp].reshape(batch, seq, N_KV_HEADS, HEAD_DIM)[:, seq - keep:])
        p_v.append(v_all[:mp].reshape(batch, seq, N_KV_HEADS, HEAD_DIM)[:, seq - keep:])
        s_k.append(k_all[mp:].reshape(dec_batch, dec_seq, N_KV_HEADS, HEAD_DIM))
        s_v.append(v_all[mp:].reshape(dec_batch, dec_seq, N_KV_HEADS, HEAD_DIM))

        att = attn_prompt(q, kv, attn_sinks[l], m_total=m, m_prompt=mp, seq=seq)
        att_s = attn_sample(q[mp:], k_all[mp:], v_all[mp:], state_swa_k[l],
                            state_swa_v[l], attn_sinks[l], nb=8)
        att = att.at[mp:].set(att_s)

        gmo = gmlp_prompt(gm, gm_spatial[l], gm_bias[l].T, gm_norm[l].reshape(1, -1),
                          m_total=m, m_prompt=mp)
        gmo_s, gv_s = gmlp_sample(gm, gm_spatial[l], gm_bias[l],
                                  gm_norm[l].reshape(1, -1), m_prompt=mp, m_sample=ms,
                                  t_new=dec_seq)
        gmo = gmo.at[mp:].set(gmo_s)
        s_g.append(gv_s.reshape(dec_batch, dec_seq, GM_HEADS, GM_HEAD_DIM))

        mix = matmul_two(att, gmo, w_out_b, l, tm=tm_mm, tn=512, out_dtype=BF16,
                         name="mm_out")
        x, h = post_rows(mix, x, norm_post_mix[l].reshape(1, d),
                         norm_pre_ffn[l].reshape(1, d), tm=tm_rows)

        a, tail = ffn_a_prompt(h, w_gate_b, w_up_b, conv_w, conv_b3, l, m_prompt=mp,
                               seq=seq, tm=tm_ffn, tf=tf)
        a, g_s = ffn_a_sample(h, a, w_gate_b, w_up_b, conv_w, conv_b3, state_conv[l], l,
                              m_prompt=mp, m_sample=ms, t_new=dec_seq, tf=tf)
        tiles_per_seq = seq // tm_ffn
        p_c.append(tail[tiles_per_seq - 1::tiles_per_seq, 8 - (CONV_W - 1):])
        s_c.append(g_s.reshape(dec_batch, dec_seq, f)[:, dec_seq - (CONV_W - 1):])

        ffn = matmul_cols(a, w_down_b, l, col_block0=0, n=d, tm=tm_down, tn=256,
                          out_dtype=BF16, epilogue=_identity, name="mm_down")
        if l + 1 < depth:
            x, h = post_rows(ffn, x, norm_post_ffn[l].reshape(1, d),
                             norm_pre_mix[l + 1].reshape(1, d), tm=tm_rows)
        else:
            x = post_rows(ffn, x, norm_post_ffn[l].reshape(1, d), None, tm=tm_rows)

    return (x[:mp].reshape(batch, seq, d), x[mp:].reshape(dec_batch, dec_seq, d),
            jnp.stack(p_k), jnp.stack(p_v), jnp.stack(p_c),
            jnp.stack(s_k), jnp.stack(s_v), jnp.stack(s_c), jnp.stack(s_g))
```

```python
import functools

import jax
import jax.numpy as jnp
from jax import lax
from jax.experimental import pallas as pl
from jax.experimental.pallas import tpu as pltpu

F32 = jnp.float32
BF16 = jnp.bfloat16

D_MODEL = 4096
HEAD_DIM = 64
N_HEADS = 32
N_KV_HEADS = 4
GQA_GROUP = 8
WINDOW = 128
CHUNK = 128
GM_HEAD_DIM = 128
GM_HEADS = 16
GM_WIDTH = 2048
Q_COLS = 2048
KV_COLS = 256
CONV_W = 3
EPS = 1e-6

MIB = 1024 * 1024


def _params(semantics, vmem_mib):
    return pltpu.CompilerParams(
        dimension_semantics=semantics, vmem_limit_bytes=vmem_mib * MIB
    )


def _rms(x, g):
    ms = jnp.mean(x * x, axis=-1, keepdims=True)
    return x * lax.rsqrt(ms + EPS) * g


def _split_specs(tm, d, n_prompt_tiles):
    return [
        pl.BlockSpec((tm, d), lambda i: (jnp.minimum(i, n_prompt_tiles - 1), 0)),
        pl.BlockSpec((tm, d), lambda i: (0, 0)),
    ]


def _load_rows(refs, n_prompt_tiles):
    if len(refs) == 1:
        return refs[0][...]
    is_prompt = pl.program_id(0) < n_prompt_tiles
    return jnp.where(is_prompt, refs[0][...], refs[1][...])


def _store_rows(refs, value, n_prompt_tiles):
    if len(refs) == 1:
        refs[0][...] = value
        return

    @pl.when(pl.program_id(0) < n_prompt_tiles)
    def _():
        refs[0][...] = value

    @pl.when(pl.program_id(0) >= n_prompt_tiles)
    def _():
        refs[1][...] = value


def _rms_kernel(*refs, n_in, n_prompt_tiles):
    x = _load_rows(refs[:n_in], n_prompt_tiles)
    g_ref, o_ref = refs[n_in:]
    o_ref[...] = _rms(x, g_ref[...]).astype(o_ref.dtype)


def rms_rows(x_parts, g, *, tm):
    d = x_parts[0].shape[1]
    m = sum(p.shape[0] for p in x_parts)
    npt = x_parts[0].shape[0] // tm
    row = pl.BlockSpec((tm, d), lambda i: (i, 0))
    x_specs = [row] if len(x_parts) == 1 else _split_specs(tm, d, npt)
    return pl.pallas_call(
        functools.partial(_rms_kernel, n_in=len(x_parts), n_prompt_tiles=npt),
        grid=(m // tm,),
        in_specs=x_specs + [pl.BlockSpec((1, d), lambda i: (0, 0))],
        out_specs=row,
        out_shape=jax.ShapeDtypeStruct((m, d), BF16),
        compiler_params=_params(("arbitrary",), 40),
        name="rms_rows",
    )(*x_parts, g)


def _post_kernel(*refs, n_in, n_out, has_next, n_prompt_tiles):
    m_ref = refs[0]
    x = _load_rows(refs[1:1 + n_in], n_prompt_tiles)
    rest = refs[1 + n_in:]
    xn = x + _rms(m_ref[...].astype(F32), rest[0][...])
    rest = rest[1:]
    if has_next:
        gnext_ref, rest = rest[0], rest[1:]
    _store_rows(rest[:n_out], xn, n_prompt_tiles)
    if has_next:
        ho_ref = rest[n_out]
        ho_ref[...] = _rms(xn, gnext_ref[...]).astype(ho_ref.dtype)


def post_rows(m_in, x_parts, g_post, g_next, *, tm, split_out=None):
    m, d = m_in.shape
    npt = (x_parts[0].shape[0] if split_out is None else split_out[0]) // tm
    if len(x_parts) == 1 and split_out is None:
        npt = m // tm
    row = pl.BlockSpec((tm, d), lambda i: (i, 0))
    vec = pl.BlockSpec((1, d), lambda i: (0, 0))
    x_specs = [row] if len(x_parts) == 1 else _split_specs(tm, d, npt)
    if split_out is None:
        xo_specs = [row]
        xo_shapes = [jax.ShapeDtypeStruct((m, d), F32)]
    else:
        xo_specs = _split_specs(tm, d, npt)
        xo_shapes = [jax.ShapeDtypeStruct((r, d), F32) for r in split_out]
    has_next = g_next is not None
    gains = [g_post, g_next] if has_next else [g_post]
    out = pl.pallas_call(
        functools.partial(_post_kernel, n_in=len(x_parts), n_out=len(xo_specs),
                          has_next=has_next, n_prompt_tiles=npt),
        grid=(m // tm,),
        in_specs=[row] + x_specs + [vec] * len(gains),
        out_specs=xo_specs + ([row] if has_next else []),
        out_shape=xo_shapes
        + ([jax.ShapeDtypeStruct((m, d), BF16)] if has_next else []),
        compiler_params=_params(("arbitrary",), 48),
        name="post_rows",
    )(m_in, *x_parts, *gains)
    x_new = out[:len(xo_specs)]
    return (x_new, out[-1]) if has_next else (x_new, None)


def _mm_kernel(a_ref, b_ref, o_ref, *, epilogue):
    acc = lax.dot_general(a_ref[...], b_ref[...], (((1,), (0,)), ((), ())),
                          preferred_element_type=F32)
    o_ref[...] = epilogue(acc).astype(o_ref.dtype)


def _mm2_kernel(a1_ref, a2_ref, b1_ref, b2_ref, o_ref):
    acc = jnp.dot(a1_ref[...], b1_ref[...], preferred_element_type=F32)
    acc = acc + jnp.dot(a2_ref[...], b2_ref[...], preferred_element_type=F32)
    o_ref[...] = acc.astype(o_ref.dtype)


def matmul_cols(a, w, layer, *, col_block0, n, tm, tn, out_dtype, epilogue, name):
    m, k = a.shape
    return pl.pallas_call(
        functools.partial(_mm_kernel, epilogue=epilogue),
        grid=(m // tm, n // tn),
        in_specs=[
            pl.BlockSpec((tm, k), lambda i, j: (i, 0)),
            pl.BlockSpec((None, k, tn), lambda i, j: (layer, 0, j + col_block0)),
        ],
        out_specs=pl.BlockSpec((tm, tn), lambda i, j: (i, j)),
        out_shape=jax.ShapeDtypeStruct((m, n), out_dtype),
        compiler_params=_params(("parallel", "arbitrary"), 56),
        name=name,
    )(a, w)


def matmul_two(a1, a2, w, layer, *, tm, tn, out_dtype, name):
    m, k1 = a1.shape
    k2 = a2.shape[1]
    assert k1 == k2
    n = w.shape[-1]
    return pl.pallas_call(
        _mm2_kernel,
        grid=(m // tm, n // tn),
        in_specs=[
            pl.BlockSpec((tm, k1), lambda i, j: (i, 0)),
            pl.BlockSpec((tm, k2), lambda i, j: (i, 0)),
            pl.BlockSpec((None, None, k1, tn), lambda i, j: (layer, 0, 0, j)),
            pl.BlockSpec((None, None, k2, tn), lambda i, j: (layer, 1, 0, j)),
        ],
        out_specs=pl.BlockSpec((tm, tn), lambda i, j: (i, j)),
        out_shape=jax.ShapeDtypeStruct((m, n), out_dtype),
        compiler_params=_params(("parallel", "arbitrary"), 56),
        name=name,
    )(a1, a2, w, w)


def _identity(x):
    return x


def _scale_q(x):
    return x * (HEAD_DIM ** -0.5)


def _gelu(x):
    return jax.nn.gelu(x, approximate=True)


def _softmax_with_sink(s, sink):
    m = jnp.maximum(jnp.max(s, axis=-1, keepdims=True), sink)
    e = jnp.exp(s - m)
    den = jnp.sum(e, axis=-1, keepdims=True) + jnp.exp(sink - m)
    return e / den


def _attn_prompt_kernel(sink_ref, q_ref, kvc_ref, kvp_ref, o_ref, *, blocks_per_seq):
    n = pl.program_id(0) % blocks_per_seq
    w = WINDOW
    row = lax.broadcasted_iota(jnp.int32, (w, 2 * w), 0)
    col = lax.broadcasted_iota(jnp.int32, (w, 2 * w), 1)
    prev_ok = jnp.logical_and(jnp.logical_and(col < w, col > row), n > 0)
    own_ok = jnp.logical_and(col >= w, col - w <= row)
    valid = jnp.logical_or(prev_ok, own_ok)
    for kvh in range(N_KV_HEADS):
        ks = slice(kvh * HEAD_DIM, (kvh + 1) * HEAD_DIM)
        vs = slice(KV_COLS + kvh * HEAD_DIM, KV_COLS + (kvh + 1) * HEAD_DIM)
        k = jnp.concatenate([kvp_ref[:, ks], kvc_ref[:, ks]], axis=0).astype(BF16)
        v = jnp.concatenate([kvp_ref[:, vs], kvc_ref[:, vs]], axis=0).astype(BF16)
        heads = [kvh * GQA_GROUP + g for g in range(GQA_GROUP)]
        qs = jnp.concatenate(
            [q_ref[:, h * HEAD_DIM:(h + 1) * HEAD_DIM] for h in heads], axis=0
        )
        s = lax.dot_general(
            qs, k, (((1,), (1,)), ((), ())), preferred_element_type=F32
        )
        ps = []
        for g, h in enumerate(heads):
            sg = jnp.where(valid, s[g * w:(g + 1) * w], -jnp.inf)
            ps.append(_softmax_with_sink(sg, sink_ref[h]).astype(BF16))
        o = jnp.dot(jnp.concatenate(ps, axis=0), v, preferred_element_type=F32)
        for g, h in enumerate(heads):
            o_ref[:, h * HEAD_DIM:(h + 1) * HEAD_DIM] = o[g * w:(g + 1) * w].astype(
                o_ref.dtype
            )


def attn_prompt(q, kv, sinks, *, m_total, m_prompt, seq):
    w = WINDOW
    nblk = m_prompt // w
    return pl.pallas_call(
        functools.partial(_attn_prompt_kernel, blocks_per_seq=seq // w),
        grid=(nblk,),
        in_specs=[
            pl.BlockSpec(memory_space=pltpu.SMEM),
            pl.BlockSpec((w, Q_COLS), lambda r: (r, 0)),
            pl.BlockSpec((w, 2 * KV_COLS), lambda r: (r, 0)),
            pl.BlockSpec((w, 2 * KV_COLS), lambda r: (jnp.maximum(r - 1, 0), 0)),
        ],
        out_specs=pl.BlockSpec((w, Q_COLS), lambda r: (r, 0)),
        out_shape=jax.ShapeDtypeStruct((m_total, Q_COLS), BF16),
        compiler_params=_params(("parallel",), 40),
        name="attn_prompt",
    )(sinks, q, kv, kv)


def _attn_sample_kernel(sink_ref, q_ref, k_ref, v_ref, o_ref, *, t_new):
    nb, nkv, rows, d = q_ref.shape
    keys = k_ref.shape[2]
    q = q_ref[...].reshape(nb * nkv, rows, d)
    k = k_ref[...].reshape(nb * nkv, keys, d)
    v = v_ref[...].reshape(nb * nkv, keys, d)
    s = jnp.einsum("bqd,bkd->bqk", q, k, preferred_element_type=F32)
    t = lax.broadcasted_iota(jnp.int32, (rows, keys), 0) // GQA_GROUP
    col = lax.broadcasted_iota(jnp.int32, (rows, keys), 1)
    valid = jnp.logical_and(col > t, col <= WINDOW + t)
    s = jnp.where(valid[None], s, -jnp.inf)
    sink = jnp.broadcast_to(sink_ref[...][None], (nb, nkv, rows, 1)).reshape(
        nb * nkv, rows, 1
    )
    p = _softmax_with_sink(s, sink).astype(BF16)
    o = jnp.einsum("bqk,bkd->bqd", p, v, preferred_element_type=F32)
    o_ref[...] = o.reshape(nb, nkv, rows, d).astype(o_ref.dtype)


def attn_sample(q_s, k_new, v_new, k_buf, v_buf, sinks, *, nb):
    b, l = k_buf.shape[0], k_buf.shape[1]
    t_new = q_s.shape[0] // b
    rows = t_new * GQA_GROUP
    keys = 2 * WINDOW
    q4 = q_s.reshape(b, t_new, N_KV_HEADS, GQA_GROUP, HEAD_DIM)
    q4 = q4.transpose(0, 2, 1, 3, 4).reshape(b, N_KV_HEADS, rows, HEAD_DIM)

    def keys_layout(buf, new):
        cat = jnp.concatenate(
            [buf, new.reshape(b, t_new, N_KV_HEADS, HEAD_DIM)], axis=1
        )
        cat = jnp.pad(cat, ((0, 0), (0, keys - l - t_new), (0, 0), (0, 0)))
        return cat.transpose(0, 2, 1, 3).astype(BF16)

    k4 = keys_layout(k_buf, k_new)
    v4 = keys_layout(v_buf, v_new)
    sink_rows = jnp.broadcast_to(
        sinks.reshape(N_KV_HEADS, 1, GQA_GROUP), (N_KV_HEADS, t_new, GQA_GROUP)
    ).reshape(N_KV_HEADS, rows, 1)
    o4 = pl.pallas_call(
        functools.partial(_attn_sample_kernel, t_new=t_new),
        grid=(b // nb,),
        in_specs=[
            pl.BlockSpec((N_KV_HEADS, rows, 1), lambda i: (0, 0, 0)),
            pl.BlockSpec((nb, N_KV_HEADS, rows, HEAD_DIM), lambda i: (i, 0, 0, 0)),
            pl.BlockSpec((nb, N_KV_HEADS, keys, HEAD_DIM), lambda i: (i, 0, 0, 0)),
            pl.BlockSpec((nb, N_KV_HEADS, keys, HEAD_DIM), lambda i: (i, 0, 0, 0)),
        ],
        out_specs=pl.BlockSpec(
            (nb, N_KV_HEADS, rows, HEAD_DIM), lambda i: (i, 0, 0, 0)
        ),
        out_shape=jax.ShapeDtypeStruct((b, N_KV_HEADS, rows, HEAD_DIM), BF16),
        compiler_params=_params(("parallel",), 40),
        name="attn_sample",
    )(sink_rows, q4, k4, v4)
    o = o4.reshape(b, N_KV_HEADS, t_new, GQA_GROUP, HEAD_DIM).transpose(0, 2, 1, 3, 4)
    return o.reshape(b * t_new, Q_COLS)


def _place_rows_kernel(rows_ref, buf_ref, o_ref):
    del buf_ref
    o_ref[...] = rows_ref[...]


def place_rows(buf, rows, *, row_block):
    r, c = rows.shape
    return pl.pallas_call(
        _place_rows_kernel,
        grid=(1,),
        in_specs=[
            pl.BlockSpec((r, c), lambda i: (0, 0)),
            pl.BlockSpec(memory_space=pl.ANY),
        ],
        out_specs=pl.BlockSpec((r, c), lambda i: (row_block, 0)),
        out_shape=jax.ShapeDtypeStruct(buf.shape, buf.dtype),
        input_output_aliases={1: 0},
        compiler_params=_params(("arbitrary",), 16),
        name="place_rows",
    )(rows, buf)


def _gmlp_prompt_kernel(u_ref, v_ref, w_ref, bt_ref, gn_ref, o_ref):
    gv = _rms(v_ref[...].astype(F32), gn_ref[...])
    c = CHUNK
    row = lax.broadcasted_iota(jnp.int32, (c, c), 0)
    col = lax.broadcasted_iota(jnp.int32, (c, c), 1)
    causal = col <= row
    for h in range(GM_HEADS):
        cs = slice(h * GM_HEAD_DIM, (h + 1) * GM_HEAD_DIM)
        w = jnp.where(causal, w_ref[h], 0.0).astype(BF16)
        mixed = jnp.dot(w, gv[:, cs].astype(BF16), preferred_element_type=F32)
        mixed = mixed + bt_ref[:, h:h + 1]
        o_ref[:, cs] = (u_ref[:, cs].astype(F32) * mixed).astype(o_ref.dtype)


def gmlp_prompt(gm, w_s, b_t, g_norm, *, m_total, m_prompt):
    c = CHUNK
    return pl.pallas_call(
        _gmlp_prompt_kernel,
        grid=(m_prompt // c,),
        in_specs=[
            pl.BlockSpec((c, GM_WIDTH), lambda r: (r, 0)),
            pl.BlockSpec((c, GM_WIDTH), lambda r: (r, 1)),
            pl.BlockSpec((GM_HEADS, c, c), lambda r: (0, 0, 0)),
            pl.BlockSpec((c, GM_HEADS), lambda r: (0, 0)),
            pl.BlockSpec((1, GM_WIDTH), lambda r: (0, 0)),
        ],
        out_specs=pl.BlockSpec((c, GM_WIDTH), lambda r: (r, 0)),
        out_shape=jax.ShapeDtypeStruct((m_total, GM_WIDTH), BF16),
        compiler_params=_params(("parallel",), 40),
        name="gmlp_prompt",
    )(gm, gm, w_s, b_t, g_norm)


def _gmlp_sample_kernel(u_ref, v_ref, wc_ref, bc_ref, gn_ref, obuf_ref, o_ref, gv_ref, *,
                        t_new):
    del obuf_ref
    rows = u_ref.shape[0]
    nseq = rows // t_new
    gv = _rms(v_ref[...].astype(F32), gn_ref[...])
    gv_ref[...] = gv
    t = lax.broadcasted_iota(jnp.int32, (t_new, GM_HEAD_DIM), 0)
    for h in range(GM_HEADS):
        cs = slice(h * GM_HEAD_DIM, (h + 1) * GM_HEAD_DIM)
        x = gv[:, cs].reshape(nseq, t_new, GM_HEAD_DIM)
        acc = jnp.zeros((nseq, t_new, GM_HEAD_DIM), F32)
        for s in range(t_new):
            coef = jnp.where(t >= s, wc_ref[h, s], 0.0)
            acc = acc + coef[None] * x[:, s:s + 1, :]
        mixed = acc + bc_ref[h][None]
        u = u_ref[:, cs].astype(F32).reshape(nseq, t_new, GM_HEAD_DIM)
        o_ref[:, cs] = (u * mixed).reshape(rows, GM_HEAD_DIM).astype(o_ref.dtype)


def gmlp_sample(gm, out_buf, w_s, b_s, g_norm, *, m_prompt, m_sample, t_new):
    lanes = GM_HEAD_DIM
    w8 = w_s[:, :t_new, :t_new]
    wc = jnp.broadcast_to(
        w8.transpose(0, 2, 1)[..., None], (GM_HEADS, t_new, t_new, lanes)
    )
    bc = jnp.broadcast_to(b_s[:, :t_new, None], (GM_HEADS, t_new, lanes))
    blk = m_prompt // m_sample
    return pl.pallas_call(
        functools.partial(_gmlp_sample_kernel, t_new=t_new),
        grid=(1,),
        in_specs=[
            pl.BlockSpec((m_sample, GM_WIDTH), lambda i: (blk, 0)),
            pl.BlockSpec((m_sample, GM_WIDTH), lambda i: (blk, 1)),
            pl.BlockSpec((GM_HEADS, t_new, t_new, lanes), lambda i: (0, 0, 0, 0)),
            pl.BlockSpec((GM_HEADS, t_new, lanes), lambda i: (0, 0, 0)),
            pl.BlockSpec((1, GM_WIDTH), lambda i: (0, 0)),
            pl.BlockSpec(memory_space=pl.ANY),
        ],
        out_specs=[
            pl.BlockSpec((m_sample, GM_WIDTH), lambda i: (blk, 0)),
            pl.BlockSpec((m_sample, GM_WIDTH), lambda i: (0, 0)),
        ],
        out_shape=[
            jax.ShapeDtypeStruct(out_buf.shape, BF16),
            jax.ShapeDtypeStruct((m_sample, GM_WIDTH), F32),
        ],
        input_output_aliases={5: 0},
        compiler_params=_params(("arbitrary",), 40),
        name="gmlp_sample",
    )(gm, gm, wc, bc, g_norm, out_buf)


def _conv_gelu_gate(g, g1, g2, up, cw_ref, cb_ref):
    c = cb_ref[...] + cw_ref[0:1, :] * g2
    c = c + cw_ref[1:2, :] * g1
    c = c + cw_ref[2:3, :] * g
    return _gelu(c) * up


def _ffn_a_prompt_kernel(h_ref, wg_ref, wu_ref, cw_ref, cb_ref, a_ref, tail_ref,
                         prev_ref, carry_ref, *, tiles_per_seq, chunk):
    i = pl.program_id(0)
    j = pl.program_id(1)
    tm = h_ref.shape[0]
    tf = wg_ref.shape[1]

    @pl.when(i % tiles_per_seq == 0)
    def _():
        prev_ref[...] = jnp.zeros((8, tf), F32)

    @pl.when(i % tiles_per_seq != 0)
    def _():
        prev_ref[...] = carry_ref[j]

    prev8 = prev_ref[...]
    wg = wg_ref[...].astype(BF16)
    wu = wu_ref[...].astype(BF16)
    for r0 in range(0, tm, chunk):
        h = h_ref[r0:r0 + chunk, :]
        g = jnp.dot(h, wg, preferred_element_type=F32)
        up = jnp.dot(h, wu, preferred_element_type=F32)
        ext = jnp.concatenate([prev8, g], axis=0)
        g1 = pltpu.roll(ext, 1, 0)[8:]
        g2 = pltpu.roll(ext, 2, 0)[8:]
        a_ref[r0:r0 + chunk, :] = _conv_gelu_gate(g, g1, g2, up, cw_ref, cb_ref).astype(
            a_ref.dtype
        )
        prev8 = g[chunk - 8:, :]
    carry_ref[j] = prev8
    tail_ref[...] = prev8


def ffn_a_prompt(h, w_gate, w_up, conv_w, conv_b, layer, *, m_prompt, seq, tm, tf,
                 chunk):
    m_total, d = h.shape
    f = w_gate.shape[-1]
    nf = f // tf
    ni = m_prompt // tm
    return pl.pallas_call(
        functools.partial(_ffn_a_prompt_kernel, tiles_per_seq=seq // tm, chunk=chunk),
        grid=(ni, nf),
        in_specs=[
            pl.BlockSpec((tm, d), lambda i, j: (i, 0), pipeline_mode=pl.Buffered(1)),
            pl.BlockSpec((None, d, tf), lambda i, j: (layer, 0, j)),
            pl.BlockSpec((None, d, tf), lambda i, j: (layer, 0, j)),
            pl.BlockSpec((None, CONV_W, tf), lambda i, j: (layer, 0, j)),
            pl.BlockSpec((None, 1, tf), lambda i, j: (layer, 0, j)),
        ],
        out_specs=[
            pl.BlockSpec((tm, tf), lambda i, j: (i, j)),
            pl.BlockSpec((None, 8, tf), lambda i, j: (i, 0, j)),
        ],
        out_shape=[
            jax.ShapeDtypeStruct((m_total, f), BF16),
            jax.ShapeDtypeStruct((ni, 8, f), F32),
        ],
        scratch_shapes=[
            pltpu.VMEM((8, tf), F32),
            pltpu.VMEM((nf, 8, tf), F32),
        ],
        compiler_params=_params(("arbitrary", "arbitrary"), 56),
        name="ffn_a_prompt",
    )(h, w_gate, w_up, conv_w, conv_b)


def _ffn_a_sample_kernel(h_ref, wg_ref, wu_ref, cw_ref, cb_ref, st1_ref, st2_ref,
                         abuf_ref, a_ref, g_ref, *, t_new):
    del abuf_ref
    h = h_ref[...]
    g = jnp.dot(h, wg_ref[...].astype(BF16), preferred_element_type=F32)
    up = jnp.dot(h, wu_ref[...].astype(BF16), preferred_element_type=F32)
    g_ref[...] = g
    t = lax.broadcasted_iota(jnp.int32, g.shape, 0) % t_new
    g1 = jnp.where(t >= 1, pltpu.roll(g, 1, 0), st1_ref[...])
    g2 = jnp.where(t >= 2, pltpu.roll(g, 2, 0), st2_ref[...])
    a_ref[...] = _conv_gelu_gate(g, g1, g2, up, cw_ref, cb_ref).astype(a_ref.dtype)


def ffn_a_sample(h, a_buf, w_gate, w_up, conv_w, conv_b, conv_state, layer, *,
                 m_prompt, m_sample, t_new, tf):
    m_total, d = h.shape
    f = w_gate.shape[-1]
    b = conv_state.shape[0]
    blk = m_prompt // m_sample
    zeros = jnp.zeros((b, t_new, f), F32)
    st1 = zeros.at[:, 0].set(conv_state[:, 1]).reshape(m_sample, f)
    st2 = zeros.at[:, 0].set(conv_state[:, 0]).at[:, 1].set(conv_state[:, 1])
    st2 = st2.reshape(m_sample, f)
    return pl.pallas_call(
        functools.partial(_ffn_a_sample_kernel, t_new=t_new),
        grid=(f // tf,),
        in_specs=[
            pl.BlockSpec((m_sample, d), lambda j: (blk, 0)),
            pl.BlockSpec((None, d, tf), lambda j: (layer, 0, j)),
            pl.BlockSpec((None, d, tf), lambda j: (layer, 0, j)),
            pl.BlockSpec((None, CONV_W, tf), lambda j: (layer, 0, j)),
            pl.BlockSpec((None, 1, tf), lambda j: (layer, 0, j)),
            pl.BlockSpec((m_sample, tf), lambda j: (0, j)),
            pl.BlockSpec((m_sample, tf), lambda j: (0, j)),
            pl.BlockSpec(memory_space=pl.ANY),
        ],
        out_specs=[
            pl.BlockSpec((m_sample, tf), lambda j: (blk, j)),
            pl.BlockSpec((m_sample, tf), lambda j: (0, j)),
        ],
        out_shape=[
            jax.ShapeDtypeStruct((m_total, f), BF16),
            jax.ShapeDtypeStruct((m_sample, f), F32),
        ],
        input_output_aliases={7: 0},
        compiler_params=_params(("arbitrary",), 40),
        name="ffn_a_sample",
    )(h, w_gate, w_up, conv_w, conv_b, st1, st2, a_buf)


def kernel(x_prompt, x_sample, state_swa_k, state_swa_v, state_conv, w_in, w_out,
           attn_sinks, gm_spatial, gm_bias, gm_norm, norm_pre_mix, norm_post_mix,
           norm_pre_ffn, norm_post_ffn, w_ffn_gate, w_ffn_up, w_ffn_down, conv_w,
           conv_b):
    batch, seq, d = x_prompt.shape
    dec_batch, dec_seq, _ = x_sample.shape
    depth = w_in.shape[0]
    f = w_ffn_gate.shape[-1]
    mp = batch * seq
    ms = dec_batch * dec_seq
    m = mp + ms
    keep = min(WINDOW, seq)

    tm_mm = m // 8
    tm_down = m // 11
    tm_ffn = seq
    tf = 256
    tm_rows = ms
    assert mp % tm_rows == 0

    w_in_b = w_in.astype(BF16)
    w_out_b = w_out.astype(BF16).reshape(depth, 2, Q_COLS, d)
    w_down_b = w_ffn_down.astype(BF16)
    conv_b3 = conv_b.reshape(depth, 1, f)

    x = (x_prompt.reshape(mp, d), x_sample.reshape(ms, d))
    h = rms_rows(x, norm_pre_mix[0].reshape(1, d), tm=tm_rows)

    p_k, p_v, p_c, s_k, s_v, s_c, s_g = [], [], [], [], [], [], []
    for l in range(depth):
        tn = 512
        q = matmul_cols(h, w_in_b, l, col_block0=0, n=Q_COLS, tm=tm_mm, tn=tn,
                        out_dtype=BF16, epilogue=_scale_q, name="mm_q")
        kv = matmul_cols(h, w_in_b, l, col_block0=Q_COLS // tn, n=2 * KV_COLS,
                         tm=tm_mm, tn=tn, out_dtype=F32, epilogue=_identity,
                         name="mm_kv")
        gm = matmul_cols(h, w_in_b, l, col_block0=(Q_COLS + 2 * KV_COLS) // tn,
                         n=2 * GM_WIDTH, tm=tm_mm, tn=tn, out_dtype=BF16,
                         epilogue=_gelu, name="mm_gm")

        k_all, v_all = kv[:, :KV_COLS], kv[:, KV_COLS:]
        p_k.append(k_all[:mp].reshape(batch, seq, N_KV_HEADS, HEAD_DIM)[:, seq - keep:])
        p_v.append(v_all[:mp].reshape(batch, seq, N_KV_HEADS, HEAD_DIM)[:, seq - keep:])
        s_k.append(k_all[mp:].reshape(dec_batch, dec_seq, N_KV_HEADS, HEAD_DIM))
        s_v.append(v_all[mp:].reshape(dec_batch, dec_seq, N_KV_HEADS, HEAD_DIM))

        att = attn_prompt(q, kv, attn_sinks[l], m_total=m, m_prompt=mp, seq=seq)
        att_s = attn_sample(q[mp:], k_all[mp:], v_all[mp:], state_swa_k[l],
                            state_swa_v[l], attn_sinks[l], nb=8)
        att = place_rows(att, att_s, row_block=mp // ms)

        gmo = gmlp_prompt(gm, gm_spatial[l], gm_bias[l].T, gm_norm[l].reshape(1, -1),
                          m_total=m, m_prompt=mp)
        gmo, gv_s = gmlp_sample(gm, gmo, gm_spatial[l], gm_bias[l],
                                gm_norm[l].reshape(1, -1), m_prompt=mp, m_sample=ms,
                                t_new=dec_seq)
        s_g.append(gv_s.reshape(dec_batch, dec_seq, GM_HEADS, GM_HEAD_DIM))

        mix = matmul_two(att, gmo, w_out_b, l, tm=tm_mm, tn=512, out_dtype=BF16,
                         name="mm_out")
        x, h = post_rows(mix, x, norm_post_mix[l].reshape(1, d),
                         norm_pre_ffn[l].reshape(1, d), tm=tm_rows)

        a, tail = ffn_a_prompt(h, w_ffn_gate, w_ffn_up, conv_w, conv_b3, l, m_prompt=mp,
                               seq=seq, tm=tm_ffn, tf=tf, chunk=512)
        a, g_s = ffn_a_sample(h, a, w_ffn_gate, w_ffn_up, conv_w, conv_b3, state_conv[l], l,
                              m_prompt=mp, m_sample=ms, t_new=dec_seq, tf=tf)
        tiles_per_seq = seq // tm_ffn
        p_c.append(tail[tiles_per_seq - 1::tiles_per_seq, 8 - (CONV_W - 1):])
        s_c.append(g_s.reshape(dec_batch, dec_seq, f)[:, dec_seq - (CONV_W - 1):])

        ffn = matmul_cols(a, w_down_b, l, col_block0=0, n=d, tm=tm_down, tn=256,
                          out_dtype=BF16, epilogue=_identity, name="mm_down")
        last = l + 1 == depth
        x, h = post_rows(ffn, x, norm_post_ffn[l].reshape(1, d),
                         None if last else norm_pre_mix[l + 1].reshape(1, d),
                         tm=tm_rows, split_out=(mp, ms) if last else None)

    x_p, x_s = x
    return (x_p.reshape(batch, seq, d), x_s.reshape(dec_batch, dec_seq, d),
            jnp.stack(p_k), jnp.stack(p_v), jnp.stack(p_c),
            jnp.stack(s_k), jnp.stack(s_v), jnp.stack(s_c), jnp.stack(s_g))
```

```python
import functools

import jax
import jax.numpy as jnp
from jax import lax
from jax.experimental import pallas as pl
from jax.experimental.pallas import tpu as pltpu

F32 = jnp.float32
BF16 = jnp.bfloat16

D_MODEL = 4096
HEAD_DIM = 64
N_HEADS = 32
N_KV_HEADS = 4
GQA_GROUP = 8
WINDOW = 128
CHUNK = 128
GM_HEAD_DIM = 128
GM_HEADS = 16
GM_WIDTH = 2048
Q_COLS = 2048
KV_COLS = 256
CONV_W = 3
EPS = 1e-6
LANES = 128

MIB = 1024 * 1024


def _params(semantics, vmem_mib):
    return pltpu.CompilerParams(
        dimension_semantics=semantics, vmem_limit_bytes=vmem_mib * MIB
    )


def _rms(x, g):
    ms = jnp.mean(x * x, axis=-1, keepdims=True)
    return x * lax.rsqrt(ms + EPS) * g


def _split_specs(tm, d, n_prompt_tiles):
    return [
        pl.BlockSpec((tm, d), lambda i: (jnp.minimum(i, n_prompt_tiles - 1), 0)),
        pl.BlockSpec((tm, d), lambda i: (0, 0)),
    ]


def _load_rows(refs, n_prompt_tiles):
    if len(refs) == 1:
        return refs[0][...]
    is_prompt = pl.program_id(0) < n_prompt_tiles
    return jnp.where(is_prompt, refs[0][...], refs[1][...])


def _store_rows(refs, value, n_prompt_tiles):
    if len(refs) == 1:
        refs[0][...] = value
        return

    @pl.when(pl.program_id(0) < n_prompt_tiles)
    def _():
        refs[0][...] = value

    @pl.when(pl.program_id(0) >= n_prompt_tiles)
    def _():
        refs[1][...] = value


def _rms_kernel(*refs, n_in, n_prompt_tiles):
    x = _load_rows(refs[:n_in], n_prompt_tiles)
    g_ref, o_ref = refs[n_in:]
    o_ref[...] = _rms(x, g_ref[...]).astype(o_ref.dtype)


def rms_rows(x_parts, g, *, tm):
    d = x_parts[0].shape[1]
    m = sum(p.shape[0] for p in x_parts)
    npt = x_parts[0].shape[0] // tm
    row = pl.BlockSpec((tm, d), lambda i: (i, 0))
    x_specs = [row] if len(x_parts) == 1 else _split_specs(tm, d, npt)
    return pl.pallas_call(
        functools.partial(_rms_kernel, n_in=len(x_parts), n_prompt_tiles=npt),
        grid=(m // tm,),
        in_specs=x_specs + [pl.BlockSpec((1, d), lambda i: (0, 0))],
        out_specs=row,
        out_shape=jax.ShapeDtypeStruct((m, d), BF16),
        compiler_params=_params(("arbitrary",), 40),
        name="rms_rows",
    )(*x_parts, g)


def _post_kernel(*refs, n_in, n_out, has_next, n_prompt_tiles):
    m_ref = refs[0]
    x = _load_rows(refs[1:1 + n_in], n_prompt_tiles)
    rest = refs[1 + n_in:]
    xn = x + _rms(m_ref[...].astype(F32), rest[0][...])
    rest = rest[1:]
    if has_next:
        gnext_ref, rest = rest[0], rest[1:]
    _store_rows(rest[:n_out], xn, n_prompt_tiles)
    if has_next:
        ho_ref = rest[n_out]
        ho_ref[...] = _rms(xn, gnext_ref[...]).astype(ho_ref.dtype)


def post_rows(m_in, x_parts, g_post, g_next, *, tm, split_out=None):
    m, d = m_in.shape
    npt = (x_parts[0].shape[0] if split_out is None else split_out[0]) // tm
    if len(x_parts) == 1 and split_out is None:
        npt = m // tm
    row = pl.BlockSpec((tm, d), lambda i: (i, 0))
    vec = pl.BlockSpec((1, d), lambda i: (0, 0))
    x_specs = [row] if len(x_parts) == 1 else _split_specs(tm, d, npt)
    if split_out is None:
        xo_specs = [row]
        xo_shapes = [jax.ShapeDtypeStruct((m, d), F32)]
    else:
        xo_specs = _split_specs(tm, d, npt)
        xo_shapes = [jax.ShapeDtypeStruct((r, d), F32) for r in split_out]
    has_next = g_next is not None
    gains = [g_post, g_next] if has_next else [g_post]
    out = pl.pallas_call(
        functools.partial(_post_kernel, n_in=len(x_parts), n_out=len(xo_specs),
                          has_next=has_next, n_prompt_tiles=npt),
        grid=(m // tm,),
        in_specs=[row] + x_specs + [vec] * len(gains),
        out_specs=xo_specs + ([row] if has_next else []),
        out_shape=xo_shapes
        + ([jax.ShapeDtypeStruct((m, d), BF16)] if has_next else []),
        compiler_params=_params(("arbitrary",), 48),
        name="post_rows",
    )(m_in, *x_parts, *gains)
    x_new = out[:len(xo_specs)]
    return (x_new, out[-1]) if has_next else (x_new, None)


def _mm_kernel(a_ref, b_ref, o_ref, *, epilogue):
    acc = lax.dot_general(a_ref[...], b_ref[...], (((1,), (0,)), ((), ())),
                          preferred_element_type=F32)
    o_ref[...] = epilogue(acc).astype(o_ref.dtype)


def _mm2_kernel(a1_ref, a2_ref, b1_ref, b2_ref, o_ref):
    acc = jnp.dot(a1_ref[...], b1_ref[...], preferred_element_type=F32)
    acc = acc + jnp.dot(a2_ref[...], b2_ref[...], preferred_element_type=F32)
    o_ref[...] = acc.astype(o_ref.dtype)


def matmul_cols(a, w, layer, *, col_block0, n, tm, tn, out_dtype, epilogue, name):
    m, k = a.shape
    return pl.pallas_call(
        functools.partial(_mm_kernel, epilogue=epilogue),
        grid=(m // tm, n // tn),
        in_specs=[
            pl.BlockSpec((tm, k), lambda i, j: (i, 0)),
            pl.BlockSpec((None, k, tn), lambda i, j: (layer, 0, j + col_block0)),
        ],
        out_specs=pl.BlockSpec((tm, tn), lambda i, j: (i, j)),
        out_shape=jax.ShapeDtypeStruct((m, n), out_dtype),
        compiler_params=_params(("parallel", "arbitrary"), 56),
        name=name,
    )(a, w)


def matmul_two(a1, a2, w, layer, *, tm, tn, out_dtype, name):
    m, k1 = a1.shape
    k2 = a2.shape[1]
    assert k1 == k2
    n = w.shape[-1]
    return pl.pallas_call(
        _mm2_kernel,
        grid=(m // tm, n // tn),
        in_specs=[
            pl.BlockSpec((tm, k1), lambda i, j: (i, 0)),
            pl.BlockSpec((tm, k2), lambda i, j: (i, 0)),
            pl.BlockSpec((None, None, k1, tn), lambda i, j: (layer, 0, 0, j)),
            pl.BlockSpec((None, None, k2, tn), lambda i, j: (layer, 1, 0, j)),
        ],
        out_specs=pl.BlockSpec((tm, tn), lambda i, j: (i, j)),
        out_shape=jax.ShapeDtypeStruct((m, n), out_dtype),
        compiler_params=_params(("parallel", "arbitrary"), 56),
        name=name,
    )(a1, a2, w, w)


def _identity(x):
    return x


def _scale_q(x):
    return x * (HEAD_DIM ** -0.5)


def _gelu(x):
    return jax.nn.gelu(x, approximate=True)


def _softmax_with_sink(s, sink):
    m = jnp.maximum(jnp.max(s, axis=-1, keepdims=True), sink)
    e = jnp.exp(s - m)
    den = jnp.sum(e, axis=-1, keepdims=True) + jnp.exp(sink - m)
    return e / den


def _attn_prompt_kernel(sink_ref, q_ref, kvc_ref, kvp_ref, o_ref, *, blocks_per_seq):
    n = pl.program_id(0) % blocks_per_seq
    w = WINDOW
    row = lax.broadcasted_iota(jnp.int32, (w, 2 * w), 0)
    col = lax.broadcasted_iota(jnp.int32, (w, 2 * w), 1)
    prev_ok = jnp.logical_and(jnp.logical_and(col < w, col > row), n > 0)
    own_ok = jnp.logical_and(col >= w, col - w <= row)
    valid = jnp.logical_or(prev_ok, own_ok)
    for kvh in range(N_KV_HEADS):
        ks = slice(kvh * HEAD_DIM, (kvh + 1) * HEAD_DIM)
        vs = slice(KV_COLS + kvh * HEAD_DIM, KV_COLS + (kvh + 1) * HEAD_DIM)
        k = jnp.concatenate([kvp_ref[:, ks], kvc_ref[:, ks]], axis=0).astype(BF16)
        v = jnp.concatenate([kvp_ref[:, vs], kvc_ref[:, vs]], axis=0).astype(BF16)
        v_ext = jnp.concatenate(
            [v, jnp.zeros((2 * w, LANES - HEAD_DIM), BF16), jnp.ones((2 * w, LANES), BF16)],
            axis=1,
        )
        heads = [kvh * GQA_GROUP + g for g in range(GQA_GROUP)]
        qs = jnp.concatenate(
            [q_ref[:, h * HEAD_DIM:(h + 1) * HEAD_DIM] for h in heads], axis=0
        )
        s = lax.dot_general(
            qs, k, (((1,), (1,)), ((), ())), preferred_element_type=F32
        )
        es, sink_terms = [], []
        for g, h in enumerate(heads):
            sg = jnp.where(valid, s[g * w:(g + 1) * w], -jnp.inf)
            m = jnp.maximum(jnp.max(sg, axis=-1, keepdims=True), sink_ref[h])
            es.append(jnp.exp(sg - m).astype(BF16))
            sink_terms.append(jnp.exp(sink_ref[h] - m))
        o = jnp.dot(jnp.concatenate(es, axis=0), v_ext, preferred_element_type=F32)
        for g, h in enumerate(heads):
            og = o[g * w:(g + 1) * w]
            out = og[:, :LANES] / (og[:, LANES:] + sink_terms[g])
            o_ref[:, h * HEAD_DIM:(h + 1) * HEAD_DIM] = out[:, :HEAD_DIM].astype(
                o_ref.dtype
            )


def attn_prompt(q, kv, sinks, *, m_total, m_prompt, seq):
    w = WINDOW
    nblk = m_prompt // w
    return pl.pallas_call(
        functools.partial(_attn_prompt_kernel, blocks_per_seq=seq // w),
        grid=(nblk,),
        in_specs=[
            pl.BlockSpec(memory_space=pltpu.SMEM),
            pl.BlockSpec((w, Q_COLS), lambda r: (r, 0)),
            pl.BlockSpec((w, 2 * KV_COLS), lambda r: (r, 0)),
            pl.BlockSpec((w, 2 * KV_COLS), lambda r: (jnp.maximum(r - 1, 0), 0)),
        ],
        out_specs=pl.BlockSpec((w, Q_COLS), lambda r: (r, 0)),
        out_shape=jax.ShapeDtypeStruct((m_total, Q_COLS), BF16),
        compiler_params=_params(("parallel",), 40),
        name="attn_prompt",
    )(sinks, q, kv, kv)


def _attn_sample_kernel(sink_ref, q_ref, k_ref, v_ref, o_ref, *, t_new):
    nb, nkv, rows, d = q_ref.shape
    keys = k_ref.shape[2]
    q = q_ref[...].reshape(nb * nkv, rows, d)
    k = k_ref[...].reshape(nb * nkv, keys, d)
    v = v_ref[...].reshape(nb * nkv, keys, d)
    s = jnp.einsum("bqd,bkd->bqk", q, k, preferred_element_type=F32)
    t = lax.broadcasted_iota(jnp.int32, (rows, keys), 0) // GQA_GROUP
    col = lax.broadcasted_iota(jnp.int32, (rows, keys), 1)
    valid = jnp.logical_and(col > t, col <= WINDOW + t)
    s = jnp.where(valid[None], s, -jnp.inf)
    sink = jnp.broadcast_to(sink_ref[...][None], (nb, nkv, rows, 1)).reshape(
        nb * nkv, rows, 1
    )
    p = _softmax_with_sink(s, sink).astype(BF16)
    o = jnp.einsum("bqk,bkd->bqd", p, v, preferred_element_type=F32)
    o_ref[...] = o.reshape(nb, nkv, rows, d).astype(o_ref.dtype)


def attn_sample(q_s, k_new, v_new, k_buf, v_buf, sinks, *, nb):
    b, l = k_buf.shape[0], k_buf.shape[1]
    t_new = q_s.shape[0] // b
    rows = t_new * GQA_GROUP
    keys = 2 * WINDOW
    q4 = q_s.reshape(b, t_new, N_KV_HEADS, GQA_GROUP, HEAD_DIM)
    q4 = q4.transpose(0, 2, 1, 3, 4).reshape(b, N_KV_HEADS, rows, HEAD_DIM)

    def keys_layout(buf, new):
        cat = jnp.concatenate(
            [buf, new.reshape(b, t_new, N_KV_HEADS, HEAD_DIM)], axis=1
        )
        cat = jnp.pad(cat, ((0, 0), (0, keys - l - t_new), (0, 0), (0, 0)))
        return cat.transpose(0, 2, 1, 3).astype(BF16)

    k4 = keys_layout(k_buf, k_new)
    v4 = keys_layout(v_buf, v_new)
    sink_rows = jnp.broadcast_to(
        sinks.reshape(N_KV_HEADS, 1, GQA_GROUP), (N_KV_HEADS, t_new, GQA_GROUP)
    ).reshape(N_KV_HEADS, rows, 1)
    o4 = pl.pallas_call(
        functools.partial(_attn_sample_kernel, t_new=t_new),
        grid=(b // nb,),
        in_specs=[
            pl.BlockSpec((N_KV_HEADS, rows, 1), lambda i: (0, 0, 0)),
            pl.BlockSpec((nb, N_KV_HEADS, rows, HEAD_DIM), lambda i: (i, 0, 0, 0)),
            pl.BlockSpec((nb, N_KV_HEADS, keys, HEAD_DIM), lambda i: (i, 0, 0, 0)),
            pl.BlockSpec((nb, N_KV_HEADS, keys, HEAD_DIM), lambda i: (i, 0, 0, 0)),
        ],
        out_specs=pl.BlockSpec(
            (nb, N_KV_HEADS, rows, HEAD_DIM), lambda i: (i, 0, 0, 0)
        ),
        out_shape=jax.ShapeDtypeStruct((b, N_KV_HEADS, rows, HEAD_DIM), BF16),
        compiler_params=_params(("parallel",), 40),
        name="attn_sample",
    )(sink_rows, q4, k4, v4)
    o = o4.reshape(b, N_KV_HEADS, t_new, GQA_GROUP, HEAD_DIM).transpose(0, 2, 1, 3, 4)
    return o.reshape(b * t_new, Q_COLS)


def _place_rows_kernel(rows_ref, buf_ref, o_ref):
    del buf_ref
    o_ref[...] = rows_ref[...]


def place_rows(buf, rows, *, row_block):
    r, c = rows.shape
    return pl.pallas_call(
        _place_rows_kernel,
        grid=(1,),
        in_specs=[
            pl.BlockSpec((r, c), lambda i: (0, 0)),
            pl.BlockSpec(memory_space=pl.ANY),
        ],
        out_specs=pl.BlockSpec((r, c), lambda i: (row_block, 0)),
        out_shape=jax.ShapeDtypeStruct(buf.shape, buf.dtype),
        input_output_aliases={1: 0},
        compiler_params=_params(("arbitrary",), 16),
        name="place_rows",
    )(rows, buf)


def _gmlp_prompt_kernel(u_ref, v_ref, w_ref, bt_ref, gn_ref, o_ref):
    gv = _rms(v_ref[...].astype(F32), gn_ref[...])
    c = CHUNK
    row = lax.broadcasted_iota(jnp.int32, (c, c), 0)
    col = lax.broadcasted_iota(jnp.int32, (c, c), 1)
    causal = col <= row
    for h in range(GM_HEADS):
        cs = slice(h * GM_HEAD_DIM, (h + 1) * GM_HEAD_DIM)
        w = jnp.where(causal, w_ref[h], 0.0).astype(BF16)
        mixed = jnp.dot(w, gv[:, cs].astype(BF16), preferred_element_type=F32)
        mixed = mixed + bt_ref[:, h:h + 1]
        o_ref[:, cs] = (u_ref[:, cs].astype(F32) * mixed).astype(o_ref.dtype)


def gmlp_prompt(gm, w_s, b_t, g_norm, *, m_total, m_prompt):
    c = CHUNK
    return pl.pallas_call(
        _gmlp_prompt_kernel,
        grid=(m_prompt // c,),
        in_specs=[
            pl.BlockSpec((c, GM_WIDTH), lambda r: (r, 0)),
            pl.BlockSpec((c, GM_WIDTH), lambda r: (r, 1)),
            pl.BlockSpec((GM_HEADS, c, c), lambda r: (0, 0, 0)),
            pl.BlockSpec((c, GM_HEADS), lambda r: (0, 0)),
            pl.BlockSpec((1, GM_WIDTH), lambda r: (0, 0)),
        ],
        out_specs=pl.BlockSpec((c, GM_WIDTH), lambda r: (r, 0)),
        out_shape=jax.ShapeDtypeStruct((m_total, GM_WIDTH), BF16),
        compiler_params=_params(("parallel",), 40),
        name="gmlp_prompt",
    )(gm, gm, w_s, b_t, g_norm)


def _gmlp_sample_kernel(u_ref, v_ref, wc_ref, bc_ref, gn_ref, obuf_ref, o_ref, gv_ref, *,
                        t_new):
    del obuf_ref
    rows = u_ref.shape[0]
    nseq = rows // t_new
    gv = _rms(v_ref[...].astype(F32), gn_ref[...])
    gv_ref[...] = gv
    t = lax.broadcasted_iota(jnp.int32, (t_new, GM_HEAD_DIM), 0)
    for h in range(GM_HEADS):
        cs = slice(h * GM_HEAD_DIM, (h + 1) * GM_HEAD_DIM)
        x = gv[:, cs].reshape(nseq, t_new, GM_HEAD_DIM)
        acc = jnp.zeros((nseq, t_new, GM_HEAD_DIM), F32)
        for s in range(t_new):
            coef = jnp.where(t >= s, wc_ref[h, s], 0.0)
            acc = acc + coef[None] * x[:, s:s + 1, :]
        mixed = acc + bc_ref[h][None]
        u = u_ref[:, cs].astype(F32).reshape(nseq, t_new, GM_HEAD_DIM)
        o_ref[:, cs] = (u * mixed).reshape(rows, GM_HEAD_DIM).astype(o_ref.dtype)


def gmlp_sample(gm, out_buf, w_s, b_s, g_norm, *, m_prompt, m_sample, t_new):
    lanes = GM_HEAD_DIM
    w8 = w_s[:, :t_new, :t_new]
    wc = jnp.broadcast_to(
        w8.transpose(0, 2, 1)[..., None], (GM_HEADS, t_new, t_new, lanes)
    )
    bc = jnp.broadcast_to(b_s[:, :t_new, None], (GM_HEADS, t_new, lanes))
    blk = m_prompt // m_sample
    return pl.pallas_call(
        functools.partial(_gmlp_sample_kernel, t_new=t_new),
        grid=(1,),
        in_specs=[
            pl.BlockSpec((m_sample, GM_WIDTH), lambda i: (blk, 0)),
            pl.BlockSpec((m_sample, GM_WIDTH), lambda i: (blk, 1)),
            pl.BlockSpec((GM_HEADS, t_new, t_new, lanes), lambda i: (0, 0, 0, 0)),
            pl.BlockSpec((GM_HEADS, t_new, lanes), lambda i: (0, 0, 0)),
            pl.BlockSpec((1, GM_WIDTH), lambda i: (0, 0)),
            pl.BlockSpec(memory_space=pl.ANY),
        ],
        out_specs=[
            pl.BlockSpec((m_sample, GM_WIDTH), lambda i: (blk, 0)),
            pl.BlockSpec((m_sample, GM_WIDTH), lambda i: (0, 0)),
        ],
        out_shape=[
            jax.ShapeDtypeStruct(out_buf.shape, BF16),
            jax.ShapeDtypeStruct((m_sample, GM_WIDTH), F32),
        ],
        input_output_aliases={5: 0},
        compiler_params=_params(("arbitrary",), 40),
        name="gmlp_sample",
    )(gm, gm, wc, bc, g_norm, out_buf)


def _conv_gelu_gate(g, g1, g2, up, cw_ref, cb_ref):
    c = cb_ref[...] + cw_ref[0:1, :] * g2
    c = c + cw_ref[1:2, :] * g1
    c = c + cw_ref[2:3, :] * g
    return _gelu(c) * up


def _ffn_a_kernel(h_ref, hs_ref, wg_ref, wu_ref, cw_ref, cb_ref, st_ref, a_ref, tail_ref,
                  as_ref, gs_ref, *, chunk, t_new):
    i = pl.program_id(0)
    tm = h_ref.shape[0]
    tf = wg_ref.shape[1]
    wg = wg_ref[...].astype(BF16)
    wu = wu_ref[...].astype(BF16)

    prev8 = jnp.zeros((8, tf), F32)
    for r0 in range(0, tm, chunk):
        h = h_ref[r0:r0 + chunk, :]
        g = jnp.dot(h, wg, preferred_element_type=F32)
        up = jnp.dot(h, wu, preferred_element_type=F32)
        ext = jnp.concatenate([prev8, g], axis=0)
        g1 = pltpu.roll(ext, 1, 0)[8:]
        g2 = pltpu.roll(ext, 2, 0)[8:]
        a_ref[r0:r0 + chunk, :] = _conv_gelu_gate(g, g1, g2, up, cw_ref, cb_ref).astype(
            a_ref.dtype
        )
        prev8 = g[chunk - 8:, :]
    tail_ref[...] = prev8

    @pl.when(i == pl.num_programs(0) - 1)
    def _():
        hs = hs_ref[...]
        rows = hs.shape[0]
        g = jnp.dot(hs, wg, preferred_element_type=F32)
        up = jnp.dot(hs, wu, preferred_element_type=F32)
        gs_ref[...] = g
        t = lax.broadcasted_iota(jnp.int32, g.shape, 0) % t_new
        st2 = st_ref[...]
        st1 = pltpu.roll(st2, rows - 1, 0)
        g1 = jnp.where(t >= 1, pltpu.roll(g, 1, 0), st1)
        g2 = jnp.where(t >= 2, pltpu.roll(g, 2, 0), st2)
        as_ref[...] = _conv_gelu_gate(g, g1, g2, up, cw_ref, cb_ref).astype(as_ref.dtype)


def ffn_a(h, w_gate, w_up, conv_w, conv_b, conv_state, layer, *, m_prompt, m_sample, seq,
          t_new, tf, chunk):
    m_total, d = h.shape
    f = w_gate.shape[-1]
    nf = f // tf
    ni = m_prompt // seq
    assert t_new % 8 == 0 and t_new >= CONV_W - 1
    st = jnp.pad(conv_state, ((0, 0), (0, t_new - (CONV_W - 1)), (0, 0)))
    st = st.reshape(m_sample, f)
    sample_col = lambda i, j: (0, jnp.where(i == ni - 1, j, 0))
    return pl.pallas_call(
        functools.partial(_ffn_a_kernel, chunk=chunk, t_new=t_new),
        grid=(ni, nf),
        in_specs=[
            pl.BlockSpec((seq, d), lambda i, j: (i, 0), pipeline_mode=pl.Buffered(1)),
            pl.BlockSpec((m_sample, d), lambda i, j: (m_prompt // m_sample, 0)),
            pl.BlockSpec((None, d, tf), lambda i, j: (layer, 0, j)),
            pl.BlockSpec((None, d, tf), lambda i, j: (layer, 0, j)),
            pl.BlockSpec((None, CONV_W, tf), lambda i, j: (layer, 0, j)),
            pl.BlockSpec((None, 1, tf), lambda i, j: (layer, 0, j)),
            pl.BlockSpec((m_sample, tf), lambda i, j: (0, j)),
        ],
        out_specs=[
            pl.BlockSpec((seq, tf), lambda i, j: (i, j)),
            pl.BlockSpec((None, 8, tf), lambda i, j: (i, 0, j)),
            pl.BlockSpec((m_sample, tf), sample_col),
            pl.BlockSpec((m_sample, tf), sample_col),
        ],
        out_shape=[
            jax.ShapeDtypeStruct((m_total, f), BF16),
            jax.ShapeDtypeStruct((ni, 8, f), F32),
            jax.ShapeDtypeStruct((m_sample, f), BF16),
            jax.ShapeDtypeStruct((m_sample, f), F32),
        ],
        compiler_params=_params(("arbitrary", "arbitrary"), 56),
        name="ffn_a",
    )(h, h, w_gate, w_up, conv_w, conv_b, st)


def kernel(x_prompt, x_sample, state_swa_k, state_swa_v, state_conv, w_in, w_out,
           attn_sinks, gm_spatial, gm_bias, gm_norm, norm_pre_mix, norm_post_mix,
           norm_pre_ffn, norm_post_ffn, w_ffn_gate, w_ffn_up, w_ffn_down, conv_w,
           conv_b):
    batch, seq, d = x_prompt.shape
    dec_batch, dec_seq, _ = x_sample.shape
    depth = w_in.shape[0]
    f = w_ffn_gate.shape[-1]
    mp = batch * seq
    ms = dec_batch * dec_seq
    m = mp + ms
    keep = min(WINDOW, seq)

    tm_mm = m // 8
    tm_down = m // 11
    tf = 256
    tm_rows = ms
    assert mp % tm_rows == 0

    w_in_b = w_in.astype(BF16)
    w_out_b = w_out.astype(BF16).reshape(depth, 2, Q_COLS, d)
    w_down_b = w_ffn_down.astype(BF16)
    conv_b3 = conv_b.reshape(depth, 1, f)

    x = (x_prompt.reshape(mp, d), x_sample.reshape(ms, d))
    h = rms_rows(x, norm_pre_mix[0].reshape(1, d), tm=tm_rows)

    p_k, p_v, p_c, s_k, s_v, s_c, s_g = [], [], [], [], [], [], []
    for l in range(depth):
        tn = 512
        q = matmul_cols(h, w_in_b, l, col_block0=0, n=Q_COLS, tm=tm_mm, tn=tn,
                        out_dtype=BF16, epilogue=_scale_q, name="mm_q")
        kv = matmul_cols(h, w_in_b, l, col_block0=Q_COLS // tn, n=2 * KV_COLS,
                         tm=tm_mm, tn=tn, out_dtype=F32, epilogue=_identity,
                         name="mm_kv")
        gm = matmul_cols(h, w_in_b, l, col_block0=(Q_COLS + 2 * KV_COLS) // tn,
                         n=2 * GM_WIDTH, tm=tm_mm, tn=tn, out_dtype=BF16,
                         epilogue=_gelu, name="mm_gm")

        k_all, v_all = kv[:, :KV_COLS], kv[:, KV_COLS:]
        p_k.append(k_all[:mp].reshape(batch, seq, N_KV_HEADS, HEAD_DIM)[:, seq - keep:])
        p_v.append(v_all[:mp].reshape(batch, seq, N_KV_HEADS, HEAD_DIM)[:, seq - keep:])
        s_k.append(k_all[mp:].reshape(dec_batch, dec_seq, N_KV_HEADS, HEAD_DIM))
        s_v.append(v_all[mp:].reshape(dec_batch, dec_seq, N_KV_HEADS, HEAD_DIM))

        att = attn_prompt(q, kv, attn_sinks[l], m_total=m, m_prompt=mp, seq=seq)
        att_s = attn_sample(q[mp:], k_all[mp:], v_all[mp:], state_swa_k[l],
                            state_swa_v[l], attn_sinks[l], nb=8)
        att = place_rows(att, att_s, row_block=mp // ms)

        gmo = gmlp_prompt(gm, gm_spatial[l], gm_bias[l].T, gm_norm[l].reshape(1, -1),
                          m_total=m, m_prompt=mp)
        gmo, gv_s = gmlp_sample(gm, gmo, gm_spatial[l], gm_bias[l],
                                gm_norm[l].reshape(1, -1), m_prompt=mp, m_sample=ms,
                                t_new=dec_seq)
        s_g.append(gv_s.reshape(dec_batch, dec_seq, GM_HEADS, GM_HEAD_DIM))

        mix = matmul_two(att, gmo, w_out_b, l, tm=tm_mm, tn=512, out_dtype=BF16,
                         name="mm_out")
        x, h = post_rows(mix, x, norm_post_mix[l].reshape(1, d),
                         norm_pre_ffn[l].reshape(1, d), tm=tm_rows)

        a, tail, a_s, g_s = ffn_a(h, w_ffn_gate, w_ffn_up, conv_w, conv_b3, state_conv[l],
                                  l, m_prompt=mp, m_sample=ms, seq=seq, t_new=dec_seq,
                                  tf=tf, chunk=512)
        a = place_rows(a, a_s, row_block=mp // ms)
        p_c.append(tail[:, 8 - (CONV_W - 1):])
        s_c.append(g_s.reshape(dec_batch, dec_seq, f)[:, dec_seq - (CONV_W - 1):])

        ffn = matmul_cols(a, w_down_b, l, col_block0=0, n=d, tm=tm_down, tn=256,
                          out_dtype=BF16, epilogue=_identity, name="mm_down")
        last = l + 1 == depth
        x, h = post_rows(ffn, x, norm_post_ffn[l].reshape(1, d),
                         None if last else norm_pre_mix[l + 1].reshape(1, d),
                         tm=tm_rows, split_out=(mp, ms) if last else None)

    x_p, x_s = x
    return (x_p.reshape(batch, seq, d), x_s.reshape(dec_batch, dec_seq, d),
            jnp.stack(p_k), jnp.stack(p_v), jnp.stack(p_c),
            jnp.stack(s_k), jnp.stack(s_v), jnp.stack(s_c), jnp.stack(s_g))
```

```python
import functools

import jax
import jax.numpy as jnp
from jax import lax
from jax.experimental import pallas as pl
from jax.experimental.pallas import tpu as pltpu

F32 = jnp.float32
BF16 = jnp.bfloat16

D_MODEL = 4096
HEAD_DIM = 64
N_HEADS = 32
N_KV_HEADS = 4
GQA_GROUP = 8
WINDOW = 128
CHUNK = 128
GM_HEAD_DIM = 128
GM_HEADS = 16
GM_WIDTH = 2048
Q_COLS = 2048
KV_COLS = 256
CONV_W = 3
EPS = 1e-6
LANES = 128

MIB = 1024 * 1024


def _params(semantics, vmem_mib):
    return pltpu.CompilerParams(
        dimension_semantics=semantics, vmem_limit_bytes=vmem_mib * MIB
    )


def _rms(x, g):
    ms = jnp.mean(x * x, axis=-1, keepdims=True)
    return x * lax.rsqrt(ms + EPS) * g


def _split_specs(tm, d, n_prompt_tiles):
    return [
        pl.BlockSpec((tm, d), lambda i: (jnp.minimum(i, n_prompt_tiles - 1), 0)),
        pl.BlockSpec((tm, d), lambda i: (0, 0)),
    ]


def _load_rows(refs, n_prompt_tiles):
    if len(refs) == 1:
        return refs[0][...]
    is_prompt = pl.program_id(0) < n_prompt_tiles
    return jnp.where(is_prompt, refs[0][...], refs[1][...])


def _store_rows(refs, value, n_prompt_tiles):
    if len(refs) == 1:
        refs[0][...] = value
        return

    @pl.when(pl.program_id(0) < n_prompt_tiles)
    def _():
        refs[0][...] = value

    @pl.when(pl.program_id(0) >= n_prompt_tiles)
    def _():
        refs[1][...] = value


def _rms_kernel(*refs, n_in, n_prompt_tiles):
    x = _load_rows(refs[:n_in], n_prompt_tiles)
    g_ref, o_ref = refs[n_in:]
    o_ref[...] = _rms(x, g_ref[...]).astype(o_ref.dtype)


def rms_rows(x_parts, g, *, tm):
    d = x_parts[0].shape[1]
    m = sum(p.shape[0] for p in x_parts)
    npt = x_parts[0].shape[0] // tm
    row = pl.BlockSpec((tm, d), lambda i: (i, 0))
    x_specs = [row] if len(x_parts) == 1 else _split_specs(tm, d, npt)
    return pl.pallas_call(
        functools.partial(_rms_kernel, n_in=len(x_parts), n_prompt_tiles=npt),
        grid=(m // tm,),
        in_specs=x_specs + [pl.BlockSpec((1, d), lambda i: (0, 0))],
        out_specs=row,
        out_shape=jax.ShapeDtypeStruct((m, d), BF16),
        compiler_params=_params(("arbitrary",), 40),
        name="rms_rows",
    )(*x_parts, g)


def _post_kernel(*refs, n_in, n_out, has_next, n_prompt_tiles):
    m_ref = refs[0]
    x = _load_rows(refs[1:1 + n_in], n_prompt_tiles)
    rest = refs[1 + n_in:]
    xn = x + _rms(m_ref[...].astype(F32), rest[0][...])
    rest = rest[1:]
    if has_next:
        gnext_ref, rest = rest[0], rest[1:]
    _store_rows(rest[:n_out], xn, n_prompt_tiles)
    if has_next:
        ho_ref = rest[n_out]
        ho_ref[...] = _rms(xn, gnext_ref[...]).astype(ho_ref.dtype)


def post_rows(m_in, x_parts, g_post, g_next, *, tm, split_out=None):
    m, d = m_in.shape
    npt = (x_parts[0].shape[0] if split_out is None else split_out[0]) // tm
    if len(x_parts) == 1 and split_out is None:
        npt = m // tm
    row = pl.BlockSpec((tm, d), lambda i: (i, 0))
    vec = pl.BlockSpec((1, d), lambda i: (0, 0))
    x_specs = [row] if len(x_parts) == 1 else _split_specs(tm, d, npt)
    if split_out is None:
        xo_specs = [row]
        xo_shapes = [jax.ShapeDtypeStruct((m, d), F32)]
    else:
        xo_specs = _split_specs(tm, d, npt)
        xo_shapes = [jax.ShapeDtypeStruct((r, d), F32) for r in split_out]
    has_next = g_next is not None
    gains = [g_post, g_next] if has_next else [g_post]
    out = pl.pallas_call(
        functools.partial(_post_kernel, n_in=len(x_parts), n_out=len(xo_specs),
                          has_next=has_next, n_prompt_tiles=npt),
        grid=(m // tm,),
        in_specs=[row] + x_specs + [vec] * len(gains),
        out_specs=xo_specs + ([row] if has_next else []),
        out_shape=xo_shapes
        + ([jax.ShapeDtypeStruct((m, d), BF16)] if has_next else []),
        compiler_params=_params(("arbitrary",), 48),
        name="post_rows",
    )(m_in, *x_parts, *gains)
    x_new = out[:len(xo_specs)]
    return (x_new, out[-1]) if has_next else (x_new, None)


def _grid_step():
    return pl.program_id(0) * pl.num_programs(1) + pl.program_id(1)


def _side_cast(src_ref, dst_ref, n_blocks):
    @pl.when(_grid_step() < n_blocks)
    def _():
        dst_ref[...] = src_ref[...].astype(dst_ref.dtype)


def _side_cast_specs(src, layer, block_rows, n_steps, n_cols_grid):
    _, rows, cols = src.shape
    n_blocks = rows // block_rows
    assert rows % block_rows == 0 and n_blocks <= n_steps

    def block(i, j):
        return jnp.minimum(i * n_cols_grid + j, n_blocks - 1)

    in_spec = pl.BlockSpec((None, block_rows, cols), lambda i, j: (layer, block(i, j), 0))
    out_spec = pl.BlockSpec((block_rows, cols), lambda i, j: (block(i, j), 0))
    return in_spec, out_spec, jax.ShapeDtypeStruct((rows, cols), BF16), n_blocks


def _mm_kernel(a_ref, b_ref, *rest, epilogue, cast_blocks):
    o_ref = rest[-2] if cast_blocks else rest[-1]
    acc = jnp.dot(a_ref[...], b_ref[...], preferred_element_type=F32)
    o_ref[...] = epilogue(acc).astype(o_ref.dtype)
    if cast_blocks:
        _side_cast(rest[0], rest[-1], cast_blocks)


def _mm2_kernel(a1_ref, a2_ref, b1_ref, b2_ref, *rest, cast_blocks):
    o_ref = rest[-2] if cast_blocks else rest[-1]
    acc = jnp.dot(a1_ref[...], b1_ref[...], preferred_element_type=F32)
    acc = acc + jnp.dot(a2_ref[...], b2_ref[...], preferred_element_type=F32)
    o_ref[...] = acc.astype(o_ref.dtype)
    if cast_blocks:
        _side_cast(rest[0], rest[-1], cast_blocks)


def _with_side_cast(cast, grid, in_specs, out_specs, out_shapes, operands):
    if cast is None:
        return 0
    src, layer, block_rows = cast
    i_spec, o_spec, o_shape, n_blocks = _side_cast_specs(
        src, layer, block_rows, grid[0] * grid[1], grid[1]
    )
    in_specs.append(i_spec)
    out_specs.append(o_spec)
    out_shapes.append(o_shape)
    operands.append(src)
    return n_blocks


def matmul_cols(a, w, layer, *, col_block0, n, tm, tn, out_dtype, epilogue, name,
                cast=None):
    m, k = a.shape
    grid = (m // tm, n // tn)
    in_specs = [
        pl.BlockSpec((tm, k), lambda i, j: (i, 0)),
        pl.BlockSpec((None, k, tn), lambda i, j: (layer, 0, j + col_block0)),
    ]
    out_specs = [pl.BlockSpec((tm, tn), lambda i, j: (i, j))]
    out_shapes = [jax.ShapeDtypeStruct((m, n), out_dtype)]
    operands = [a, w]
    cast_blocks = _with_side_cast(cast, grid, in_specs, out_specs, out_shapes, operands)
    out = pl.pallas_call(
        functools.partial(_mm_kernel, epilogue=epilogue, cast_blocks=cast_blocks),
        grid=grid,
        in_specs=in_specs,
        out_specs=out_specs,
        out_shape=out_shapes,
        compiler_params=_params(("arbitrary", "arbitrary"), 56),
        name=name,
    )(*operands)
    return out[0], (out[1] if cast_blocks else None)


def matmul_two(a1, a2, w, layer, *, tm, tn, out_dtype, name, cast=None):
    m, k1 = a1.shape
    k2 = a2.shape[1]
    assert k1 == k2
    n = w.shape[-1]
    grid = (m // tm, n // tn)
    in_specs = [
        pl.BlockSpec((tm, k1), lambda i, j: (i, 0)),
        pl.BlockSpec((tm, k2), lambda i, j: (i, 0)),
        pl.BlockSpec((None, None, k1, tn), lambda i, j: (layer, 0, 0, j)),
        pl.BlockSpec((None, None, k2, tn), lambda i, j: (layer, 1, 0, j)),
    ]
    out_specs = [pl.BlockSpec((tm, tn), lambda i, j: (i, j))]
    out_shapes = [jax.ShapeDtypeStruct((m, n), out_dtype)]
    operands = [a1, a2, w, w]
    cast_blocks = _with_side_cast(cast, grid, in_specs, out_specs, out_shapes, operands)
    out = pl.pallas_call(
        functools.partial(_mm2_kernel, cast_blocks=cast_blocks),
        grid=grid,
        in_specs=in_specs,
        out_specs=out_specs,
        out_shape=out_shapes,
        compiler_params=_params(("arbitrary", "arbitrary"), 56),
        name=name,
    )(*operands)
    return out[0], (out[1] if cast_blocks else None)


def _identity(x):
    return x


def _scale_q(x):
    return x * (HEAD_DIM ** -0.5)


def _gelu(x):
    return jax.nn.gelu(x, approximate=True)


def _softmax_with_sink(s, sink):
    m = jnp.maximum(jnp.max(s, axis=-1, keepdims=True), sink)
    e = jnp.exp(s - m)
    den = jnp.sum(e, axis=-1, keepdims=True) + jnp.exp(sink - m)
    return e / den


def _attn_prompt_kernel(sink_ref, q_ref, kvc_ref, kvp_ref, o_ref, *, blocks_per_seq):
    n = pl.program_id(0) % blocks_per_seq
    w = WINDOW
    row = lax.broadcasted_iota(jnp.int32, (w, 2 * w), 0)
    col = lax.broadcasted_iota(jnp.int32, (w, 2 * w), 1)
    prev_ok = jnp.logical_and(jnp.logical_and(col < w, col > row), n > 0)
    own_ok = jnp.logical_and(col >= w, col - w <= row)
    valid = jnp.logical_or(prev_ok, own_ok)
    for kvh in range(N_KV_HEADS):
        ks = slice(kvh * HEAD_DIM, (kvh + 1) * HEAD_DIM)
        vs = slice(KV_COLS + kvh * HEAD_DIM, KV_COLS + (kvh + 1) * HEAD_DIM)
        k = jnp.concatenate([kvp_ref[:, ks], kvc_ref[:, ks]], axis=0).astype(BF16)
        v = jnp.concatenate([kvp_ref[:, vs], kvc_ref[:, vs]], axis=0).astype(BF16)
        v_ext = jnp.concatenate(
            [v, jnp.zeros((2 * w, LANES - HEAD_DIM), BF16), jnp.ones((2 * w, LANES), BF16)],
            axis=1,
        )
        heads = [kvh * GQA_GROUP + g for g in range(GQA_GROUP)]
        qs = jnp.concatenate(
            [q_ref[:, h * HEAD_DIM:(h + 1) * HEAD_DIM] for h in heads], axis=0
        )
        s = lax.dot_general(
            qs, k, (((1,), (1,)), ((), ())), preferred_element_type=F32
        )
        es, sink_terms = [], []
        for g, h in enumerate(heads):
            sg = jnp.where(valid, s[g * w:(g + 1) * w], -jnp.inf)
            m = jnp.maximum(jnp.max(sg, axis=-1, keepdims=True), sink_ref[h])
            es.append(jnp.exp(sg - m).astype(BF16))
            sink_terms.append(jnp.exp(sink_ref[h] - m))
        o = jnp.dot(jnp.concatenate(es, axis=0), v_ext, preferred_element_type=F32)
        for g, h in enumerate(heads):
            og = o[g * w:(g + 1) * w]
            out = og[:, :LANES] / (og[:, LANES:] + sink_terms[g])
            o_ref[:, h * HEAD_DIM:(h + 1) * HEAD_DIM] = out[:, :HEAD_DIM].astype(
                o_ref.dtype
            )


def attn_prompt(q, kv, sinks, *, m_total, m_prompt, seq):
    w = WINDOW
    nblk = m_prompt // w
    return pl.pallas_call(
        functools.partial(_attn_prompt_kernel, blocks_per_seq=seq // w),
        grid=(nblk,),
        in_specs=[
            pl.BlockSpec(memory_space=pltpu.SMEM),
            pl.BlockSpec((w, Q_COLS), lambda r: (r, 0)),
            pl.BlockSpec((w, 2 * KV_COLS), lambda r: (r, 0)),
            pl.BlockSpec((w, 2 * KV_COLS), lambda r: (jnp.maximum(r - 1, 0), 0)),
        ],
        out_specs=pl.BlockSpec((w, Q_COLS), lambda r: (r, 0)),
        out_shape=jax.ShapeDtypeStruct((m_total, Q_COLS), BF16),
        compiler_params=_params(("parallel",), 40),
        name="attn_prompt",
    )(sinks, q, kv, kv)


def _attn_sample_kernel(sink_ref, q_ref, k_ref, v_ref, o_ref, *, t_new):
    nb, nkv, rows, d = q_ref.shape
    keys = k_ref.shape[2]
    q = q_ref[...].reshape(nb * nkv, rows, d)
    k = k_ref[...].reshape(nb * nkv, keys, d)
    v = v_ref[...].reshape(nb * nkv, keys, d)
    s = jnp.einsum("bqd,bkd->bqk", q, k, preferred_element_type=F32)
    t = lax.broadcasted_iota(jnp.int32, (rows, keys), 0) // GQA_GROUP
    col = lax.broadcasted_iota(jnp.int32, (rows, keys), 1)
    valid = jnp.logical_and(col > t, col <= WINDOW + t)
    s = jnp.where(valid[None], s, -jnp.inf)
    sink = jnp.broadcast_to(sink_ref[...][None], (nb, nkv, rows, 1)).reshape(
        nb * nkv, rows, 1
    )
    p = _softmax_with_sink(s, sink).astype(BF16)
    o = jnp.einsum("bqk,bkd->bqd", p, v, preferred_element_type=F32)
    o_ref[...] = o.reshape(nb, nkv, rows, d).astype(o_ref.dtype)


def attn_sample(q_s, k_new, v_new, k_buf, v_buf, sinks, *, nb):
    b, l = k_buf.shape[0], k_buf.shape[1]
    t_new = q_s.shape[0] // b
    rows = t_new * GQA_GROUP
    keys = 2 * WINDOW
    q4 = q_s.reshape(b, t_new, N_KV_HEADS, GQA_GROUP, HEAD_DIM)
    q4 = q4.transpose(0, 2, 1, 3, 4).reshape(b, N_KV_HEADS, rows, HEAD_DIM)

    def keys_layout(buf, new):
        cat = jnp.concatenate(
            [buf, new.reshape(b, t_new, N_KV_HEADS, HEAD_DIM)], axis=1
        )
        cat = jnp.pad(cat, ((0, 0), (0, keys - l - t_new), (0, 0), (0, 0)))
        return cat.transpose(0, 2, 1, 3).astype(BF16)

    k4 = keys_layout(k_buf, k_new)
    v4 = keys_layout(v_buf, v_new)
    sink_rows = jnp.broadcast_to(
        sinks.reshape(N_KV_HEADS, 1, GQA_GROUP), (N_KV_HEADS, t_new, GQA_GROUP)
    ).reshape(N_KV_HEADS, rows, 1)
    o4 = pl.pallas_call(
        functools.partial(_attn_sample_kernel, t_new=t_new),
        grid=(b // nb,),
        in_specs=[
            pl.BlockSpec((N_KV_HEADS, rows, 1), lambda i: (0, 0, 0)),
            pl.BlockSpec((nb, N_KV_HEADS, rows, HEAD_DIM), lambda i: (i, 0, 0, 0)),
            pl.BlockSpec((nb, N_KV_HEADS, keys, HEAD_DIM), lambda i: (i, 0, 0, 0)),
            pl.BlockSpec((nb, N_KV_HEADS, keys, HEAD_DIM), lambda i: (i, 0, 0, 0)),
        ],
        out_specs=pl.BlockSpec(
            (nb, N_KV_HEADS, rows, HEAD_DIM), lambda i: (i, 0, 0, 0)
        ),
        out_shape=jax.ShapeDtypeStruct((b, N_KV_HEADS, rows, HEAD_DIM), BF16),
        compiler_params=_params(("parallel",), 40),
        name="attn_sample",
    )(sink_rows, q4, k4, v4)
    o = o4.reshape(b, N_KV_HEADS, t_new, GQA_GROUP, HEAD_DIM).transpose(0, 2, 1, 3, 4)
    return o.reshape(b * t_new, Q_COLS)


def _place_rows_kernel(rows_ref, buf_ref, o_ref):
    del buf_ref
    o_ref[...] = rows_ref[...]


def place_rows(buf, rows, *, row_block):
    r, c = rows.shape
    return pl.pallas_call(
        _place_rows_kernel,
        grid=(1,),
        in_specs=[
            pl.BlockSpec((r, c), lambda i: (0, 0)),
            pl.BlockSpec(memory_space=pl.ANY),
        ],
        out_specs=pl.BlockSpec((r, c), lambda i: (row_block, 0)),
        out_shape=jax.ShapeDtypeStruct(buf.shape, buf.dtype),
        input_output_aliases={1: 0},
        compiler_params=_params(("arbitrary",), 16),
        name="place_rows",
    )(rows, buf)


def _gmlp_prompt_kernel(u_ref, v_ref, w_ref, bt_ref, gn_ref, o_ref):
    gv = _rms(v_ref[...].astype(F32), gn_ref[...])
    c = CHUNK
    row = lax.broadcasted_iota(jnp.int32, (c, c), 0)
    col = lax.broadcasted_iota(jnp.int32, (c, c), 1)
    causal = col <= row
    for h in range(GM_HEADS):
        cs = slice(h * GM_HEAD_DIM, (h + 1) * GM_HEAD_DIM)
        w = jnp.where(causal, w_ref[h], 0.0).astype(BF16)
        mixed = jnp.dot(w, gv[:, cs].astype(BF16), preferred_element_type=F32)
        mixed = mixed + bt_ref[:, h:h + 1]
        o_ref[:, cs] = (u_ref[:, cs].astype(F32) * mixed).astype(o_ref.dtype)


def gmlp_prompt(gm, w_s, b_t, g_norm, *, m_total, m_prompt):
    c = CHUNK
    return pl.pallas_call(
        _gmlp_prompt_kernel,
        grid=(m_prompt // c,),
        in_specs=[
            pl.BlockSpec((c, GM_WIDTH), lambda r: (r, 0)),
            pl.BlockSpec((c, GM_WIDTH), lambda r: (r, 1)),
            pl.BlockSpec((GM_HEADS, c, c), lambda r: (0, 0, 0)),
            pl.BlockSpec((c, GM_HEADS), lambda r: (0, 0)),
            pl.BlockSpec((1, GM_WIDTH), lambda r: (0, 0)),
        ],
        out_specs=pl.BlockSpec((c, GM_WIDTH), lambda r: (r, 0)),
        out_shape=jax.ShapeDtypeStruct((m_total, GM_WIDTH), BF16),
        compiler_params=_params(("parallel",), 40),
        name="gmlp_prompt",
    )(gm, gm, w_s, b_t, g_norm)


def _gmlp_sample_kernel(u_ref, v_ref, wc_ref, bc_ref, gn_ref, obuf_ref, o_ref, gv_ref, *,
                        t_new):
    del obuf_ref
    rows = u_ref.shape[0]
    nseq = rows // t_new
    gv = _rms(v_ref[...].astype(F32), gn_ref[...])
    gv_ref[...] = gv
    t = lax.broadcasted_iota(jnp.int32, (t_new, GM_HEAD_DIM), 0)
    for h in range(GM_HEADS):
        cs = slice(h * GM_HEAD_DIM, (h + 1) * GM_HEAD_DIM)
        x = gv[:, cs].reshape(nseq, t_new, GM_HEAD_DIM)
        acc = jnp.zeros((nseq, t_new, GM_HEAD_DIM), F32)
        for s in range(t_new):
            coef = jnp.where(t >= s, wc_ref[h, s], 0.0)
            acc = acc + coef[None] * x[:, s:s + 1, :]
        mixed = acc + bc_ref[h][None]
        u = u_ref[:, cs].astype(F32).reshape(nseq, t_new, GM_HEAD_DIM)
        o_ref[:, cs] = (u * mixed).reshape(rows, GM_HEAD_DIM).astype(o_ref.dtype)


def gmlp_sample(gm, out_buf, w_s, b_s, g_norm, *, m_prompt, m_sample, t_new):
    lanes = GM_HEAD_DIM
    w8 = w_s[:, :t_new, :t_new]
    wc = jnp.broadcast_to(
        w8.transpose(0, 2, 1)[..., None], (GM_HEADS, t_new, t_new, lanes)
    )
    bc = jnp.broadcast_to(b_s[:, :t_new, None], (GM_HEADS, t_new, lanes))
    blk = m_prompt // m_sample
    return pl.pallas_call(
        functools.partial(_gmlp_sample_kernel, t_new=t_new),
        grid=(1,),
        in_specs=[
            pl.BlockSpec((m_sample, GM_WIDTH), lambda i: (blk, 0)),
            pl.BlockSpec((m_sample, GM_WIDTH), lambda i: (blk, 1)),
            pl.BlockSpec((GM_HEADS, t_new, t_new, lanes), lambda i: (0, 0, 0, 0)),
            pl.BlockSpec((GM_HEADS, t_new, lanes), lambda i: (0, 0, 0)),
            pl.BlockSpec((1, GM_WIDTH), lambda i: (0, 0)),
            pl.BlockSpec(memory_space=pl.ANY),
        ],
        out_specs=[
            pl.BlockSpec((m_sample, GM_WIDTH), lambda i: (blk, 0)),
            pl.BlockSpec((m_sample, GM_WIDTH), lambda i: (0, 0)),
        ],
        out_shape=[
            jax.ShapeDtypeStruct(out_buf.shape, BF16),
            jax.ShapeDtypeStruct((m_sample, GM_WIDTH), F32),
        ],
        input_output_aliases={5: 0},
        compiler_params=_params(("arbitrary",), 40),
        name="gmlp_sample",
    )(gm, gm, wc, bc, g_norm, out_buf)


def _conv_gelu_gate(g, g1, g2, up, cw_ref, cb_ref):
    c = cb_ref[...] + cw_ref[0:1, :] * g2
    c = c + cw_ref[1:2, :] * g1
    c = c + cw_ref[2:3, :] * g
    return _gelu(c) * up


def _ffn_a_kernel(h_ref, hs_ref, wg_ref, wu_ref, cw_ref, cb_ref, st_ref, wd_ref, a_ref,
                  tail_ref, as_ref, gs_ref, wdb_ref, *, chunk, t_new, cast_blocks):
    i = pl.program_id(0)
    _side_cast(wd_ref, wdb_ref, cast_blocks)
    tm = h_ref.shape[0]
    tf = wg_ref.shape[1]
    wg = wg_ref[...].astype(BF16)
    wu = wu_ref[...].astype(BF16)

    prev8 = jnp.zeros((8, tf), F32)
    for r0 in range(0, tm, chunk):
        h = h_ref[r0:r0 + chunk, :]
        g = jnp.dot(h, wg, preferred_element_type=F32)
        up = jnp.dot(h, wu, preferred_element_type=F32)
        ext = jnp.concatenate([prev8, g], axis=0)
        g1 = pltpu.roll(ext, 1, 0)[8:]
        g2 = pltpu.roll(ext, 2, 0)[8:]
        a_ref[r0:r0 + chunk, :] = _conv_gelu_gate(g, g1, g2, up, cw_ref, cb_ref).astype(
            a_ref.dtype
        )
        prev8 = g[chunk - 8:, :]
    tail_ref[...] = prev8

    @pl.when(i == pl.num_programs(0) - 1)
    def _():
        hs = hs_ref[...]
        rows = hs.shape[0]
        g = jnp.dot(hs, wg, preferred_element_type=F32)
        up = jnp.dot(hs, wu, preferred_element_type=F32)
        gs_ref[...] = g
        t = lax.broadcasted_iota(jnp.int32, g.shape, 0) % t_new
        st2 = st_ref[...]
        st1 = pltpu.roll(st2, rows - 1, 0)
        g1 = jnp.where(t >= 1, pltpu.roll(g, 1, 0), st1)
        g2 = jnp.where(t >= 2, pltpu.roll(g, 2, 0), st2)
        as_ref[...] = _conv_gelu_gate(g, g1, g2, up, cw_ref, cb_ref).astype(as_ref.dtype)


def ffn_a(h, w_gate, w_up, w_down, conv_w, conv_b, conv_state, layer, *, m_prompt,
          m_sample, seq, t_new, tf, chunk, cast_rows):
    m_total, d = h.shape
    f = w_gate.shape[-1]
    nf = f // tf
    ni = m_prompt // seq
    assert t_new % 8 == 0 and t_new >= CONV_W - 1
    st = jnp.pad(conv_state, ((0, 0), (0, t_new - (CONV_W - 1)), (0, 0)))
    st = st.reshape(m_sample, f)
    sample_col = lambda i, j: (0, jnp.where(i == ni - 1, j, 0))
    wd_in, wd_out, wd_shape, cast_blocks = _side_cast_specs(
        w_down, layer, cast_rows, ni * nf, nf
    )
    return pl.pallas_call(
        functools.partial(_ffn_a_kernel, chunk=chunk, t_new=t_new,
                          cast_blocks=cast_blocks),
        grid=(ni, nf),
        in_specs=[
            pl.BlockSpec((seq, d), lambda i, j: (i, 0), pipeline_mode=pl.Buffered(1)),
            pl.BlockSpec((m_sample, d), lambda i, j: (m_prompt // m_sample, 0)),
            pl.BlockSpec((None, d, tf), lambda i, j: (layer, 0, j)),
            pl.BlockSpec((None, d, tf), lambda i, j: (layer, 0, j)),
            pl.BlockSpec((None, CONV_W, tf), lambda i, j: (layer, 0, j)),
            pl.BlockSpec((None, 1, tf), lambda i, j: (layer, 0, j)),
            pl.BlockSpec((m_sample, tf), lambda i, j: (0, j)),
            wd_in,
        ],
        out_specs=[
            pl.BlockSpec((seq, tf), lambda i, j: (i, j)),
            pl.BlockSpec((None, 8, tf), lambda i, j: (i, 0, j)),
            pl.BlockSpec((m_sample, tf), sample_col),
            pl.BlockSpec((m_sample, tf), sample_col),
            wd_out,
        ],
        out_shape=[
            jax.ShapeDtypeStruct((m_total, f), BF16),
            jax.ShapeDtypeStruct((ni, 8, f), F32),
            jax.ShapeDtypeStruct((m_sample, f), BF16),
            jax.ShapeDtypeStruct((m_sample, f), F32),
            wd_shape,
        ],
        compiler_params=_params(("arbitrary", "arbitrary"), 56),
        name="ffn_a",
    )(h, h, w_gate, w_up, conv_w, conv_b, st, w_down)


def kernel(x_prompt, x_sample, state_swa_k, state_swa_v, state_conv, w_in, w_out,
           attn_sinks, gm_spatial, gm_bias, gm_norm, norm_pre_mix, norm_post_mix,
           norm_pre_ffn, norm_post_ffn, w_ffn_gate, w_ffn_up, w_ffn_down, conv_w,
           conv_b):
    batch, seq, d = x_prompt.shape
    dec_batch, dec_seq, _ = x_sample.shape
    depth = w_in.shape[0]
    f = w_ffn_gate.shape[-1]
    mp = batch * seq
    ms = dec_batch * dec_seq
    m = mp + ms
    keep = min(WINDOW, seq)

    tm_mm = m // 8
    tm_down = m // 11
    tf = 256
    tm_rows = ms
    assert mp % tm_rows == 0

    w_in_b = w_in[0].astype(BF16)[None]
    w_out_b = w_out[0].astype(BF16).reshape(1, 2, Q_COLS, d)
    conv_b3 = conv_b.reshape(depth, 1, f)
    cast_rows = 64

    x = (x_prompt.reshape(mp, d), x_sample.reshape(ms, d))
    h = rms_rows(x, norm_pre_mix[0].reshape(1, d), tm=tm_rows)

    p_k, p_v, p_c, s_k, s_v, s_c, s_g = [], [], [], [], [], [], []
    for l in range(depth):
        last = l + 1 == depth
        tn = 512
        q, _ = matmul_cols(h, w_in_b, 0, col_block0=0, n=Q_COLS, tm=tm_mm, tn=tn,
                           out_dtype=BF16, epilogue=_scale_q, name="mm_q")
        kv, _ = matmul_cols(h, w_in_b, 0, col_block0=Q_COLS // tn, n=2 * KV_COLS,
                            tm=tm_mm, tn=tn, out_dtype=F32, epilogue=_identity,
                            name="mm_kv")
        gm, w_out_next = matmul_cols(
            h, w_in_b, 0, col_block0=(Q_COLS + 2 * KV_COLS) // tn, n=2 * GM_WIDTH,
            tm=tm_mm, tn=tn, out_dtype=BF16, epilogue=_gelu, name="mm_gm",
            cast=None if last else (w_out, l + 1, cast_rows))

        kv_p = kv[:mp].reshape(batch, seq, 2 * KV_COLS)[:, seq - keep:]
        kv_s = kv[mp:]
        k_s, v_s = kv_s[:, :KV_COLS], kv_s[:, KV_COLS:]
        p_k.append(kv_p[..., :KV_COLS].reshape(batch, keep, N_KV_HEADS, HEAD_DIM))
        p_v.append(kv_p[..., KV_COLS:].reshape(batch, keep, N_KV_HEADS, HEAD_DIM))
        s_k.append(k_s.reshape(dec_batch, dec_seq, N_KV_HEADS, HEAD_DIM))
        s_v.append(v_s.reshape(dec_batch, dec_seq, N_KV_HEADS, HEAD_DIM))

        att = attn_prompt(q, kv, attn_sinks[l], m_total=m, m_prompt=mp, seq=seq)
        att_s = attn_sample(q[mp:], k_s, v_s, state_swa_k[l], state_swa_v[l],
                            attn_sinks[l], nb=8)
        att = place_rows(att, att_s, row_block=mp // ms)

        gmo = gmlp_prompt(gm, gm_spatial[l], gm_bias[l].T, gm_norm[l].reshape(1, -1),
                          m_total=m, m_prompt=mp)
        gmo, gv_s = gmlp_sample(gm, gmo, gm_spatial[l], gm_bias[l],
                                gm_norm[l].reshape(1, -1), m_prompt=mp, m_sample=ms,
                                t_new=dec_seq)
        s_g.append(gv_s.reshape(dec_batch, dec_seq, GM_HEADS, GM_HEAD_DIM))

        mix, w_in_next = matmul_two(att, gmo, w_out_b, 0, tm=tm_mm, tn=512, out_dtype=BF16,
                                    name="mm_out",
                                    cast=None if last else (w_in, l + 1, cast_rows))
        x, h = post_rows(mix, x, norm_post_mix[l].reshape(1, d),
                         norm_pre_ffn[l].reshape(1, d), tm=tm_rows)

        a, tail, a_s, g_s, w_down_b = ffn_a(
            h, w_ffn_gate, w_ffn_up, w_ffn_down, conv_w, conv_b3, state_conv[l], l,
            m_prompt=mp, m_sample=ms, seq=seq, t_new=dec_seq, tf=tf, chunk=512,
            cast_rows=2 * cast_rows)
        a = place_rows(a, a_s, row_block=mp // ms)
        p_c.append(tail[:, 8 - (CONV_W - 1):])
        s_c.append(g_s.reshape(dec_batch, dec_seq, f)[:, dec_seq - (CONV_W - 1):])

        ffn, _ = matmul_cols(a, w_down_b[None], 0, col_block0=0, n=d, tm=tm_down, tn=256,
                             out_dtype=BF16, epilogue=_identity, name="mm_down")
        x, h = post_rows(ffn, x, norm_post_ffn[l].reshape(1, d),
                         None if last else norm_pre_mix[l + 1].reshape(1, d),
                         tm=tm_rows, split_out=(mp, ms) if last else None)
        if not last:
            w_in_b = w_in_next[None]
            w_out_b = w_out_next.reshape(1, 2, Q_COLS, d)

    x_p, x_s = x
    return (x_p.reshape(batch, seq, d), x_s.reshape(dec_batch, dec_seq, d),
            jnp.stack(p_k), jnp.stack(p_v), jnp.stack(p_c),
            jnp.stack(s_k), jnp.stack(s_v), jnp.stack(s_c), jnp.stack(s_g))
```

```python
import functools
import math

import jax
import jax.numpy as jnp
from jax import lax
from jax.experimental import pallas as pl
from jax.experimental.pallas import tpu as pltpu

F32 = jnp.float32
BF16 = jnp.bfloat16

D_MODEL = 4096
HEAD_DIM = 64
N_HEADS = 32
N_KV_HEADS = 4
GQA_GROUP = 8
WINDOW = 128
CHUNK = 128
GM_HEAD_DIM = 128
GM_HEADS = 16
GM_WIDTH = 2048
Q_COLS = 2048
KV_COLS = 256
CONV_W = 3
EPS = 1e-6
LANES = 128

MIB = 1024 * 1024


def _params(semantics, vmem_mib):
    return pltpu.CompilerParams(
        dimension_semantics=semantics, vmem_limit_bytes=vmem_mib * MIB
    )


def _rms(x, g):
    ms = jnp.mean(x * x, axis=-1, keepdims=True)
    return x * lax.rsqrt(ms + EPS) * g


def _split_specs(tm, d, n_prompt_tiles):
    return [
        pl.BlockSpec((tm, d), lambda i: (jnp.minimum(i, n_prompt_tiles - 1), 0)),
        pl.BlockSpec((tm, d), lambda i: (0, 0)),
    ]


def _load_rows(refs, n_prompt_tiles):
    if len(refs) == 1:
        return refs[0][...]
    is_prompt = pl.program_id(0) < n_prompt_tiles
    return jnp.where(is_prompt, refs[0][...], refs[1][...])


def _store_rows(refs, value, n_prompt_tiles):
    if len(refs) == 1:
        refs[0][...] = value
        return

    @pl.when(pl.program_id(0) < n_prompt_tiles)
    def _():
        refs[0][...] = value

    @pl.when(pl.program_id(0) >= n_prompt_tiles)
    def _():
        refs[1][...] = value


def _rms_kernel(*refs, n_in, n_prompt_tiles):
    x = _load_rows(refs[:n_in], n_prompt_tiles)
    g_ref, o_ref = refs[n_in:]
    o_ref[...] = _rms(x, g_ref[...]).astype(o_ref.dtype)


def rms_rows(x_parts, g, *, tm):
    d = x_parts[0].shape[1]
    m = sum(p.shape[0] for p in x_parts)
    npt = x_parts[0].shape[0] // tm
    row = pl.BlockSpec((tm, d), lambda i: (i, 0))
    x_specs = [row] if len(x_parts) == 1 else _split_specs(tm, d, npt)
    return pl.pallas_call(
        functools.partial(_rms_kernel, n_in=len(x_parts), n_prompt_tiles=npt),
        grid=(m // tm,),
        in_specs=x_specs + [pl.BlockSpec((1, d), lambda i: (0, 0))],
        out_specs=row,
        out_shape=jax.ShapeDtypeStruct((m, d), BF16),
        compiler_params=_params(("arbitrary",), 40),
        name="rms_rows",
    )(*x_parts, g)


def _post_kernel(*refs, n_in, n_out, has_next, n_prompt_tiles):
    m_ref = refs[0]
    x = _load_rows(refs[1:1 + n_in], n_prompt_tiles)
    rest = refs[1 + n_in:]
    xn = x + _rms(m_ref[...].astype(F32), rest[0][...])
    rest = rest[1:]
    if has_next:
        gnext_ref, rest = rest[0], rest[1:]
    _store_rows(rest[:n_out], xn, n_prompt_tiles)
    if has_next:
        ho_ref = rest[n_out]
        ho_ref[...] = _rms(xn, gnext_ref[...]).astype(ho_ref.dtype)


def post_rows(m_in, x_parts, g_post, g_next, *, tm, split_out=None):
    m, d = m_in.shape
    npt = (x_parts[0].shape[0] if split_out is None else split_out[0]) // tm
    if len(x_parts) == 1 and split_out is None:
        npt = m // tm
    row = pl.BlockSpec((tm, d), lambda i: (i, 0))
    vec = pl.BlockSpec((1, d), lambda i: (0, 0))
    x_specs = [row] if len(x_parts) == 1 else _split_specs(tm, d, npt)
    if split_out is None:
        xo_specs = [row]
        xo_shapes = [jax.ShapeDtypeStruct((m, d), F32)]
    else:
        xo_specs = _split_specs(tm, d, npt)
        xo_shapes = [jax.ShapeDtypeStruct((r, d), F32) for r in split_out]
    has_next = g_next is not None
    gains = [g_post, g_next] if has_next else [g_post]
    out = pl.pallas_call(
        functools.partial(_post_kernel, n_in=len(x_parts), n_out=len(xo_specs),
                          has_next=has_next, n_prompt_tiles=npt),
        grid=(m // tm,),
        in_specs=[row] + x_specs + [vec] * len(gains),
        out_specs=xo_specs + ([row] if has_next else []),
        out_shape=xo_shapes
        + ([jax.ShapeDtypeStruct((m, d), BF16)] if has_next else []),
        compiler_params=_params(("arbitrary",), 48),
        name="post_rows",
    )(m_in, *x_parts, *gains)
    x_new = out[:len(xo_specs)]
    return (x_new, out[-1]) if has_next else (x_new, None)


def _grid_step():
    return pl.program_id(0) * pl.num_programs(1) + pl.program_id(1)


def _side_cast(src_ref, dst_ref, n_blocks):
    @pl.when(_grid_step() < n_blocks)
    def _():
        dst_ref[...] = src_ref[...].astype(dst_ref.dtype)


def _side_cast_specs(src, layer, block_rows, n_steps, n_cols_grid):
    _, rows, cols = src.shape
    n_blocks = rows // block_rows
    assert rows % block_rows == 0 and n_blocks <= n_steps

    def block(i, j):
        return jnp.minimum(i * n_cols_grid + j, n_blocks - 1)

    in_spec = pl.BlockSpec((None, block_rows, cols), lambda i, j: (layer, block(i, j), 0))
    out_spec = pl.BlockSpec((block_rows, cols), lambda i, j: (block(i, j), 0))
    return in_spec, out_spec, jax.ShapeDtypeStruct((rows, cols), BF16), n_blocks


def _mm_kernel(a_ref, b_ref, *rest, epilogue, cast_blocks):
    o_ref = rest[-2] if cast_blocks else rest[-1]
    acc = jnp.dot(a_ref[...], b_ref[...], preferred_element_type=F32)
    o_ref[...] = epilogue(acc).astype(o_ref.dtype)
    if cast_blocks:
        _side_cast(rest[0], rest[-1], cast_blocks)


def _mm2_kernel(a1_ref, a2_ref, b1_ref, b2_ref, *rest, cast_blocks):
    o_ref = rest[-2] if cast_blocks else rest[-1]
    acc = jnp.dot(a1_ref[...], b1_ref[...], preferred_element_type=F32)
    acc = acc + jnp.dot(a2_ref[...], b2_ref[...], preferred_element_type=F32)
    o_ref[...] = acc.astype(o_ref.dtype)
    if cast_blocks:
        _side_cast(rest[0], rest[-1], cast_blocks)


def _with_side_cast(cast, grid, in_specs, out_specs, out_shapes, operands):
    if cast is None:
        return 0
    src, layer, block_rows = cast
    i_spec, o_spec, o_shape, n_blocks = _side_cast_specs(
        src, layer, block_rows, grid[0] * grid[1], grid[1]
    )
    in_specs.append(i_spec)
    out_specs.append(o_spec)
    out_shapes.append(o_shape)
    operands.append(src)
    return n_blocks


def matmul_cols(a, w, layer, *, col_block0, n, tm, tn, out_dtype, epilogue, name,
                cast=None):
    m, k = a.shape
    grid = (m // tm, n // tn)
    in_specs = [
        pl.BlockSpec((tm, k), lambda i, j: (i, 0)),
        pl.BlockSpec((None, k, tn), lambda i, j: (layer, 0, j + col_block0)),
    ]
    out_specs = [pl.BlockSpec((tm, tn), lambda i, j: (i, j))]
    out_shapes = [jax.ShapeDtypeStruct((m, n), out_dtype)]
    operands = [a, w]
    cast_blocks = _with_side_cast(cast, grid, in_specs, out_specs, out_shapes, operands)
    out = pl.pallas_call(
        functools.partial(_mm_kernel, epilogue=epilogue, cast_blocks=cast_blocks),
        grid=grid,
        in_specs=in_specs,
        out_specs=out_specs,
        out_shape=out_shapes,
        compiler_params=_params(("arbitrary", "arbitrary"), 56),
        name=name,
    )(*operands)
    return out[0], (out[1] if cast_blocks else None)


def matmul_two(a1, a2, w, layer, *, tm, tn, out_dtype, name, cast=None):
    m, k1 = a1.shape
    k2 = a2.shape[1]
    assert k1 == k2
    n = w.shape[-1]
    grid = (m // tm, n // tn)
    in_specs = [
        pl.BlockSpec((tm, k1), lambda i, j: (i, 0)),
        pl.BlockSpec((tm, k2), lambda i, j: (i, 0)),
        pl.BlockSpec((None, None, k1, tn), lambda i, j: (layer, 0, 0, j)),
        pl.BlockSpec((None, None, k2, tn), lambda i, j: (layer, 1, 0, j)),
    ]
    out_specs = [pl.BlockSpec((tm, tn), lambda i, j: (i, j))]
    out_shapes = [jax.ShapeDtypeStruct((m, n), out_dtype)]
    operands = [a1, a2, w, w]
    cast_blocks = _with_side_cast(cast, grid, in_specs, out_specs, out_shapes, operands)
    out = pl.pallas_call(
        functools.partial(_mm2_kernel, cast_blocks=cast_blocks),
        grid=grid,
        in_specs=in_specs,
        out_specs=out_specs,
        out_shape=out_shapes,
        compiler_params=_params(("arbitrary", "arbitrary"), 56),
        name=name,
    )(*operands)
    return out[0], (out[1] if cast_blocks else None)


def _identity(x):
    return x


def _scale_q(x):
    return x * (HEAD_DIM ** -0.5)


_GELU_A = -2.0 * math.log2(math.e) * math.sqrt(2.0 / math.pi)
_GELU_B = _GELU_A * 0.044715


def _gelu_times(x, y):
    t = x * x * _GELU_B + _GELU_A
    return (x * y) / (1.0 + jnp.exp2(t * x))


def _gelu(x):
    t = x * x * _GELU_B + _GELU_A
    return x / (1.0 + jnp.exp2(t * x))


def _softmax_with_sink(s, sink):
    m = jnp.maximum(jnp.max(s, axis=-1, keepdims=True), sink)
    e = jnp.exp(s - m)
    den = jnp.sum(e, axis=-1, keepdims=True) + jnp.exp(sink - m)
    return e / den


def _attn_prompt_kernel(sink_ref, q_ref, kvc_ref, kvp_ref, o_ref, *, blocks_per_seq):
    n = pl.program_id(0) % blocks_per_seq
    w = WINDOW
    row = lax.broadcasted_iota(jnp.int32, (w, 2 * w), 0)
    col = lax.broadcasted_iota(jnp.int32, (w, 2 * w), 1)
    prev_ok = jnp.logical_and(jnp.logical_and(col < w, col > row), n > 0)
    own_ok = jnp.logical_and(col >= w, col - w <= row)
    valid = jnp.logical_or(prev_ok, own_ok)
    for kvh in range(N_KV_HEADS):
        ks = slice(kvh * HEAD_DIM, (kvh + 1) * HEAD_DIM)
        vs = slice(KV_COLS + kvh * HEAD_DIM, KV_COLS + (kvh + 1) * HEAD_DIM)
        k = jnp.concatenate([kvp_ref[:, ks], kvc_ref[:, ks]], axis=0).astype(BF16)
        v = jnp.concatenate([kvp_ref[:, vs], kvc_ref[:, vs]], axis=0).astype(BF16)
        v_ext = jnp.concatenate(
            [v, jnp.zeros((2 * w, LANES - HEAD_DIM), BF16), jnp.ones((2 * w, LANES), BF16)],
            axis=1,
        )
        heads = [kvh * GQA_GROUP + g for g in range(GQA_GROUP)]
        qs = jnp.concatenate(
            [q_ref[:, h * HEAD_DIM:(h + 1) * HEAD_DIM] for h in heads], axis=0
        )
        s = lax.dot_general(
            qs, k, (((1,), (1,)), ((), ())), preferred_element_type=F32
        )
        es, sink_terms = [], []
        for g, h in enumerate(heads):
            sg = jnp.where(valid, s[g * w:(g + 1) * w], -jnp.inf)
            m = jnp.maximum(jnp.max(sg, axis=-1, keepdims=True), sink_ref[h])
            es.append(jnp.exp(sg - m).astype(BF16))
            sink_terms.append(jnp.exp(sink_ref[h] - m))
        o = jnp.dot(jnp.concatenate(es, axis=0), v_ext, preferred_element_type=F32)
        for g, h in enumerate(heads):
            og = o[g * w:(g + 1) * w]
            out = og[:, :LANES] / (og[:, LANES:] + sink_terms[g])
            o_ref[:, h * HEAD_DIM:(h + 1) * HEAD_DIM] = out[:, :HEAD_DIM].astype(
                o_ref.dtype
            )


def attn_prompt(q, kv, sinks, *, m_total, m_prompt, seq):
    w = WINDOW
    nblk = m_prompt // w
    return pl.pallas_call(
        functools.partial(_attn_prompt_kernel, blocks_per_seq=seq // w),
        grid=(nblk,),
        in_specs=[
            pl.BlockSpec(memory_space=pltpu.SMEM),
            pl.BlockSpec((w, Q_COLS), lambda r: (r, 0)),
            pl.BlockSpec((w, 2 * KV_COLS), lambda r: (r, 0)),
            pl.BlockSpec((w, 2 * KV_COLS), lambda r: (jnp.maximum(r - 1, 0), 0)),
        ],
        out_specs=pl.BlockSpec((w, Q_COLS), lambda r: (r, 0)),
        out_shape=jax.ShapeDtypeStruct((m_total, Q_COLS), BF16),
        compiler_params=_params(("parallel",), 40),
        name="attn_prompt",
    )(sinks, q, kv, kv)


def _attn_sample_kernel(sink_ref, q_ref, k_ref, v_ref, o_ref, *, t_new):
    nb, nkv, rows, d = q_ref.shape
    keys = k_ref.shape[2]
    q = q_ref[...].reshape(nb * nkv, rows, d)
    k = k_ref[...].reshape(nb * nkv, keys, d)
    v = v_ref[...].reshape(nb * nkv, keys, d)
    s = jnp.einsum("bqd,bkd->bqk", q, k, preferred_element_type=F32)
    t = lax.broadcasted_iota(jnp.int32, (rows, keys), 0) // GQA_GROUP
    col = lax.broadcasted_iota(jnp.int32, (rows, keys), 1)
    valid = jnp.logical_and(col > t, col <= WINDOW + t)
    s = jnp.where(valid[None], s, -jnp.inf)
    sink = jnp.broadcast_to(sink_ref[...][None], (nb, nkv, rows, 1)).reshape(
        nb * nkv, rows, 1
    )
    p = _softmax_with_sink(s, sink).astype(BF16)
    o = jnp.einsum("bqk,bkd->bqd", p, v, preferred_element_type=F32)
    o_ref[...] = o.reshape(nb, nkv, rows, d).astype(o_ref.dtype)


def attn_sample(q_s, k_new, v_new, k_buf, v_buf, sinks, *, nb):
    b, l = k_buf.shape[0], k_buf.shape[1]
    t_new = q_s.shape[0] // b
    rows = t_new * GQA_GROUP
    keys = 2 * WINDOW
    q4 = q_s.reshape(b, t_new, N_KV_HEADS, GQA_GROUP, HEAD_DIM)
    q4 = q4.transpose(0, 2, 1, 3, 4).reshape(b, N_KV_HEADS, rows, HEAD_DIM)

    def keys_layout(buf, new):
        cat = jnp.concatenate(
            [buf, new.reshape(b, t_new, N_KV_HEADS, HEAD_DIM)], axis=1
        )
        cat = jnp.pad(cat, ((0, 0), (0, keys - l - t_new), (0, 0), (0, 0)))
        return cat.transpose(0, 2, 1, 3).astype(BF16)

    k4 = keys_layout(k_buf, k_new)
    v4 = keys_layout(v_buf, v_new)
    sink_rows = jnp.broadcast_to(
        sinks.reshape(N_KV_HEADS, 1, GQA_GROUP), (N_KV_HEADS, t_new, GQA_GROUP)
    ).reshape(N_KV_HEADS, rows, 1)
    o4 = pl.pallas_call(
        functools.partial(_attn_sample_kernel, t_new=t_new),
        grid=(b // nb,),
        in_specs=[
            pl.BlockSpec((N_KV_HEADS, rows, 1), lambda i: (0, 0, 0)),
            pl.BlockSpec((nb, N_KV_HEADS, rows, HEAD_DIM), lambda i: (i, 0, 0, 0)),
            pl.BlockSpec((nb, N_KV_HEADS, keys, HEAD_DIM), lambda i: (i, 0, 0, 0)),
            pl.BlockSpec((nb, N_KV_HEADS, keys, HEAD_DIM), lambda i: (i, 0, 0, 0)),
        ],
        out_specs=pl.BlockSpec(
            (nb, N_KV_HEADS, rows, HEAD_DIM), lambda i: (i, 0, 0, 0)
        ),
        out_shape=jax.ShapeDtypeStruct((b, N_KV_HEADS, rows, HEAD_DIM), BF16),
        compiler_params=_params(("parallel",), 40),
        name="attn_sample",
    )(sink_rows, q4, k4, v4)
    o = o4.reshape(b, N_KV_HEADS, t_new, GQA_GROUP, HEAD_DIM).transpose(0, 2, 1, 3, 4)
    return o.reshape(b * t_new, Q_COLS)


def _place_rows_kernel(rows_ref, buf_ref, o_ref):
    del buf_ref
    o_ref[...] = rows_ref[...]


def place_rows(buf, rows, *, row_block):
    r, c = rows.shape
    return pl.pallas_call(
        _place_rows_kernel,
        grid=(1,),
        in_specs=[
            pl.BlockSpec((r, c), lambda i: (0, 0)),
            pl.BlockSpec(memory_space=pl.ANY),
        ],
        out_specs=pl.BlockSpec((r, c), lambda i: (row_block, 0)),
        out_shape=jax.ShapeDtypeStruct(buf.shape, buf.dtype),
        input_output_aliases={1: 0},
        compiler_params=_params(("arbitrary",), 16),
        name="place_rows",
    )(rows, buf)


def _gmlp_prompt_kernel(u_ref, v_ref, w_ref, bt_ref, gn_ref, o_ref):
    gv = _rms(v_ref[...].astype(F32), gn_ref[...])
    c = CHUNK
    row = lax.broadcasted_iota(jnp.int32, (c, c), 0)
    col = lax.broadcasted_iota(jnp.int32, (c, c), 1)
    causal = col <= row
    for h in range(GM_HEADS):
        cs = slice(h * GM_HEAD_DIM, (h + 1) * GM_HEAD_DIM)
        w = jnp.where(causal, w_ref[h], 0.0).astype(BF16)
        mixed = jnp.dot(w, gv[:, cs].astype(BF16), preferred_element_type=F32)
        mixed = mixed + bt_ref[:, h:h + 1]
        o_ref[:, cs] = (u_ref[:, cs].astype(F32) * mixed).astype(o_ref.dtype)


def gmlp_prompt(gm, w_s, b_t, g_norm, *, m_total, m_prompt):
    c = CHUNK
    return pl.pallas_call(
        _gmlp_prompt_kernel,
        grid=(m_prompt // c,),
        in_specs=[
            pl.BlockSpec((c, GM_WIDTH), lambda r: (r, 0)),
            pl.BlockSpec((c, GM_WIDTH), lambda r: (r, 1)),
            pl.BlockSpec((GM_HEADS, c, c), lambda r: (0, 0, 0)),
            pl.BlockSpec((c, GM_HEADS), lambda r: (0, 0)),
            pl.BlockSpec((1, GM_WIDTH), lambda r: (0, 0)),
        ],
        out_specs=pl.BlockSpec((c, GM_WIDTH), lambda r: (r, 0)),
        out_shape=jax.ShapeDtypeStruct((m_total, GM_WIDTH), BF16),
        compiler_params=_params(("parallel",), 40),
        name="gmlp_prompt",
    )(gm, gm, w_s, b_t, g_norm)


def _gmlp_sample_kernel(u_ref, v_ref, wc_ref, bc_ref, gn_ref, obuf_ref, o_ref, gv_ref, *,
                        t_new):
    del obuf_ref
    rows = u_ref.shape[0]
    nseq = rows // t_new
    gv = _rms(v_ref[...].astype(F32), gn_ref[...])
    gv_ref[...] = gv
    t = lax.broadcasted_iota(jnp.int32, (t_new, GM_HEAD_DIM), 0)
    for h in range(GM_HEADS):
        cs = slice(h * GM_HEAD_DIM, (h + 1) * GM_HEAD_DIM)
        x = gv[:, cs].reshape(nseq, t_new, GM_HEAD_DIM)
        acc = jnp.zeros((nseq, t_new, GM_HEAD_DIM), F32)
        for s in range(t_new):
            coef = jnp.where(t >= s, wc_ref[h, s], 0.0)
            acc = acc + coef[None] * x[:, s:s + 1, :]
        mixed = acc + bc_ref[h][None]
        u = u_ref[:, cs].astype(F32).reshape(nseq, t_new, GM_HEAD_DIM)
        o_ref[:, cs] = (u * mixed).reshape(rows, GM_HEAD_DIM).astype(o_ref.dtype)


def gmlp_sample(gm, out_buf, w_s, b_s, g_norm, *, m_prompt, m_sample, t_new):
    lanes = GM_HEAD_DIM
    w8 = w_s[:, :t_new, :t_new]
    wc = jnp.broadcast_to(
        w8.transpose(0, 2, 1)[..., None], (GM_HEADS, t_new, t_new, lanes)
    )
    bc = jnp.broadcast_to(b_s[:, :t_new, None], (GM_HEADS, t_new, lanes))
    blk = m_prompt // m_sample
    return pl.pallas_call(
        functools.partial(_gmlp_sample_kernel, t_new=t_new),
        grid=(1,),
        in_specs=[
            pl.BlockSpec((m_sample, GM_WIDTH), lambda i: (blk, 0)),
            pl.BlockSpec((m_sample, GM_WIDTH), lambda i: (blk, 1)),
            pl.BlockSpec((GM_HEADS, t_new, t_new, lanes), lambda i: (0, 0, 0, 0)),
            pl.BlockSpec((GM_HEADS, t_new, lanes), lambda i: (0, 0, 0)),
            pl.BlockSpec((1, GM_WIDTH), lambda i: (0, 0)),
            pl.BlockSpec(memory_space=pl.ANY),
        ],
        out_specs=[
            pl.BlockSpec((m_sample, GM_WIDTH), lambda i: (blk, 0)),
            pl.BlockSpec((m_sample, GM_WIDTH), lambda i: (0, 0)),
        ],
        out_shape=[
            jax.ShapeDtypeStruct(out_buf.shape, BF16),
            jax.ShapeDtypeStruct((m_sample, GM_WIDTH), F32),
        ],
        input_output_aliases={5: 0},
        compiler_params=_params(("arbitrary",), 40),
        name="gmlp_sample",
    )(gm, gm, wc, bc, g_norm, out_buf)


def _conv_gelu_gate(g, g1, g2, up, cw_ref, cb_ref):
    c = cb_ref[...] + cw_ref[0:1, :] * g2
    c = c + cw_ref[1:2, :] * g1
    c = c + cw_ref[2:3, :] * g
    return _gelu_times(c, up)


def _ffn_a_kernel(h_ref, hs_ref, wg_ref, wu_ref, cw_ref, cb_ref, st_ref, wd_ref, a_ref,
                  tail_ref, as_ref, gs_ref, wdb_ref, *, chunk, t_new, cast_blocks):
    i = pl.program_id(0)
    _side_cast(wd_ref, wdb_ref, cast_blocks)
    tm = h_ref.shape[0]
    tf = wg_ref.shape[1]
    wg = wg_ref[...].astype(BF16)
    wu = wu_ref[...].astype(BF16)

    prev8 = jnp.zeros((8, tf), F32)
    for r0 in range(0, tm, chunk):
        h = h_ref[r0:r0 + chunk, :]
        g = jnp.dot(h, wg, preferred_element_type=F32)
        up = jnp.dot(h, wu, preferred_element_type=F32)
        ext = jnp.concatenate([prev8, g], axis=0)
        g1 = pltpu.roll(ext, 1, 0)[8:]
        g2 = pltpu.roll(ext, 2, 0)[8:]
        a_ref[r0:r0 + chunk, :] = _conv_gelu_gate(g, g1, g2, up, cw_ref, cb_ref).astype(
            a_ref.dtype
        )
        prev8 = g[chunk - 8:, :]
    tail_ref[...] = prev8

    @pl.when(i == pl.num_programs(0) - 1)
    def _():
        hs = hs_ref[...]
        rows = hs.shape[0]
        g = jnp.dot(hs, wg, preferred_element_type=F32)
        up = jnp.dot(hs, wu, preferred_element_type=F32)
        gs_ref[...] = g
        t = lax.broadcasted_iota(jnp.int32, g.shape, 0) % t_new
        st2 = st_ref[...]
        st1 = pltpu.roll(st2, rows - 1, 0)
        g1 = jnp.where(t >= 1, pltpu.roll(g, 1, 0), st1)
        g2 = jnp.where(t >= 2, pltpu.roll(g, 2, 0), st2)
        as_ref[...] = _conv_gelu_gate(g, g1, g2, up, cw_ref, cb_ref).astype(as_ref.dtype)


def ffn_a(h, w_gate, w_up, w_down, conv_w, conv_b, conv_state, layer, *, m_prompt,
          m_sample, seq, t_new, tf, chunk, cast_rows):
    m_total, d = h.shape
    f = w_gate.shape[-1]
    nf = f // tf
    ni = m_prompt // seq
    assert t_new % 8 == 0 and t_new >= CONV_W - 1
    st = jnp.pad(conv_state, ((0, 0), (0, t_new - (CONV_W - 1)), (0, 0)))
    st = st.reshape(m_sample, f)
    sample_col = lambda i, j: (0, jnp.where(i == ni - 1, j, 0))
    wd_in, wd_out, wd_shape, cast_blocks = _side_cast_specs(
        w_down, layer, cast_rows, ni * nf, nf
    )
    return pl.pallas_call(
        functools.partial(_ffn_a_kernel, chunk=chunk, t_new=t_new,
                          cast_blocks=cast_blocks),
        grid=(ni, nf),
        in_specs=[
            pl.BlockSpec((seq, d), lambda i, j: (i, 0), pipeline_mode=pl.Buffered(1)),
            pl.BlockSpec((m_sample, d), lambda i, j: (m_prompt // m_sample, 0)),
            pl.BlockSpec((None, d, tf), lambda i, j: (layer, 0, j)),
            pl.BlockSpec((None, d, tf), lambda i, j: (layer, 0, j)),
            pl.BlockSpec((None, CONV_W, tf), lambda i, j: (layer, 0, j)),
            pl.BlockSpec((None, 1, tf), lambda i, j: (layer, 0, j)),
            pl.BlockSpec((m_sample, tf), lambda i, j: (0, j)),
            wd_in,
        ],
        out_specs=[
            pl.BlockSpec((seq, tf), lambda i, j: (i, j)),
            pl.BlockSpec((None, 8, tf), lambda i, j: (i, 0, j)),
            pl.BlockSpec((m_sample, tf), sample_col),
            pl.BlockSpec((m_sample, tf), sample_col),
            wd_out,
        ],
        out_shape=[
            jax.ShapeDtypeStruct((m_total, f), BF16),
            jax.ShapeDtypeStruct((ni, 8, f), F32),
            jax.ShapeDtypeStruct((m_sample, f), BF16),
            jax.ShapeDtypeStruct((m_sample, f), F32),
            wd_shape,
        ],
        compiler_params=_params(("arbitrary", "arbitrary"), 56),
        name="ffn_a",
    )(h, h, w_gate, w_up, conv_w, conv_b, st, w_down)


def kernel(x_prompt, x_sample, state_swa_k, state_swa_v, state_conv, w_in, w_out,
           attn_sinks, gm_spatial, gm_bias, gm_norm, norm_pre_mix, norm_post_mix,
           norm_pre_ffn, norm_post_ffn, w_ffn_gate, w_ffn_up, w_ffn_down, conv_w,
           conv_b):
    batch, seq, d = x_prompt.shape
    dec_batch, dec_seq, _ = x_sample.shape
    depth = w_in.shape[0]
    f = w_ffn_gate.shape[-1]
    mp = batch * seq
    ms = dec_batch * dec_seq
    m = mp + ms
    keep = min(WINDOW, seq)

    tm_mm = m // 8
    tm_down = m // 11
    tf = 256
    tm_rows = ms
    assert mp % tm_rows == 0

    w_in_b = w_in[0].astype(BF16)[None]
    w_out_b = w_out[0].astype(BF16).reshape(1, 2, Q_COLS, d)
    conv_b3 = conv_b.reshape(depth, 1, f)
    cast_rows = 64

    x = (x_prompt.reshape(mp, d), x_sample.reshape(ms, d))
    h = rms_rows(x, norm_pre_mix[0].reshape(1, d), tm=tm_rows)

    p_k, p_v, p_c, s_k, s_v, s_c, s_g = [], [], [], [], [], [], []
    for l in range(depth):
        last = l + 1 == depth
        tn = 512
        q, _ = matmul_cols(h, w_in_b, 0, col_block0=0, n=Q_COLS, tm=tm_mm, tn=2 * tn,
                           out_dtype=BF16, epilogue=_scale_q, name="mm_q")
        kv, _ = matmul_cols(h, w_in_b, 0, col_block0=Q_COLS // tn, n=2 * KV_COLS,
                            tm=tm_mm, tn=tn, out_dtype=F32, epilogue=_identity,
                            name="mm_kv")
        gm, w_out_next = matmul_cols(
            h, w_in_b, 0, col_block0=(Q_COLS + 2 * KV_COLS) // tn, n=2 * GM_WIDTH,
            tm=tm_mm, tn=tn, out_dtype=BF16, epilogue=_gelu, name="mm_gm",
            cast=None if last else (w_out, l + 1, cast_rows))

        kv_p = jnp.stack([kv[(b + 1) * seq - keep:(b + 1) * seq] for b in range(batch)])
        kv_s = kv[mp:]
        k_s, v_s = kv_s[:, :KV_COLS], kv_s[:, KV_COLS:]
        p_k.append(kv_p[..., :KV_COLS].reshape(batch, keep, N_KV_HEADS, HEAD_DIM))
        p_v.append(kv_p[..., KV_COLS:].reshape(batch, keep, N_KV_HEADS, HEAD_DIM))
        s_k.append(k_s.reshape(dec_batch, dec_seq, N_KV_HEADS, HEAD_DIM))
        s_v.append(v_s.reshape(dec_batch, dec_seq, N_KV_HEADS, HEAD_DIM))

        att = attn_prompt(q, kv, attn_sinks[l], m_total=m, m_prompt=mp, seq=seq)
        att_s = attn_sample(q[mp:], k_s, v_s, state_swa_k[l], state_swa_v[l],
                            attn_sinks[l], nb=8)
        att = place_rows(att, att_s, row_block=mp // ms)

        gmo = gmlp_prompt(gm, gm_spatial[l], gm_bias[l].T, gm_norm[l].reshape(1, -1),
                          m_total=m, m_prompt=mp)
        gmo, gv_s = gmlp_sample(gm, gmo, gm_spatial[l], gm_bias[l],
                                gm_norm[l].reshape(1, -1), m_prompt=mp, m_sample=ms,
                                t_new=dec_seq)
        s_g.append(gv_s.reshape(dec_batch, dec_seq, GM_HEADS, GM_HEAD_DIM))

        mix, w_in_next = matmul_two(att, gmo, w_out_b, 0, tm=tm_mm, tn=1024,
                                    out_dtype=BF16, name="mm_out",
                                    cast=None if last else (w_in, l + 1, 2 * cast_rows))
        x, h = post_rows(mix, x, norm_post_mix[l].reshape(1, d),
                         norm_pre_ffn[l].reshape(1, d), tm=tm_rows)

        a, tail, a_s, g_s, w_down_b = ffn_a(
            h, w_ffn_gate, w_ffn_up, w_ffn_down, conv_w, conv_b3, state_conv[l], l,
            m_prompt=mp, m_sample=ms, seq=seq, t_new=dec_seq, tf=tf, chunk=512,
            cast_rows=2 * cast_rows)
        a = place_rows(a, a_s, row_block=mp // ms)
        p_c.append(tail[:, 8 - (CONV_W - 1):])
        s_c.append(g_s.reshape(dec_batch, dec_seq, f)[:, dec_seq - (CONV_W - 1):])

        ffn, _ = matmul_cols(a, w_down_b[None], 0, col_block0=0, n=d, tm=tm_down, tn=256,
                             out_dtype=BF16, epilogue=_identity, name="mm_down")
        x, h = post_rows(ffn, x, norm_post_ffn[l].reshape(1, d),
                         None if last else norm_pre_mix[l + 1].reshape(1, d),
                         tm=tm_rows, split_out=(mp, ms) if last else None)
        if not last:
            w_in_b = w_in_next[None]
            w_out_b = w_out_next.reshape(1, 2, Q_COLS, d)

    x_p, x_s = x
    return (x_p.reshape(batch, seq, d), x_s.reshape(dec_batch, dec_seq, d),
            jnp.stack(p_k), jnp.stack(p_v), jnp.stack(p_c),
            jnp.stack(s_k), jnp.stack(s_v), jnp.stack(s_c), jnp.stack(s_g))
```

```python
import functools
import math

import jax
import jax.numpy as jnp
from jax import lax
from jax.experimental import pallas as pl
from jax.experimental.pallas import tpu as pltpu

F32 = jnp.float32
BF16 = jnp.bfloat16

D_MODEL = 4096
HEAD_DIM = 64
N_HEADS = 32
N_KV_HEADS = 4
GQA_GROUP = 8
WINDOW = 128
CHUNK = 128
GM_HEAD_DIM = 128
GM_HEADS = 16
GM_WIDTH = 2048
Q_COLS = 2048
KV_COLS = 256
CONV_W = 3
EPS = 1e-6
LANES = 128

MIB = 1024 * 1024


def _params(semantics, vmem_mib):
    return pltpu.CompilerParams(
        dimension_semantics=semantics, vmem_limit_bytes=vmem_mib * MIB
    )


def _rms(x, g):
    ms = jnp.mean(x * x, axis=-1, keepdims=True)
    return x * lax.rsqrt(ms + EPS) * g


def _split_specs(tm, d, n_prompt_tiles):
    return [
        pl.BlockSpec((tm, d), lambda i: (jnp.minimum(i, n_prompt_tiles - 1), 0)),
        pl.BlockSpec((tm, d), lambda i: (0, 0)),
    ]


def _load_rows(refs, n_prompt_tiles):
    if len(refs) == 1:
        return refs[0][...]
    is_prompt = pl.program_id(0) < n_prompt_tiles
    return jnp.where(is_prompt, refs[0][...], refs[1][...])


def _store_rows(refs, value, n_prompt_tiles):
    if len(refs) == 1:
        refs[0][...] = value
        return

    @pl.when(pl.program_id(0) < n_prompt_tiles)
    def _():
        refs[0][...] = value

    @pl.when(pl.program_id(0) >= n_prompt_tiles)
    def _():
        refs[1][...] = value


def _rms_kernel(*refs, n_in, n_prompt_tiles):
    x = _load_rows(refs[:n_in], n_prompt_tiles)
    g_ref, o_ref = refs[n_in:]
    o_ref[...] = _rms(x, g_ref[...]).astype(o_ref.dtype)


def rms_rows(x_parts, g, *, tm):
    d = x_parts[0].shape[1]
    m = sum(p.shape[0] for p in x_parts)
    npt = x_parts[0].shape[0] // tm
    row = pl.BlockSpec((tm, d), lambda i: (i, 0))
    x_specs = [row] if len(x_parts) == 1 else _split_specs(tm, d, npt)
    return pl.pallas_call(
        functools.partial(_rms_kernel, n_in=len(x_parts), n_prompt_tiles=npt),
        grid=(m // tm,),
        in_specs=x_specs + [pl.BlockSpec((1, d), lambda i: (0, 0))],
        out_specs=row,
        out_shape=jax.ShapeDtypeStruct((m, d), BF16),
        compiler_params=_params(("arbitrary",), 40),
        name="rms_rows",
    )(*x_parts, g)


def _post_kernel(*refs, n_in, n_out, has_next, n_prompt_tiles):
    m_ref = refs[0]
    x = _load_rows(refs[1:1 + n_in], n_prompt_tiles)
    rest = refs[1 + n_in:]
    xn = x + _rms(m_ref[...].astype(F32), rest[0][...])
    rest = rest[1:]
    if has_next:
        gnext_ref, rest = rest[0], rest[1:]
    _store_rows(rest[:n_out], xn, n_prompt_tiles)
    if has_next:
        ho_ref = rest[n_out]
        ho_ref[...] = _rms(xn, gnext_ref[...]).astype(ho_ref.dtype)


def post_rows(m_in, x_parts, g_post, g_next, *, tm, split_out=None):
    m, d = m_in.shape
    npt = (x_parts[0].shape[0] if split_out is None else split_out[0]) // tm
    if len(x_parts) == 1 and split_out is None:
        npt = m // tm
    row = pl.BlockSpec((tm, d), lambda i: (i, 0))
    vec = pl.BlockSpec((1, d), lambda i: (0, 0))
    x_specs = [row] if len(x_parts) == 1 else _split_specs(tm, d, npt)
    if split_out is None:
        xo_specs = [row]
        xo_shapes = [jax.ShapeDtypeStruct((m, d), F32)]
    else:
        xo_specs = _split_specs(tm, d, npt)
        xo_shapes = [jax.ShapeDtypeStruct((r, d), F32) for r in split_out]
    has_next = g_next is not None
    gains = [g_post, g_next] if has_next else [g_post]
    out = pl.pallas_call(
        functools.partial(_post_kernel, n_in=len(x_parts), n_out=len(xo_specs),
                          has_next=has_next, n_prompt_tiles=npt),
        grid=(m // tm,),
        in_specs=[row] + x_specs + [vec] * len(gains),
        out_specs=xo_specs + ([row] if has_next else []),
        out_shape=xo_shapes
        + ([jax.ShapeDtypeStruct((m, d), BF16)] if has_next else []),
        compiler_params=_params(("arbitrary",), 48),
        name="post_rows",
    )(m_in, *x_parts, *gains)
    x_new = out[:len(xo_specs)]
    return (x_new, out[-1]) if has_next else (x_new, None)


def _grid_step():
    return pl.program_id(0) * pl.num_programs(1) + pl.program_id(1)


def _side_cast(src_ref, dst_ref, n_blocks):
    @pl.when(_grid_step() < n_blocks)
    def _():
        dst_ref[...] = src_ref[...].astype(dst_ref.dtype)


def _side_cast_specs(src, layer, block_rows, n_steps, n_cols_grid):
    _, rows, cols = src.shape
    n_blocks = rows // block_rows
    assert rows % block_rows == 0 and n_blocks <= n_steps

    def block(i, j):
        return jnp.minimum(i * n_cols_grid + j, n_blocks - 1)

    in_spec = pl.BlockSpec((None, block_rows, cols), lambda i, j: (layer, block(i, j), 0))
    out_spec = pl.BlockSpec((block_rows, cols), lambda i, j: (block(i, j), 0))
    return in_spec, out_spec, jax.ShapeDtypeStruct((rows, cols), BF16), n_blocks


def _mm_kernel(a_ref, b_ref, *rest, epilogue, cast_blocks):
    o_ref = rest[-2] if cast_blocks else rest[-1]
    acc = jnp.dot(a_ref[...], b_ref[...], preferred_element_type=F32)
    o_ref[...] = epilogue(acc).astype(o_ref.dtype)
    if cast_blocks:
        _side_cast(rest[0], rest[-1], cast_blocks)


def _mm2_kernel(a1_ref, a2_ref, b1_ref, b2_ref, *rest, cast_blocks):
    o_ref = rest[-2] if cast_blocks else rest[-1]
    acc = jnp.dot(a1_ref[...], b1_ref[...], preferred_element_type=F32)
    acc = acc + jnp.dot(a2_ref[...], b2_ref[...], preferred_element_type=F32)
    o_ref[...] = acc.astype(o_ref.dtype)
    if cast_blocks:
        _side_cast(rest[0], rest[-1], cast_blocks)


def _with_side_cast(cast, grid, in_specs, out_specs, out_shapes, operands):
    if cast is None:
        return 0
    src, layer, block_rows = cast
    i_spec, o_spec, o_shape, n_blocks = _side_cast_specs(
        src, layer, block_rows, grid[0] * grid[1], grid[1]
    )
    in_specs.append(i_spec)
    out_specs.append(o_spec)
    out_shapes.append(o_shape)
    operands.append(src)
    return n_blocks


def matmul_cols(a, w, layer, *, col_block0, n, tm, tn, out_dtype, epilogue, name,
                cast=None):
    m, k = a.shape
    grid = (m // tm, n // tn)
    in_specs = [
        pl.BlockSpec((tm, k), lambda i, j: (i, 0)),
        pl.BlockSpec((None, k, tn), lambda i, j: (layer, 0, j + col_block0)),
    ]
    out_specs = [pl.BlockSpec((tm, tn), lambda i, j: (i, j))]
    out_shapes = [jax.ShapeDtypeStruct((m, n), out_dtype)]
    operands = [a, w]
    cast_blocks = _with_side_cast(cast, grid, in_specs, out_specs, out_shapes, operands)
    out = pl.pallas_call(
        functools.partial(_mm_kernel, epilogue=epilogue, cast_blocks=cast_blocks),
        grid=grid,
        in_specs=in_specs,
        out_specs=out_specs,
        out_shape=out_shapes,
        compiler_params=_params(("arbitrary", "arbitrary"), 56),
        name=name,
    )(*operands)
    return out[0], (out[1] if cast_blocks else None)


def matmul_two(a1, a2, w, layer, *, tm, tn, out_dtype, name, cast=None):
    m, k1 = a1.shape
    k2 = a2.shape[1]
    assert k1 == k2
    n = w.shape[-1]
    grid = (m // tm, n // tn)
    in_specs = [
        pl.BlockSpec((tm, k1), lambda i, j: (i, 0)),
        pl.BlockSpec((tm, k2), lambda i, j: (i, 0)),
        pl.BlockSpec((None, None, k1, tn), lambda i, j: (layer, 0, 0, j)),
        pl.BlockSpec((None, None, k2, tn), lambda i, j: (layer, 1, 0, j)),
    ]
    out_specs = [pl.BlockSpec((tm, tn), lambda i, j: (i, j))]
    out_shapes = [jax.ShapeDtypeStruct((m, n), out_dtype)]
    operands = [a1, a2, w, w]
    cast_blocks = _with_side_cast(cast, grid, in_specs, out_specs, out_shapes, operands)
    out = pl.pallas_call(
        functools.partial(_mm2_kernel, cast_blocks=cast_blocks),
        grid=grid,
        in_specs=in_specs,
        out_specs=out_specs,
        out_shape=out_shapes,
        compiler_params=_params(("arbitrary", "arbitrary"), 56),
        name=name,
    )(*operands)
    return out[0], (out[1] if cast_blocks else None)


def _identity(x):
    return x


def _scale_q(x):
    return x * (HEAD_DIM ** -0.5)


_GELU_A = -2.0 * math.log2(math.e) * math.sqrt(2.0 / math.pi)
_GELU_B = _GELU_A * 0.044715


def _gelu_times(x, y):
    t = x * x * _GELU_B + _GELU_A
    return (x * y) / (1.0 + jnp.exp2(t * x))


def _gelu(x):
    t = x * x * _GELU_B + _GELU_A
    return x / (1.0 + jnp.exp2(t * x))


def _softmax_with_sink(s, sink):
    m = jnp.maximum(jnp.max(s, axis=-1, keepdims=True), sink)
    e = jnp.exp(s - m)
    den = jnp.sum(e, axis=-1, keepdims=True) + jnp.exp(sink - m)
    return e / den


def _attn_prompt_kernel(sink_ref, q_ref, kvc_ref, kvp_ref, os_ref, o_ref, *,
                        blocks_per_seq, n_prompt_blocks):
    @pl.when(pl.program_id(0) >= n_prompt_blocks)
    def _():
        o_ref[...] = os_ref[...]

    @pl.when(pl.program_id(0) < n_prompt_blocks)
    def _():
        _attn_prompt_block(sink_ref, q_ref, kvc_ref, kvp_ref, o_ref,
                           blocks_per_seq=blocks_per_seq)


def _attn_prompt_block(sink_ref, q_ref, kvc_ref, kvp_ref, o_ref, *, blocks_per_seq):
    n = pl.program_id(0) % blocks_per_seq
    w = WINDOW
    row = lax.broadcasted_iota(jnp.int32, (w, 2 * w), 0)
    col = lax.broadcasted_iota(jnp.int32, (w, 2 * w), 1)
    prev_ok = jnp.logical_and(jnp.logical_and(col < w, col > row), n > 0)
    own_ok = jnp.logical_and(col >= w, col - w <= row)
    valid = jnp.logical_or(prev_ok, own_ok)
    for kvh in range(N_KV_HEADS):
        ks = slice(kvh * HEAD_DIM, (kvh + 1) * HEAD_DIM)
        vs = slice(KV_COLS + kvh * HEAD_DIM, KV_COLS + (kvh + 1) * HEAD_DIM)
        k = jnp.concatenate([kvp_ref[:, ks], kvc_ref[:, ks]], axis=0).astype(BF16)
        v = jnp.concatenate([kvp_ref[:, vs], kvc_ref[:, vs]], axis=0).astype(BF16)
        v_ext = jnp.concatenate(
            [v, jnp.zeros((2 * w, LANES - HEAD_DIM), BF16), jnp.ones((2 * w, LANES), BF16)],
            axis=1,
        )
        heads = [kvh * GQA_GROUP + g for g in range(GQA_GROUP)]
        qs = jnp.concatenate(
            [q_ref[:, h * HEAD_DIM:(h + 1) * HEAD_DIM] for h in heads], axis=0
        )
        s = lax.dot_general(
            qs, k, (((1,), (1,)), ((), ())), preferred_element_type=F32
        )
        es, sink_terms = [], []
        for g, h in enumerate(heads):
            sg = jnp.where(valid, s[g * w:(g + 1) * w], -jnp.inf)
            m = jnp.maximum(jnp.max(sg, axis=-1, keepdims=True), sink_ref[h])
            es.append(jnp.exp(sg - m).astype(BF16))
            sink_terms.append(jnp.exp(sink_ref[h] - m))
        o = jnp.dot(jnp.concatenate(es, axis=0), v_ext, preferred_element_type=F32)
        for g, h in enumerate(heads):
            og = o[g * w:(g + 1) * w]
            out = og[:, :LANES] / (og[:, LANES:] + sink_terms[g])
            o_ref[:, h * HEAD_DIM:(h + 1) * HEAD_DIM] = out[:, :HEAD_DIM].astype(
                o_ref.dtype
            )


def attn_prompt(q, kv, sinks, att_sample, *, m_total, m_prompt, seq):
    w = WINDOW
    nblk = m_prompt // w
    last = nblk - 1
    return pl.pallas_call(
        functools.partial(_attn_prompt_kernel, blocks_per_seq=seq // w,
                          n_prompt_blocks=nblk),
        grid=(m_total // w,),
        in_specs=[
            pl.BlockSpec(memory_space=pltpu.SMEM),
            pl.BlockSpec((w, Q_COLS), lambda r: (jnp.minimum(r, last), 0)),
            pl.BlockSpec((w, 2 * KV_COLS), lambda r: (jnp.minimum(r, last), 0)),
            pl.BlockSpec((w, 2 * KV_COLS),
                         lambda r: (jnp.clip(r - 1, 0, last), 0)),
            pl.BlockSpec((w, Q_COLS), lambda r: (jnp.maximum(r - nblk, 0), 0)),
        ],
        out_specs=pl.BlockSpec((w, Q_COLS), lambda r: (r, 0)),
        out_shape=jax.ShapeDtypeStruct((m_total, Q_COLS), BF16),
        compiler_params=_params(("arbitrary",), 40),
        name="attn_prompt",
    )(sinks, q, kv, kv, att_sample)


def _attn_sample_kernel(sink_ref, q_ref, k_ref, v_ref, o_ref, *, t_new):
    nb, nkv, rows, d = q_ref.shape
    keys = k_ref.shape[2]
    q = q_ref[...].reshape(nb * nkv, rows, d)
    k = k_ref[...].reshape(nb * nkv, keys, d)
    v = v_ref[...].reshape(nb * nkv, keys, d)
    s = jnp.einsum("bqd,bkd->bqk", q, k, preferred_element_type=F32)
    t = lax.broadcasted_iota(jnp.int32, (rows, keys), 0) // GQA_GROUP
    col = lax.broadcasted_iota(jnp.int32, (rows, keys), 1)
    valid = jnp.logical_and(col > t, col <= WINDOW + t)
    s = jnp.where(valid[None], s, -jnp.inf)
    sink = jnp.broadcast_to(sink_ref[...][None], (nb, nkv, rows, 1)).reshape(
        nb * nkv, rows, 1
    )
    p = _softmax_with_sink(s, sink).astype(BF16)
    o = jnp.einsum("bqk,bkd->bqd", p, v, preferred_element_type=F32)
    o_ref[...] = o.reshape(nb, nkv, rows, d).astype(o_ref.dtype)


def attn_sample(q_s, k_new, v_new, k_buf, v_buf, sinks, *, nb):
    b, l = k_buf.shape[0], k_buf.shape[1]
    t_new = q_s.shape[0] // b
    rows = t_new * GQA_GROUP
    keys = 2 * WINDOW
    q4 = q_s.reshape(b, t_new, N_KV_HEADS, GQA_GROUP, HEAD_DIM)
    q4 = q4.transpose(0, 2, 1, 3, 4).reshape(b, N_KV_HEADS, rows, HEAD_DIM)

    def keys_layout(buf, new):
        cat = jnp.concatenate(
            [buf, new.reshape(b, t_new, N_KV_HEADS, HEAD_DIM)], axis=1
        )
        cat = jnp.pad(cat, ((0, 0), (0, keys - l - t_new), (0, 0), (0, 0)))
        return cat.transpose(0, 2, 1, 3).astype(BF16)

    k4 = keys_layout(k_buf, k_new)
    v4 = keys_layout(v_buf, v_new)
    sink_rows = jnp.broadcast_to(
        sinks.reshape(N_KV_HEADS, 1, GQA_GROUP), (N_KV_HEADS, t_new, GQA_GROUP)
    ).reshape(N_KV_HEADS, rows, 1)
    o4 = pl.pallas_call(
        functools.partial(_attn_sample_kernel, t_new=t_new),
        grid=(b // nb,),
        in_specs=[
            pl.BlockSpec((N_KV_HEADS, rows, 1), lambda i: (0, 0, 0)),
            pl.BlockSpec((nb, N_KV_HEADS, rows, HEAD_DIM), lambda i: (i, 0, 0, 0)),
            pl.BlockSpec((nb, N_KV_HEADS, keys, HEAD_DIM), lambda i: (i, 0, 0, 0)),
            pl.BlockSpec((nb, N_KV_HEADS, keys, HEAD_DIM), lambda i: (i, 0, 0, 0)),
        ],
        out_specs=pl.BlockSpec(
            (nb, N_KV_HEADS, rows, HEAD_DIM), lambda i: (i, 0, 0, 0)
        ),
        out_shape=jax.ShapeDtypeStruct((b, N_KV_HEADS, rows, HEAD_DIM), BF16),
        compiler_params=_params(("parallel",), 40),
        name="attn_sample",
    )(sink_rows, q4, k4, v4)
    o = o4.reshape(b, N_KV_HEADS, t_new, GQA_GROUP, HEAD_DIM).transpose(0, 2, 1, 3, 4)
    return o.reshape(b * t_new, Q_COLS)


def _place_rows_kernel(rows_ref, buf_ref, o_ref):
    del buf_ref
    o_ref[...] = rows_ref[...]


def place_rows(buf, rows, *, row_block):
    r, c = rows.shape
    return pl.pallas_call(
        _place_rows_kernel,
        grid=(1,),
        in_specs=[
            pl.BlockSpec((r, c), lambda i: (0, 0)),
            pl.BlockSpec(memory_space=pl.ANY),
        ],
        out_specs=pl.BlockSpec((r, c), lambda i: (row_block, 0)),
        out_shape=jax.ShapeDtypeStruct(buf.shape, buf.dtype),
        input_output_aliases={1: 0},
        compiler_params=_params(("arbitrary",), 16),
        name="place_rows",
    )(rows, buf)


def _gmlp_prompt_kernel(u_ref, v_ref, w_ref, bt_ref, gn_ref, os_ref, o_ref, *,
                        n_prompt_blocks):
    @pl.when(pl.program_id(0) >= n_prompt_blocks)
    def _():
        o_ref[...] = os_ref[...]

    @pl.when(pl.program_id(0) < n_prompt_blocks)
    def _():
        _gmlp_prompt_block(u_ref, v_ref, w_ref, bt_ref, gn_ref, o_ref)


def _gmlp_prompt_block(u_ref, v_ref, w_ref, bt_ref, gn_ref, o_ref):
    gv = _rms(v_ref[...].astype(F32), gn_ref[...])
    c = CHUNK
    row = lax.broadcasted_iota(jnp.int32, (c, c), 0)
    col = lax.broadcasted_iota(jnp.int32, (c, c), 1)
    causal = col <= row
    for h in range(GM_HEADS):
        cs = slice(h * GM_HEAD_DIM, (h + 1) * GM_HEAD_DIM)
        w = jnp.where(causal, w_ref[h], 0.0).astype(BF16)
        mixed = jnp.dot(w, gv[:, cs].astype(BF16), preferred_element_type=F32)
        mixed = mixed + bt_ref[:, h:h + 1]
        o_ref[:, cs] = (u_ref[:, cs].astype(F32) * mixed).astype(o_ref.dtype)


def gmlp_prompt(gm, w_s, b_t, g_norm, gated_sample, *, m_total, m_prompt):
    c = CHUNK
    nblk = m_prompt // c
    last = nblk - 1
    return pl.pallas_call(
        functools.partial(_gmlp_prompt_kernel, n_prompt_blocks=nblk),
        grid=(m_total // c,),
        in_specs=[
            pl.BlockSpec((c, GM_WIDTH), lambda r: (jnp.minimum(r, last), 0)),
            pl.BlockSpec((c, GM_WIDTH), lambda r: (jnp.minimum(r, last), 1)),
            pl.BlockSpec((GM_HEADS, c, c), lambda r: (0, 0, 0)),
            pl.BlockSpec((c, GM_HEADS), lambda r: (0, 0)),
            pl.BlockSpec((1, GM_WIDTH), lambda r: (0, 0)),
            pl.BlockSpec((c, GM_WIDTH), lambda r: (jnp.maximum(r - nblk, 0), 0)),
        ],
        out_specs=pl.BlockSpec((c, GM_WIDTH), lambda r: (r, 0)),
        out_shape=jax.ShapeDtypeStruct((m_total, GM_WIDTH), BF16),
        compiler_params=_params(("arbitrary",), 40),
        name="gmlp_prompt",
    )(gm, gm, w_s, b_t, g_norm, gated_sample)


def _gmlp_sample_kernel(u_ref, v_ref, wc_ref, bc_ref, gn_ref, o_ref, gv_ref, *, t_new):
    rows = u_ref.shape[0]
    nseq = rows // t_new
    gv = _rms(v_ref[...].astype(F32), gn_ref[...])
    gv_ref[...] = gv
    t = lax.broadcasted_iota(jnp.int32, (t_new, GM_HEAD_DIM), 0)
    for h in range(GM_HEADS):
        cs = slice(h * GM_HEAD_DIM, (h + 1) * GM_HEAD_DIM)
        x = gv[:, cs].reshape(nseq, t_new, GM_HEAD_DIM)
        acc = jnp.zeros((nseq, t_new, GM_HEAD_DIM), F32)
        for s in range(t_new):
            coef = jnp.where(t >= s, wc_ref[h, s], 0.0)
            acc = acc + coef[None] * x[:, s:s + 1, :]
        mixed = acc + bc_ref[h][None]
        u = u_ref[:, cs].astype(F32).reshape(nseq, t_new, GM_HEAD_DIM)
        o_ref[:, cs] = (u * mixed).reshape(rows, GM_HEAD_DIM).astype(o_ref.dtype)


def gmlp_sample(gm, w_s, b_s, g_norm, *, m_prompt, m_sample, t_new):
    lanes = GM_HEAD_DIM
    w8 = w_s[:, :t_new, :t_new]
    wc = jnp.broadcast_to(
        w8.transpose(0, 2, 1)[..., None], (GM_HEADS, t_new, t_new, lanes)
    )
    bc = jnp.broadcast_to(b_s[:, :t_new, None], (GM_HEADS, t_new, lanes))
    blk = m_prompt // m_sample
    return pl.pallas_call(
        functools.partial(_gmlp_sample_kernel, t_new=t_new),
        grid=(1,),
        in_specs=[
            pl.BlockSpec((m_sample, GM_WIDTH), lambda i: (blk, 0)),
            pl.BlockSpec((m_sample, GM_WIDTH), lambda i: (blk, 1)),
            pl.BlockSpec((GM_HEADS, t_new, t_new, lanes), lambda i: (0, 0, 0, 0)),
            pl.BlockSpec((GM_HEADS, t_new, lanes), lambda i: (0, 0, 0)),
            pl.BlockSpec((1, GM_WIDTH), lambda i: (0, 0)),
        ],
        out_specs=[
            pl.BlockSpec((m_sample, GM_WIDTH), lambda i: (0, 0)),
            pl.BlockSpec((m_sample, GM_WIDTH), lambda i: (0, 0)),
        ],
        out_shape=[
            jax.ShapeDtypeStruct((m_sample, GM_WIDTH), BF16),
            jax.ShapeDtypeStruct((m_sample, GM_WIDTH), F32),
        ],
        compiler_params=_params(("arbitrary",), 40),
        name="gmlp_sample",
    )(gm, gm, wc, bc, g_norm)


def _conv_gelu_gate(g, g1, g2, up, cw_ref, cb_ref):
    c = cb_ref[...] + cw_ref[0:1, :] * g2
    c = c + cw_ref[1:2, :] * g1
    c = c + cw_ref[2:3, :] * g
    return _gelu_times(c, up)


def _ffn_a_kernel(h_ref, hs_ref, wg_ref, wu_ref, cw_ref, cb_ref, st_ref, wd_ref, a_ref,
                  tail_ref, as_ref, gs_ref, wdb_ref, *, chunk, t_new, cast_blocks):
    i = pl.program_id(0)
    _side_cast(wd_ref, wdb_ref, cast_blocks)
    tm = h_ref.shape[0]
    tf = wg_ref.shape[1]
    wg = wg_ref[...].astype(BF16)
    wu = wu_ref[...].astype(BF16)

    prev8 = jnp.zeros((8, tf), F32)
    for r0 in range(0, tm, chunk):
        h = h_ref[r0:r0 + chunk, :]
        g = jnp.dot(h, wg, preferred_element_type=F32)
        up = jnp.dot(h, wu, preferred_element_type=F32)
        ext = jnp.concatenate([prev8, g], axis=0)
        g1 = pltpu.roll(ext, 1, 0)[8:]
        g2 = pltpu.roll(ext, 2, 0)[8:]
        a_ref[r0:r0 + chunk, :] = _conv_gelu_gate(g, g1, g2, up, cw_ref, cb_ref).astype(
            a_ref.dtype
        )
        prev8 = g[chunk - 8:, :]
    tail_ref[...] = prev8

    @pl.when(i == pl.num_programs(0) - 1)
    def _():
        hs = hs_ref[...]
        rows = hs.shape[0]
        g = jnp.dot(hs, wg, preferred_element_type=F32)
        up = jnp.dot(hs, wu, preferred_element_type=F32)
        gs_ref[...] = g
        t = lax.broadcasted_iota(jnp.int32, g.shape, 0) % t_new
        st2 = st_ref[...]
        st1 = pltpu.roll(st2, rows - 1, 0)
        g1 = jnp.where(t >= 1, pltpu.roll(g, 1, 0), st1)
        g2 = jnp.where(t >= 2, pltpu.roll(g, 2, 0), st2)
        as_ref[...] = _conv_gelu_gate(g, g1, g2, up, cw_ref, cb_ref).astype(as_ref.dtype)


def ffn_a(h, w_gate, w_up, w_down, conv_w, conv_b, conv_state, layer, *, m_prompt,
          m_sample, seq, t_new, tf, chunk, cast_rows):
    m_total, d = h.shape
    f = w_gate.shape[-1]
    nf = f // tf
    ni = m_prompt // seq
    assert t_new % 8 == 0 and t_new >= CONV_W - 1
    st = jnp.pad(conv_state, ((0, 0), (0, t_new - (CONV_W - 1)), (0, 0)))
    st = st.reshape(m_sample, f)
    sample_col = lambda i, j: (0, jnp.where(i == ni - 1, j, 0))
    wd_in, wd_out, wd_shape, cast_blocks = _side_cast_specs(
        w_down, layer, cast_rows, ni * nf, nf
    )
    return pl.pallas_call(
        functools.partial(_ffn_a_kernel, chunk=chunk, t_new=t_new,
                          cast_blocks=cast_blocks),
        grid=(ni, nf),
        in_specs=[
            pl.BlockSpec((seq, d), lambda i, j: (i, 0), pipeline_mode=pl.Buffered(1)),
            pl.BlockSpec((m_sample, d), lambda i, j: (m_prompt // m_sample, 0)),
            pl.BlockSpec((None, d, tf), lambda i, j: (layer, 0, j)),
            pl.BlockSpec((None, d, tf), lambda i, j: (layer, 0, j)),
            pl.BlockSpec((None, CONV_W, tf), lambda i, j: (layer, 0, j)),
            pl.BlockSpec((None, 1, tf), lambda i, j: (layer, 0, j)),
            pl.BlockSpec((m_sample, tf), lambda i, j: (0, j)),
            wd_in,
        ],
        out_specs=[
            pl.BlockSpec((seq, tf), lambda i, j: (i, j)),
            pl.BlockSpec((None, 8, tf), lambda i, j: (i, 0, j)),
            pl.BlockSpec((m_sample, tf), sample_col),
            pl.BlockSpec((m_sample, tf), sample_col),
            wd_out,
        ],
        out_shape=[
            jax.ShapeDtypeStruct((m_total, f), BF16),
            jax.ShapeDtypeStruct((ni, 8, f), F32),
            jax.ShapeDtypeStruct((m_sample, f), BF16),
            jax.ShapeDtypeStruct((m_sample, f), F32),
            wd_shape,
        ],
        compiler_params=_params(("arbitrary", "arbitrary"), 56),
        name="ffn_a",
    )(h, h, w_gate, w_up, conv_w, conv_b, st, w_down)


def kernel(x_prompt, x_sample, state_swa_k, state_swa_v, state_conv, w_in, w_out,
           attn_sinks, gm_spatial, gm_bias, gm_norm, norm_pre_mix, norm_post_mix,
           norm_pre_ffn, norm_post_ffn, w_ffn_gate, w_ffn_up, w_ffn_down, conv_w,
           conv_b):
    batch, seq, d = x_prompt.shape
    dec_batch, dec_seq, _ = x_sample.shape
    depth = w_in.shape[0]
    f = w_ffn_gate.shape[-1]
    mp = batch * seq
    ms = dec_batch * dec_seq
    m = mp + ms
    keep = min(WINDOW, seq)

    tm_mm = m // 8
    tm_down = m // 11
    tf = 256
    tm_rows = ms
    assert mp % tm_rows == 0

    w_in_b = w_in[0].astype(BF16)[None]
    conv_b3 = conv_b.reshape(depth, 1, f)
    cast_rows = 64

    x = (x_prompt.reshape(mp, d), x_sample.reshape(ms, d))
    h = rms_rows(x, norm_pre_mix[0].reshape(1, d), tm=tm_rows)

    p_k, p_v, p_c, s_k, s_v, s_c, s_g = [], [], [], [], [], [], []
    for l in range(depth):
        last = l + 1 == depth
        tn = 512
        q, _ = matmul_cols(h, w_in_b, 0, col_block0=0, n=Q_COLS, tm=tm_mm, tn=2 * tn,
                           out_dtype=BF16, epilogue=_scale_q, name="mm_q")
        kv, _ = matmul_cols(h, w_in_b, 0, col_block0=Q_COLS // tn, n=2 * KV_COLS,
                            tm=tm_mm, tn=tn, out_dtype=F32, epilogue=_identity,
                            name="mm_kv")
        gm, w_out_b = matmul_cols(
            h, w_in_b, 0, col_block0=(Q_COLS + 2 * KV_COLS) // tn, n=2 * GM_WIDTH,
            tm=tm_mm, tn=tn, out_dtype=BF16, epilogue=_gelu, name="mm_gm",
            cast=(w_out, l, cast_rows))
        w_out_b = w_out_b.reshape(1, 2, Q_COLS, d)

        kv_p = jnp.stack([kv[(b + 1) * seq - keep:(b + 1) * seq] for b in range(batch)])
        kv_s = kv[mp:]
        k_s, v_s = kv_s[:, :KV_COLS], kv_s[:, KV_COLS:]
        p_k.append(kv_p[..., :KV_COLS].reshape(batch, keep, N_KV_HEADS, HEAD_DIM))
        p_v.append(kv_p[..., KV_COLS:].reshape(batch, keep, N_KV_HEADS, HEAD_DIM))
        s_k.append(k_s.reshape(dec_batch, dec_seq, N_KV_HEADS, HEAD_DIM))
        s_v.append(v_s.reshape(dec_batch, dec_seq, N_KV_HEADS, HEAD_DIM))

        att_s = attn_sample(q[mp:], k_s, v_s, state_swa_k[l], state_swa_v[l],
                            attn_sinks[l], nb=8)
        att = attn_prompt(q, kv, attn_sinks[l], att_s, m_total=m, m_prompt=mp, seq=seq)

        gmo_s, gv_s = gmlp_sample(gm, gm_spatial[l], gm_bias[l], gm_norm[l].reshape(1, -1),
                                  m_prompt=mp, m_sample=ms, t_new=dec_seq)
        gmo = gmlp_prompt(gm, gm_spatial[l], gm_bias[l].T, gm_norm[l].reshape(1, -1),
                          gmo_s, m_total=m, m_prompt=mp)
        s_g.append(gv_s.reshape(dec_batch, dec_seq, GM_HEADS, GM_HEAD_DIM))

        mix, w_in_next = matmul_two(att, gmo, w_out_b, 0, tm=tm_mm, tn=1024,
                                    out_dtype=BF16, name="mm_out",
                                    cast=None if last else (w_in, l + 1, 2 * cast_rows))
        x, h = post_rows(mix, x, norm_post_mix[l].reshape(1, d),
                         norm_pre_ffn[l].reshape(1, d), tm=tm_rows)

        a, tail, a_s, g_s, w_down_b = ffn_a(
            h, w_ffn_gate, w_ffn_up, w_ffn_down, conv_w, conv_b3, state_conv[l], l,
            m_prompt=mp, m_sample=ms, seq=seq, t_new=dec_seq, tf=tf, chunk=512,
            cast_rows=2 * cast_rows)
        a = place_rows(a, a_s, row_block=mp // ms)
        p_c.append(tail[:, 8 - (CONV_W - 1):])
        s_c.append(g_s.reshape(dec_batch, dec_seq, f)[:, dec_seq - (CONV_W - 1):])

        ffn, _ = matmul_cols(a, w_down_b[None], 0, col_block0=0, n=d, tm=tm_down, tn=256,
                             out_dtype=BF16, epilogue=_identity, name="mm_down")
        x, h = post_rows(ffn, x, norm_post_ffn[l].reshape(1, d),
                         None if last else norm_pre_mix[l + 1].reshape(1, d),
                         tm=tm_rows, split_out=(mp, ms) if last else None)
        if not last:
            w_in_b = w_in_next[None]

    x_p, x_s = x
    return (x_p.reshape(batch, seq, d), x_s.reshape(dec_batch, dec_seq, d),
            jnp.stack(p_k), jnp.stack(p_v), jnp.stack(p_c),
            jnp.stack(s_k), jnp.stack(s_v), jnp.stack(s_c), jnp.stack(s_g))
```

```python
import functools
import math

import jax
import jax.numpy as jnp
from jax import lax
from jax.experimental import pallas as pl
from jax.experimental.pallas import tpu as pltpu

F32 = jnp.float32
BF16 = jnp.bfloat16

D_MODEL = 4096
HEAD_DIM = 64
N_HEADS = 32
N_KV_HEADS = 4
GQA_GROUP = 8
WINDOW = 128
CHUNK = 128
GM_HEAD_DIM = 128
GM_HEADS = 16
GM_WIDTH = 2048
Q_COLS = 2048
KV_COLS = 256
CONV_W = 3
EPS = 1e-6
LANES = 128
LOG2_E = math.log2(math.e)

MIB = 1024 * 1024


def _params(semantics, vmem_mib):
    return pltpu.CompilerParams(
        dimension_semantics=semantics, vmem_limit_bytes=vmem_mib * MIB
    )


def _rms(x, g):
    ms = jnp.mean(x * x, axis=-1, keepdims=True)
    return x * lax.rsqrt(ms + EPS) * g


def _split_specs(tm, d, n_prompt_tiles):
    return [
        pl.BlockSpec((tm, d), lambda i: (jnp.minimum(i, n_prompt_tiles - 1), 0)),
        pl.BlockSpec((tm, d), lambda i: (0, 0)),
    ]


def _load_rows(refs, n_prompt_tiles):
    if len(refs) == 1:
        return refs[0][...]
    is_prompt = pl.program_id(0) < n_prompt_tiles
    return jnp.where(is_prompt, refs[0][...], refs[1][...])


def _store_rows(refs, value, n_prompt_tiles):
    if len(refs) == 1:
        refs[0][...] = value
        return

    @pl.when(pl.program_id(0) < n_prompt_tiles)
    def _():
        refs[0][...] = value

    @pl.when(pl.program_id(0) >= n_prompt_tiles)
    def _():
        refs[1][...] = value


def _rms_kernel(*refs, n_in, n_prompt_tiles):
    x = _load_rows(refs[:n_in], n_prompt_tiles)
    g_ref, o_ref = refs[n_in:]
    o_ref[...] = _rms(x, g_ref[...]).astype(o_ref.dtype)


def rms_rows(x_parts, g, *, tm):
    d = x_parts[0].shape[1]
    m = sum(p.shape[0] for p in x_parts)
    npt = x_parts[0].shape[0] // tm
    row = pl.BlockSpec((tm, d), lambda i: (i, 0))
    x_specs = [row] if len(x_parts) == 1 else _split_specs(tm, d, npt)
    return pl.pallas_call(
        functools.partial(_rms_kernel, n_in=len(x_parts), n_prompt_tiles=npt),
        grid=(m // tm,),
        in_specs=x_specs + [pl.BlockSpec((1, d), lambda i: (0, 0))],
        out_specs=row,
        out_shape=jax.ShapeDtypeStruct((m, d), BF16),
        compiler_params=_params(("arbitrary",), 40),
        name="rms_rows",
    )(*x_parts, g)


def _post_kernel(*refs, n_in, n_out, has_next, n_prompt_tiles):
    m_ref = refs[0]
    x = _load_rows(refs[1:1 + n_in], n_prompt_tiles)
    rest = refs[1 + n_in:]
    xn = x + _rms(m_ref[...].astype(F32), rest[0][...])
    rest = rest[1:]
    if has_next:
        gnext_ref, rest = rest[0], rest[1:]
    _store_rows(rest[:n_out], xn, n_prompt_tiles)
    if has_next:
        ho_ref = rest[n_out]
        ho_ref[...] = _rms(xn, gnext_ref[...]).astype(ho_ref.dtype)


def post_rows(m_in, x_parts, g_post, g_next, *, tm, split_out=None):
    m, d = m_in.shape
    npt = (x_parts[0].shape[0] if split_out is None else split_out[0]) // tm
    if len(x_parts) == 1 and split_out is None:
        npt = m // tm
    row = pl.BlockSpec((tm, d), lambda i: (i, 0))
    vec = pl.BlockSpec((1, d), lambda i: (0, 0))
    x_specs = [row] if len(x_parts) == 1 else _split_specs(tm, d, npt)
    if split_out is None:
        xo_specs = [row]
        xo_shapes = [jax.ShapeDtypeStruct((m, d), F32)]
    else:
        xo_specs = _split_specs(tm, d, npt)
        xo_shapes = [jax.ShapeDtypeStruct((r, d), F32) for r in split_out]
    has_next = g_next is not None
    gains = [g_post, g_next] if has_next else [g_post]
    out = pl.pallas_call(
        functools.partial(_post_kernel, n_in=len(x_parts), n_out=len(xo_specs),
                          has_next=has_next, n_prompt_tiles=npt),
        grid=(m // tm,),
        in_specs=[row] + x_specs + [vec] * len(gains),
        out_specs=xo_specs + ([row] if has_next else []),
        out_shape=xo_shapes
        + ([jax.ShapeDtypeStruct((m, d), BF16)] if has_next else []),
        compiler_params=_params(("arbitrary",), 48),
        name="post_rows",
    )(m_in, *x_parts, *gains)
    x_new = out[:len(xo_specs)]
    return (x_new, out[-1]) if has_next else (x_new, None)


def _grid_step():
    return pl.program_id(0) * pl.num_programs(1) + pl.program_id(1)


def _side_cast(src_ref, dst_ref, n_blocks):
    @pl.when(_grid_step() < n_blocks)
    def _():
        dst_ref[...] = src_ref[...].astype(dst_ref.dtype)


def _side_cast_specs(src, layer, block_rows, n_steps, n_cols_grid):
    _, rows, cols = src.shape
    n_blocks = rows // block_rows
    assert rows % block_rows == 0 and n_blocks <= n_steps

    def block(i, j):
        return jnp.minimum(i * n_cols_grid + j, n_blocks - 1)

    in_spec = pl.BlockSpec((None, block_rows, cols), lambda i, j: (layer, block(i, j), 0))
    out_spec = pl.BlockSpec((block_rows, cols), lambda i, j: (block(i, j), 0))
    return in_spec, out_spec, jax.ShapeDtypeStruct((rows, cols), BF16), n_blocks


def _mm_kernel(a_ref, b_ref, *rest, epilogue, cast_blocks):
    o_ref = rest[-2] if cast_blocks else rest[-1]
    acc = jnp.dot(a_ref[...], b_ref[...], preferred_element_type=F32)
    o_ref[...] = epilogue(acc).astype(o_ref.dtype)
    if cast_blocks:
        _side_cast(rest[0], rest[-1], cast_blocks)


def _mm2_kernel(a1_ref, a2_ref, b1_ref, b2_ref, *rest, cast_blocks):
    o_ref = rest[-2] if cast_blocks else rest[-1]
    acc = jnp.dot(a1_ref[...], b1_ref[...], preferred_element_type=F32)
    acc = acc + jnp.dot(a2_ref[...], b2_ref[...], preferred_element_type=F32)
    o_ref[...] = acc.astype(o_ref.dtype)
    if cast_blocks:
        _side_cast(rest[0], rest[-1], cast_blocks)


def _with_side_cast(cast, grid, in_specs, out_specs, out_shapes, operands):
    if cast is None:
        return 0
    src, layer, block_rows = cast
    i_spec, o_spec, o_shape, n_blocks = _side_cast_specs(
        src, layer, block_rows, grid[0] * grid[1], grid[1]
    )
    in_specs.append(i_spec)
    out_specs.append(o_spec)
    out_shapes.append(o_shape)
    operands.append(src)
    return n_blocks


def matmul_cols(a, w, layer, *, col_block0, n, tm, tn, out_dtype, epilogue, name,
                cast=None):
    m, k = a.shape
    grid = (m // tm, n // tn)
    in_specs = [
        pl.BlockSpec((tm, k), lambda i, j: (i, 0)),
        pl.BlockSpec((None, k, tn), lambda i, j: (layer, 0, j + col_block0)),
    ]
    out_specs = [pl.BlockSpec((tm, tn), lambda i, j: (i, j))]
    out_shapes = [jax.ShapeDtypeStruct((m, n), out_dtype)]
    operands = [a, w]
    cast_blocks = _with_side_cast(cast, grid, in_specs, out_specs, out_shapes, operands)
    out = pl.pallas_call(
        functools.partial(_mm_kernel, epilogue=epilogue, cast_blocks=cast_blocks),
        grid=grid,
        in_specs=in_specs,
        out_specs=out_specs,
        out_shape=out_shapes,
        compiler_params=_params(("arbitrary", "arbitrary"), 56),
        name=name,
    )(*operands)
    return out[0], (out[1] if cast_blocks else None)


def matmul_two(a1, a2, w, layer, *, tm, tn, out_dtype, name, cast=None):
    m, k1 = a1.shape
    k2 = a2.shape[1]
    assert k1 == k2
    n = w.shape[-1]
    grid = (m // tm, n // tn)
    in_specs = [
        pl.BlockSpec((tm, k1), lambda i, j: (i, 0)),
        pl.BlockSpec((tm, k2), lambda i, j: (i, 0)),
        pl.BlockSpec((None, None, k1, tn), lambda i, j: (layer, 0, 0, j)),
        pl.BlockSpec((None, None, k2, tn), lambda i, j: (layer, 1, 0, j)),
    ]
    out_specs = [pl.BlockSpec((tm, tn), lambda i, j: (i, j))]
    out_shapes = [jax.ShapeDtypeStruct((m, n), out_dtype)]
    operands = [a1, a2, w, w]
    cast_blocks = _with_side_cast(cast, grid, in_specs, out_specs, out_shapes, operands)
    out = pl.pallas_call(
        functools.partial(_mm2_kernel, cast_blocks=cast_blocks),
        grid=grid,
        in_specs=in_specs,
        out_specs=out_specs,
        out_shape=out_shapes,
        compiler_params=_params(("arbitrary", "arbitrary"), 56),
        name=name,
    )(*operands)
    return out[0], (out[1] if cast_blocks else None)


def _identity(x):
    return x


def _scale_q(x):
    return x * (HEAD_DIM ** -0.5 * LOG2_E)


_GELU_A = -2.0 * LOG2_E * math.sqrt(2.0 / math.pi)
_GELU_B = _GELU_A * 0.044715


def _gelu_times(x, y):
    t = x * x * _GELU_B + _GELU_A
    return (x * y) / (1.0 + jnp.exp2(t * x))


def _gelu(x):
    t = x * x * _GELU_B + _GELU_A
    return x / (1.0 + jnp.exp2(t * x))


def _softmax2_with_sink(s, sink):
    m = jnp.maximum(jnp.max(s, axis=-1, keepdims=True), sink)
    e = jnp.exp2(s - m)
    den = jnp.sum(e, axis=-1, keepdims=True) + jnp.exp2(sink - m)
    return e / den


def _attn_prompt_kernel(sink_ref, q_ref, kvc_ref, kvp_ref, os_ref, o_ref, *,
                        blocks_per_seq, n_prompt_steps, per_step):
    step = pl.program_id(0)
    w = WINDOW

    @pl.when(step >= n_prompt_steps)
    def _():
        o_ref[...] = os_ref[...]

    @pl.when(step < n_prompt_steps)
    def _():
        for b in range(per_step):
            rows = pl.ds(b * w, w)
            prev_ref = kvp_ref if b == 0 else kvc_ref.at[pl.ds((b - 1) * w, w)]
            _attn_prompt_block(sink_ref, q_ref.at[rows], kvc_ref.at[rows], prev_ref,
                               o_ref.at[rows], (step * per_step + b) % blocks_per_seq)


def _attn_prompt_block(sink_ref, q_ref, kvc_ref, kvp_ref, o_ref, n):
    w = WINDOW
    row = lax.broadcasted_iota(jnp.int32, (w, 2 * w), 0)
    col = lax.broadcasted_iota(jnp.int32, (w, 2 * w), 1)
    prev_ok = jnp.logical_and(jnp.logical_and(col < w, col > row), n > 0)
    own_ok = jnp.logical_and(col >= w, col - w <= row)
    valid = jnp.logical_or(prev_ok, own_ok)
    for kvh in range(N_KV_HEADS):
        ks = slice(kvh * HEAD_DIM, (kvh + 1) * HEAD_DIM)
        vs = slice(KV_COLS + kvh * HEAD_DIM, KV_COLS + (kvh + 1) * HEAD_DIM)
        k = jnp.concatenate([kvp_ref[:, ks], kvc_ref[:, ks]], axis=0).astype(BF16)
        v = jnp.concatenate([kvp_ref[:, vs], kvc_ref[:, vs]], axis=0).astype(BF16)
        v_ext = jnp.concatenate(
            [v, jnp.zeros((2 * w, LANES - HEAD_DIM), BF16), jnp.ones((2 * w, LANES), BF16)],
            axis=1,
        )
        heads = [kvh * GQA_GROUP + g for g in range(GQA_GROUP)]
        qs = jnp.concatenate(
            [q_ref[:, h * HEAD_DIM:(h + 1) * HEAD_DIM] for h in heads], axis=0
        )
        s = lax.dot_general(
            qs, k, (((1,), (1,)), ((), ())), preferred_element_type=F32
        )
        es, sink_terms = [], []
        for g, h in enumerate(heads):
            sink = sink_ref[h] * LOG2_E
            sg = jnp.where(valid, s[g * w:(g + 1) * w], -jnp.inf)
            m = jnp.maximum(jnp.max(sg, axis=-1, keepdims=True), sink)
            es.append(jnp.exp2(sg - m).astype(BF16))
            sink_terms.append(jnp.exp2(sink - m))
        o = jnp.dot(jnp.concatenate(es, axis=0), v_ext, preferred_element_type=F32)
        for g, h in enumerate(heads):
            og = o[g * w:(g + 1) * w]
            out = og[:, :LANES] / (og[:, LANES:] + sink_terms[g])
            o_ref[:, h * HEAD_DIM:(h + 1) * HEAD_DIM] = out[:, :HEAD_DIM].astype(
                o_ref.dtype
            )


def attn_prompt(q, kv, sinks, att_sample, *, m_total, m_prompt, seq, per_step):
    w = WINDOW
    rows = per_step * w
    assert m_prompt % rows == 0 and att_sample.shape[0] % rows == 0
    nsteps = m_prompt // rows
    last = nsteps - 1
    last_prev = m_prompt // w - 1
    return pl.pallas_call(
        functools.partial(_attn_prompt_kernel, blocks_per_seq=seq // w,
                          n_prompt_steps=nsteps, per_step=per_step),
        grid=(m_total // rows,),
        in_specs=[
            pl.BlockSpec(memory_space=pltpu.SMEM),
            pl.BlockSpec((rows, Q_COLS), lambda r: (jnp.minimum(r, last), 0)),
            pl.BlockSpec((rows, 2 * KV_COLS), lambda r: (jnp.minimum(r, last), 0)),
            pl.BlockSpec((w, 2 * KV_COLS),
                         lambda r: (jnp.clip(per_step * r - 1, 0, last_prev), 0)),
            pl.BlockSpec((rows, Q_COLS), lambda r: (jnp.maximum(r - nsteps, 0), 0)),
        ],
        out_specs=pl.BlockSpec((rows, Q_COLS), lambda r: (r, 0)),
        out_shape=jax.ShapeDtypeStruct((m_total, Q_COLS), BF16),
        compiler_params=_params(("arbitrary",), 40),
        name="attn_prompt",
    )(sinks, q, kv, kv, att_sample)


def _attn_sample_kernel(sink_ref, q_ref, k_ref, v_ref, o_ref, *, t_new):
    nb, nkv, rows, d = q_ref.shape
    keys = k_ref.shape[2]
    q = q_ref[...].reshape(nb * nkv, rows, d)
    k = k_ref[...].reshape(nb * nkv, keys, d)
    v = v_ref[...].reshape(nb * nkv, keys, d)
    s = jnp.einsum("bqd,bkd->bqk", q, k, preferred_element_type=F32)
    t = lax.broadcasted_iota(jnp.int32, (rows, keys), 0) // GQA_GROUP
    col = lax.broadcasted_iota(jnp.int32, (rows, keys), 1)
    valid = jnp.logical_and(col > t, col <= WINDOW + t)
    s = jnp.where(valid[None], s, -jnp.inf)
    sink = jnp.broadcast_to(sink_ref[...][None], (nb, nkv, rows, 1)).reshape(
        nb * nkv, rows, 1
    )
    p = _softmax2_with_sink(s, sink * LOG2_E).astype(BF16)
    o = jnp.einsum("bqk,bkd->bqd", p, v, preferred_element_type=F32)
    o_ref[...] = o.reshape(nb, nkv, rows, d).astype(o_ref.dtype)


def attn_sample(q_s, k_new, v_new, k_buf, v_buf, sinks, *, nb):
    b, l = k_buf.shape[0], k_buf.shape[1]
    t_new = q_s.shape[0] // b
    rows = t_new * GQA_GROUP
    keys = 2 * WINDOW
    q4 = q_s.reshape(b, t_new, N_KV_HEADS, GQA_GROUP, HEAD_DIM)
    q4 = q4.transpose(0, 2, 1, 3, 4).reshape(b, N_KV_HEADS, rows, HEAD_DIM)

    def keys_layout(buf, new):
        cat = jnp.concatenate(
            [buf, new.reshape(b, t_new, N_KV_HEADS, HEAD_DIM)], axis=1
        )
        cat = jnp.pad(cat, ((0, 0), (0, keys - l - t_new), (0, 0), (0, 0)))
        return cat.transpose(0, 2, 1, 3).astype(BF16)

    k4 = keys_layout(k_buf, k_new)
    v4 = keys_layout(v_buf, v_new)
    sink_rows = jnp.broadcast_to(
        sinks.reshape(N_KV_HEADS, 1, GQA_GROUP), (N_KV_HEADS, t_new, GQA_GROUP)
    ).reshape(N_KV_HEADS, rows, 1)
    o4 = pl.pallas_call(
        functools.partial(_attn_sample_kernel, t_new=t_new),
        grid=(b // nb,),
        in_specs=[
            pl.BlockSpec((N_KV_HEADS, rows, 1), lambda i: (0, 0, 0)),
            pl.BlockSpec((nb, N_KV_HEADS, rows, HEAD_DIM), lambda i: (i, 0, 0, 0)),
            pl.BlockSpec((nb, N_KV_HEADS, keys, HEAD_DIM), lambda i: (i, 0, 0, 0)),
            pl.BlockSpec((nb, N_KV_HEADS, keys, HEAD_DIM), lambda i: (i, 0, 0, 0)),
        ],
        out_specs=pl.BlockSpec(
            (nb, N_KV_HEADS, rows, HEAD_DIM), lambda i: (i, 0, 0, 0)
        ),
        out_shape=jax.ShapeDtypeStruct((b, N_KV_HEADS, rows, HEAD_DIM), BF16),
        compiler_params=_params(("parallel",), 40),
        name="attn_sample",
    )(sink_rows, q4, k4, v4)
    o = o4.reshape(b, N_KV_HEADS, t_new, GQA_GROUP, HEAD_DIM).transpose(0, 2, 1, 3, 4)
    return o.reshape(b * t_new, Q_COLS)


def _place_rows_kernel(rows_ref, buf_ref, o_ref):
    del buf_ref
    o_ref[...] = rows_ref[...]


def place_rows(buf, rows, *, row_block):
    r, c = rows.shape
    return pl.pallas_call(
        _place_rows_kernel,
        grid=(1,),
        in_specs=[
            pl.BlockSpec((r, c), lambda i: (0, 0)),
            pl.BlockSpec(memory_space=pl.ANY),
        ],
        out_specs=pl.BlockSpec((r, c), lambda i: (row_block, 0)),
        out_shape=jax.ShapeDtypeStruct(buf.shape, buf.dtype),
        input_output_aliases={1: 0},
        compiler_params=_params(("arbitrary",), 16),
        name="place_rows",
    )(rows, buf)


def _gmlp_prompt_kernel(u_ref, v_ref, w_ref, bt_ref, gn_ref, os_ref, o_ref, *,
                        n_prompt_steps, per_step):
    @pl.when(pl.program_id(0) >= n_prompt_steps)
    def _():
        o_ref[...] = os_ref[...]

    @pl.when(pl.program_id(0) < n_prompt_steps)
    def _():
        for b in range(per_step):
            rows = pl.ds(b * CHUNK, CHUNK)
            _gmlp_prompt_chunk(u_ref.at[rows], v_ref.at[rows], w_ref, bt_ref, gn_ref,
                               o_ref.at[rows])


def _gmlp_prompt_chunk(u_ref, v_ref, w_ref, bt_ref, gn_ref, o_ref):
    gv = _rms(v_ref[...].astype(F32), gn_ref[...])
    c = CHUNK
    row = lax.broadcasted_iota(jnp.int32, (c, c), 0)
    col = lax.broadcasted_iota(jnp.int32, (c, c), 1)
    causal = col <= row
    for h in range(GM_HEADS):
        cs = slice(h * GM_HEAD_DIM, (h + 1) * GM_HEAD_DIM)
        w = jnp.where(causal, w_ref[h], 0.0).astype(BF16)
        mixed = jnp.dot(w, gv[:, cs].astype(BF16), preferred_element_type=F32)
        mixed = mixed + bt_ref[:, h:h + 1]
        o_ref[:, cs] = (u_ref[:, cs].astype(F32) * mixed).astype(o_ref.dtype)


def gmlp_prompt(gm, w_s, b_t, g_norm, gated_sample, *, m_total, m_prompt, per_step):
    c = CHUNK
    rows = per_step * c
    assert m_prompt % rows == 0 and gated_sample.shape[0] % rows == 0
    nsteps = m_prompt // rows
    last = nsteps - 1
    return pl.pallas_call(
        functools.partial(_gmlp_prompt_kernel, n_prompt_steps=nsteps, per_step=per_step),
        grid=(m_total // rows,),
        in_specs=[
            pl.BlockSpec((rows, GM_WIDTH), lambda r: (jnp.minimum(r, last), 0)),
            pl.BlockSpec((rows, GM_WIDTH), lambda r: (jnp.minimum(r, last), 1)),
            pl.BlockSpec((GM_HEADS, c, c), lambda r: (0, 0, 0)),
            pl.BlockSpec((c, GM_HEADS), lambda r: (0, 0)),
            pl.BlockSpec((1, GM_WIDTH), lambda r: (0, 0)),
            pl.BlockSpec((rows, GM_WIDTH), lambda r: (jnp.maximum(r - nsteps, 0), 0)),
        ],
        out_specs=pl.BlockSpec((rows, GM_WIDTH), lambda r: (r, 0)),
        out_shape=jax.ShapeDtypeStruct((m_total, GM_WIDTH), BF16),
        compiler_params=_params(("arbitrary",), 40),
        name="gmlp_prompt",
    )(gm, gm, w_s, b_t, g_norm, gated_sample)


def _gmlp_sample_kernel(u_ref, v_ref, wc_ref, bc_ref, gn_ref, o_ref, gv_ref, *, t_new):
    rows = u_ref.shape[0]
    nseq = rows // t_new
    gv = _rms(v_ref[...].astype(F32), gn_ref[...])
    gv_ref[...] = gv
    t = lax.broadcasted_iota(jnp.int32, (t_new, GM_HEAD_DIM), 0)
    for h in range(GM_HEADS):
        cs = slice(h * GM_HEAD_DIM, (h + 1) * GM_HEAD_DIM)
        x = gv[:, cs].reshape(nseq, t_new, GM_HEAD_DIM)
        acc = jnp.zeros((nseq, t_new, GM_HEAD_DIM), F32)
        for s in range(t_new):
            coef = jnp.where(t >= s, wc_ref[h, s], 0.0)
            acc = acc + coef[None] * x[:, s:s + 1, :]
        mixed = acc + bc_ref[h][None]
        u = u_ref[:, cs].astype(F32).reshape(nseq, t_new, GM_HEAD_DIM)
        o_ref[:, cs] = (u * mixed).reshape(rows, GM_HEAD_DIM).astype(o_ref.dtype)


def gmlp_sample(gm, w_s, b_s, g_norm, *, m_prompt, m_sample, t_new):
    lanes = GM_HEAD_DIM
    w8 = w_s[:, :t_new, :t_new]
    wc = jnp.broadcast_to(
        w8.transpose(0, 2, 1)[..., None], (GM_HEADS, t_new, t_new, lanes)
    )
    bc = jnp.broadcast_to(b_s[:, :t_new, None], (GM_HEADS, t_new, lanes))
    blk = m_prompt // m_sample
    return pl.pallas_call(
        functools.partial(_gmlp_sample_kernel, t_new=t_new),
        grid=(1,),
        in_specs=[
            pl.BlockSpec((m_sample, GM_WIDTH), lambda i: (blk, 0)),
            pl.BlockSpec((m_sample, GM_WIDTH), lambda i: (blk, 1)),
            pl.BlockSpec((GM_HEADS, t_new, t_new, lanes), lambda i: (0, 0, 0, 0)),
            pl.BlockSpec((GM_HEADS, t_new, lanes), lambda i: (0, 0, 0)),
            pl.BlockSpec((1, GM_WIDTH), lambda i: (0, 0)),
        ],
        out_specs=[
            pl.BlockSpec((m_sample, GM_WIDTH), lambda i: (0, 0)),
            pl.BlockSpec((m_sample, GM_WIDTH), lambda i: (0, 0)),
        ],
        out_shape=[
            jax.ShapeDtypeStruct((m_sample, GM_WIDTH), BF16),
            jax.ShapeDtypeStruct((m_sample, GM_WIDTH), F32),
        ],
        compiler_params=_params(("arbitrary",), 40),
        name="gmlp_sample",
    )(gm, gm, wc, bc, g_norm)


def _conv_gelu_gate(g, g1, g2, up, cw_ref, cb_ref):
    c = cb_ref[...] + cw_ref[0:1, :] * g2
    c = c + cw_ref[1:2, :] * g1
    c = c + cw_ref[2:3, :] * g
    return _gelu_times(c, up)


def _ffn_a_kernel(h_ref, hs_ref, wg_ref, wu_ref, cw_ref, cb_ref, st_ref, wd_ref, a_ref,
                  tail_ref, as_ref, gs_ref, wdb_ref, *, chunk, t_new, cast_blocks):
    i = pl.program_id(0)
    _side_cast(wd_ref, wdb_ref, cast_blocks)
    tm = h_ref.shape[0]
    tf = wg_ref.shape[1]
    wg = wg_ref[...].astype(BF16)
    wu = wu_ref[...].astype(BF16)

    prev8 = jnp.zeros((8, tf), F32)
    for r0 in range(0, tm, chunk):
        h = h_ref[r0:r0 + chunk, :]
        g = jnp.dot(h, wg, preferred_element_type=F32)
        up = jnp.dot(h, wu, preferred_element_type=F32)
        ext = jnp.concatenate([prev8, g], axis=0)
        g1 = pltpu.roll(ext, 1, 0)[8:]
        g2 = pltpu.roll(ext, 2, 0)[8:]
        a_ref[r0:r0 + chunk, :] = _conv_gelu_gate(g, g1, g2, up, cw_ref, cb_ref).astype(
            a_ref.dtype
        )
        prev8 = g[chunk - 8:, :]
    tail_ref[...] = prev8

    @pl.when(i == pl.num_programs(0) - 1)
    def _():
        hs = hs_ref[...]
        rows = hs.shape[0]
        g = jnp.dot(hs, wg, preferred_element_type=F32)
        up = jnp.dot(hs, wu, preferred_element_type=F32)
        gs_ref[...] = g
        t = lax.broadcasted_iota(jnp.int32, g.shape, 0) % t_new
        st2 = st_ref[...]
        st1 = pltpu.roll(st2, rows - 1, 0)
        g1 = jnp.where(t >= 1, pltpu.roll(g, 1, 0), st1)
        g2 = jnp.where(t >= 2, pltpu.roll(g, 2, 0), st2)
        as_ref[...] = _conv_gelu_gate(g, g1, g2, up, cw_ref, cb_ref).astype(as_ref.dtype)


def ffn_a(h, w_gate, w_up, w_down, conv_w, conv_b, conv_state, layer, *, m_prompt,
          m_sample, seq, t_new, tf, chunk, cast_rows):
    m_total, d = h.shape
    f = w_gate.shape[-1]
    nf = f // tf
    ni = m_prompt // seq
    assert t_new % 8 == 0 and t_new >= CONV_W - 1
    st = jnp.pad(conv_state, ((0, 0), (0, t_new - (CONV_W - 1)), (0, 0)))
    st = st.reshape(m_sample, f)
    sample_col = lambda i, j: (0, jnp.where(i == ni - 1, j, 0))
    wd_in, wd_out, wd_shape, cast_blocks = _side_cast_specs(
        w_down, layer, cast_rows, ni * nf, nf
    )
    return pl.pallas_call(
        functools.partial(_ffn_a_kernel, chunk=chunk, t_new=t_new,
                          cast_blocks=cast_blocks),
        grid=(ni, nf),
        in_specs=[
            pl.BlockSpec((seq, d), lambda i, j: (i, 0), pipeline_mode=pl.Buffered(1)),
            pl.BlockSpec((m_sample, d), lambda i, j: (m_prompt // m_sample, 0)),
            pl.BlockSpec((None, d, tf), lambda i, j: (layer, 0, j)),
            pl.BlockSpec((None, d, tf), lambda i, j: (layer, 0, j)),
            pl.BlockSpec((None, CONV_W, tf), lambda i, j: (layer, 0, j)),
            pl.BlockSpec((None, 1, tf), lambda i, j: (layer, 0, j)),
            pl.BlockSpec((m_sample, tf), lambda i, j: (0, j)),
            wd_in,
        ],
        out_specs=[
            pl.BlockSpec((seq, tf), lambda i, j: (i, j)),
            pl.BlockSpec((None, 8, tf), lambda i, j: (i, 0, j)),
            pl.BlockSpec((m_sample, tf), sample_col),
            pl.BlockSpec((m_sample, tf), sample_col),
            wd_out,
        ],
        out_shape=[
            jax.ShapeDtypeStruct((m_total, f), BF16),
            jax.ShapeDtypeStruct((ni, 8, f), F32),
            jax.ShapeDtypeStruct((m_sample, f), BF16),
            jax.ShapeDtypeStruct((m_sample, f), F32),
            wd_shape,
        ],
        compiler_params=_params(("arbitrary", "arbitrary"), 56),
        name="ffn_a",
    )(h, h, w_gate, w_up, conv_w, conv_b, st, w_down)


def kernel(x_prompt, x_sample, state_swa_k, state_swa_v, state_conv, w_in, w_out,
           attn_sinks, gm_spatial, gm_bias, gm_norm, norm_pre_mix, norm_post_mix,
           norm_pre_ffn, norm_post_ffn, w_ffn_gate, w_ffn_up, w_ffn_down, conv_w,
           conv_b):
    batch, seq, d = x_prompt.shape
    dec_batch, dec_seq, _ = x_sample.shape
    depth = w_in.shape[0]
    f = w_ffn_gate.shape[-1]
    mp = batch * seq
    ms = dec_batch * dec_seq
    m = mp + ms
    keep = min(WINDOW, seq)

    tm_mm = m // 8
    tm_down = m // 11
    tf = 256
    tm_rows = ms
    assert mp % tm_rows == 0

    w_in_b = w_in[0].astype(BF16)[None]
    conv_b3 = conv_b.reshape(depth, 1, f)
    cast_rows = 64

    x = (x_prompt.reshape(mp, d), x_sample.reshape(ms, d))
    h = rms_rows(x, norm_pre_mix[0].reshape(1, d), tm=tm_rows)

    p_k, p_v, p_c, s_k, s_v, s_c, s_g = [], [], [], [], [], [], []
    for l in range(depth):
        last = l + 1 == depth
        tn = 512
        q, _ = matmul_cols(h, w_in_b, 0, col_block0=0, n=Q_COLS, tm=tm_mm, tn=2 * tn,
                           out_dtype=BF16, epilogue=_scale_q, name="mm_q")
        kv, _ = matmul_cols(h, w_in_b, 0, col_block0=Q_COLS // tn, n=2 * KV_COLS,
                            tm=tm_mm, tn=tn, out_dtype=F32, epilogue=_identity,
                            name="mm_kv")
        gm, w_out_b = matmul_cols(
            h, w_in_b, 0, col_block0=(Q_COLS + 2 * KV_COLS) // tn, n=2 * GM_WIDTH,
            tm=tm_mm, tn=tn, out_dtype=BF16, epilogue=_gelu, name="mm_gm",
            cast=(w_out, l, cast_rows))
        w_out_b = w_out_b.reshape(1, 2, Q_COLS, d)

        kv_p = jnp.stack([kv[(b + 1) * seq - keep:(b + 1) * seq] for b in range(batch)])
        kv_s = kv[mp:]
        k_s, v_s = kv_s[:, :KV_COLS], kv_s[:, KV_COLS:]
        p_k.append(kv_p[..., :KV_COLS].reshape(batch, keep, N_KV_HEADS, HEAD_DIM))
        p_v.append(kv_p[..., KV_COLS:].reshape(batch, keep, N_KV_HEADS, HEAD_DIM))
        s_k.append(k_s.reshape(dec_batch, dec_seq, N_KV_HEADS, HEAD_DIM))
        s_v.append(v_s.reshape(dec_batch, dec_seq, N_KV_HEADS, HEAD_DIM))

        att_s = attn_sample(q[mp:], k_s, v_s, state_swa_k[l], state_swa_v[l],
                            attn_sinks[l], nb=8)
        att = attn_prompt(q, kv, attn_sinks[l], att_s, m_total=m, m_prompt=mp, seq=seq,
                          per_step=2)

        gmo_s, gv_s = gmlp_sample(gm, gm_spatial[l], gm_bias[l], gm_norm[l].reshape(1, -1),
                                  m_prompt=mp, m_sample=ms, t_new=dec_seq)
        gmo = gmlp_prompt(gm, gm_spatial[l], gm_bias[l].T, gm_norm[l].reshape(1, -1),
                          gmo_s, m_total=m, m_prompt=mp, per_step=2)
        s_g.append(gv_s.reshape(dec_batch, dec_seq, GM_HEADS, GM_HEAD_DIM))

        mix, w_in_next = matmul_two(att, gmo, w_out_b, 0, tm=tm_mm, tn=1024,
                                    out_dtype=BF16, name="mm_out",
                                    cast=None if last else (w_in, l + 1, 2 * cast_rows))
        x, h = post_rows(mix, x, norm_post_mix[l].reshape(1, d),
                         norm_pre_ffn[l].reshape(1, d), tm=tm_rows)

        a, tail, a_s, g_s, w_down_b = ffn_a(
            h, w_ffn_gate, w_ffn_up, w_ffn_down, conv_w, conv_b3, state_conv[l], l,
            m_prompt=mp, m_sample=ms, seq=seq, t_new=dec_seq, tf=tf, chunk=512,
            cast_rows=2 * cast_rows)
        a = place_rows(a, a_s, row_block=mp // ms)
        p_c.append(tail[:, 8 - (CONV_W - 1):])
        s_c.append(g_s.reshape(dec_batch, dec_seq, f)[:, dec_seq - (CONV_W - 1):])

        ffn, _ = matmul_cols(a, w_down_b[None], 0, col_block0=0, n=d, tm=tm_down, tn=256,
                             out_dtype=BF16, epilogue=_identity, name="mm_down")
        x, h = post_rows(ffn, x, norm_post_ffn[l].reshape(1, d),
                         None if last else norm_pre_mix[l + 1].reshape(1, d),
                         tm=tm_rows, split_out=(mp, ms) if last else None)
        if not last:
            w_in_b = w_in_next[None]

    x_p, x_s = x
    return (x_p.reshape(batch, seq, d), x_s.reshape(dec_batch, dec_seq, d),
            jnp.stack(p_k), jnp.stack(p_v), jnp.stack(p_c),
            jnp.stack(s_k), jnp.stack(s_v), jnp.stack(s_c), jnp.stack(s_g))
```

```python
import functools
import math

import jax
import jax.numpy as jnp
from jax import lax
from jax.experimental import pallas as pl
from jax.experimental.pallas import tpu as pltpu

F32 = jnp.float32
BF16 = jnp.bfloat16

D_MODEL = 4096
HEAD_DIM = 64
N_HEADS = 32
N_KV_HEADS = 4
GQA_GROUP = 8
WINDOW = 128
CHUNK = 128
GM_HEAD_DIM = 128
GM_HEADS = 16
GM_WIDTH = 2048
Q_COLS = 2048
KV_COLS = 256
CONV_W = 3
EPS = 1e-6
LANES = 128
LOG2_E = math.log2(math.e)

MIB = 1024 * 1024


def _params(semantics, vmem_mib):
    return pltpu.CompilerParams(
        dimension_semantics=semantics, vmem_limit_bytes=vmem_mib * MIB
    )


def _rms(x, g):
    ms = jnp.mean(x * x, axis=-1, keepdims=True)
    return x * lax.rsqrt(ms + EPS) * g


def _split_specs(tm, d, n_prompt_tiles):
    return [
        pl.BlockSpec((tm, d), lambda i: (jnp.minimum(i, n_prompt_tiles - 1), 0)),
        pl.BlockSpec((tm, d), lambda i: (0, 0)),
    ]


def _load_rows(refs, n_prompt_tiles):
    if len(refs) == 1:
        return refs[0][...]
    is_prompt = pl.program_id(0) < n_prompt_tiles
    return jnp.where(is_prompt, refs[0][...], refs[1][...])


def _store_rows(refs, value, n_prompt_tiles):
    if len(refs) == 1:
        refs[0][...] = value
        return

    @pl.when(pl.program_id(0) < n_prompt_tiles)
    def _():
        refs[0][...] = value

    @pl.when(pl.program_id(0) >= n_prompt_tiles)
    def _():
        refs[1][...] = value


def _rms_kernel(*refs, n_in, n_prompt_tiles):
    x = _load_rows(refs[:n_in], n_prompt_tiles)
    g_ref, o_ref = refs[n_in:]
    o_ref[...] = _rms(x, g_ref[...]).astype(o_ref.dtype)


def rms_rows(x_parts, g, *, tm):
    d = x_parts[0].shape[1]
    m = sum(p.shape[0] for p in x_parts)
    npt = x_parts[0].shape[0] // tm
    row = pl.BlockSpec((tm, d), lambda i: (i, 0))
    x_specs = [row] if len(x_parts) == 1 else _split_specs(tm, d, npt)
    return pl.pallas_call(
        functools.partial(_rms_kernel, n_in=len(x_parts), n_prompt_tiles=npt),
        grid=(m // tm,),
        in_specs=x_specs + [pl.BlockSpec((1, d), lambda i: (0, 0))],
        out_specs=row,
        out_shape=jax.ShapeDtypeStruct((m, d), BF16),
        compiler_params=_params(("arbitrary",), 40),
        name="rms_rows",
    )(*x_parts, g)


def _post_kernel(*refs, n_in, n_out, has_next, n_prompt_tiles):
    m_ref = refs[0]
    x = _load_rows(refs[1:1 + n_in], n_prompt_tiles)
    rest = refs[1 + n_in:]
    xn = x + _rms(m_ref[...].astype(F32), rest[0][...])
    rest = rest[1:]
    if has_next:
        gnext_ref, rest = rest[0], rest[1:]
    _store_rows(rest[:n_out], xn, n_prompt_tiles)
    if has_next:
        ho_ref = rest[n_out]
        ho_ref[...] = _rms(xn, gnext_ref[...]).astype(ho_ref.dtype)


def post_rows(m_in, x_parts, g_post, g_next, *, tm, split_out=None):
    m, d = m_in.shape
    npt = (x_parts[0].shape[0] if split_out is None else split_out[0]) // tm
    if len(x_parts) == 1 and split_out is None:
        npt = m // tm
    row = pl.BlockSpec((tm, d), lambda i: (i, 0))
    vec = pl.BlockSpec((1, d), lambda i: (0, 0))
    x_specs = [row] if len(x_parts) == 1 else _split_specs(tm, d, npt)
    if split_out is None:
        xo_specs = [row]
        xo_shapes = [jax.ShapeDtypeStruct((m, d), F32)]
    else:
        xo_specs = _split_specs(tm, d, npt)
        xo_shapes = [jax.ShapeDtypeStruct((r, d), F32) for r in split_out]
    has_next = g_next is not None
    gains = [g_post, g_next] if has_next else [g_post]
    out = pl.pallas_call(
        functools.partial(_post_kernel, n_in=len(x_parts), n_out=len(xo_specs),
                          has_next=has_next, n_prompt_tiles=npt),
        grid=(m // tm,),
        in_specs=[row] + x_specs + [vec] * len(gains),
        out_specs=xo_specs + ([row] if has_next else []),
        out_shape=xo_shapes
        + ([jax.ShapeDtypeStruct((m, d), BF16)] if has_next else []),
        compiler_params=_params(("arbitrary",), 48),
        name="post_rows",
    )(m_in, *x_parts, *gains)
    x_new = out[:len(xo_specs)]
    return (x_new, out[-1]) if has_next else (x_new, None)


def _grid_step():
    return pl.program_id(0) * pl.num_programs(1) + pl.program_id(1)


def _side_cast(src_ref, dst_ref, n_blocks):
    @pl.when(_grid_step() < n_blocks)
    def _():
        dst_ref[...] = src_ref[...].astype(dst_ref.dtype)


def _side_cast_specs(src, layer, block_rows, n_steps, n_cols_grid):
    _, rows, cols = src.shape
    n_blocks = rows // block_rows
    assert rows % block_rows == 0 and n_blocks <= n_steps

    def block(i, j):
        return jnp.minimum(i * n_cols_grid + j, n_blocks - 1)

    in_spec = pl.BlockSpec((None, block_rows, cols), lambda i, j: (layer, block(i, j), 0))
    out_spec = pl.BlockSpec((block_rows, cols), lambda i, j: (block(i, j), 0))
    return in_spec, out_spec, jax.ShapeDtypeStruct((rows, cols), BF16), n_blocks


def _mm_kernel(a_ref, b_ref, *rest, epilogue, cast_blocks):
    o_ref = rest[-2] if cast_blocks else rest[-1]
    acc = jnp.dot(a_ref[...], b_ref[...], preferred_element_type=F32)
    o_ref[...] = epilogue(acc).astype(o_ref.dtype)
    if cast_blocks:
        _side_cast(rest[0], rest[-1], cast_blocks)


def _mm2_kernel(a1_ref, a2_ref, b1_ref, b2_ref, *rest, cast_blocks):
    o_ref = rest[-2] if cast_blocks else rest[-1]
    acc = jnp.dot(a1_ref[...], b1_ref[...], preferred_element_type=F32)
    acc = acc + jnp.dot(a2_ref[...], b2_ref[...], preferred_element_type=F32)
    o_ref[...] = acc.astype(o_ref.dtype)
    if cast_blocks:
        _side_cast(rest[0], rest[-1], cast_blocks)


def _with_side_cast(cast, grid, in_specs, out_specs, out_shapes, operands):
    if cast is None:
        return 0
    src, layer, block_rows = cast
    i_spec, o_spec, o_shape, n_blocks = _side_cast_specs(
        src, layer, block_rows, grid[0] * grid[1], grid[1]
    )
    in_specs.append(i_spec)
    out_specs.append(o_spec)
    out_shapes.append(o_shape)
    operands.append(src)
    return n_blocks


def matmul_cols(a, w, layer, *, col_block0, n, tm, tn, out_dtype, epilogue, name,
                cast=None, vmem_mib=56):
    m, k = a.shape
    grid = (m // tm, n // tn)
    in_specs = [
        pl.BlockSpec((tm, k), lambda i, j: (i, 0)),
        pl.BlockSpec((None, k, tn), lambda i, j: (layer, 0, j + col_block0)),
    ]
    out_specs = [pl.BlockSpec((tm, tn), lambda i, j: (i, j))]
    out_shapes = [jax.ShapeDtypeStruct((m, n), out_dtype)]
    operands = [a, w]
    cast_blocks = _with_side_cast(cast, grid, in_specs, out_specs, out_shapes, operands)
    out = pl.pallas_call(
        functools.partial(_mm_kernel, epilogue=epilogue, cast_blocks=cast_blocks),
        grid=grid,
        in_specs=in_specs,
        out_specs=out_specs,
        out_shape=out_shapes,
        compiler_params=_params(("arbitrary", "arbitrary"), vmem_mib),
        name=name,
    )(*operands)
    return out[0], (out[1] if cast_blocks else None)


def matmul_two(a1, a2, w, layer, *, tm, tn, out_dtype, name, cast=None):
    m, k1 = a1.shape
    k2 = a2.shape[1]
    assert k1 == k2
    n = w.shape[-1]
    grid = (m // tm, n // tn)
    in_specs = [
        pl.BlockSpec((tm, k1), lambda i, j: (i, 0)),
        pl.BlockSpec((tm, k2), lambda i, j: (i, 0)),
        pl.BlockSpec((None, None, k1, tn), lambda i, j: (layer, 0, 0, j)),
        pl.BlockSpec((None, None, k2, tn), lambda i, j: (layer, 1, 0, j)),
    ]
    out_specs = [pl.BlockSpec((tm, tn), lambda i, j: (i, j))]
    out_shapes = [jax.ShapeDtypeStruct((m, n), out_dtype)]
    operands = [a1, a2, w, w]
    cast_blocks = _with_side_cast(cast, grid, in_specs, out_specs, out_shapes, operands)
    out = pl.pallas_call(
        functools.partial(_mm2_kernel, cast_blocks=cast_blocks),
        grid=grid,
        in_specs=in_specs,
        out_specs=out_specs,
        out_shape=out_shapes,
        compiler_params=_params(("arbitrary", "arbitrary"), 56),
        name=name,
    )(*operands)
    return out[0], (out[1] if cast_blocks else None)


def _identity(x):
    return x


def _scale_q(x):
    return x * (HEAD_DIM ** -0.5 * LOG2_E)


_GELU_A = -2.0 * LOG2_E * math.sqrt(2.0 / math.pi)
_GELU_B = _GELU_A * 0.044715


def _gelu_times(x, y):
    t = x * x * _GELU_B + _GELU_A
    return (x * y) / (1.0 + jnp.exp2(t * x))


def _gelu(x):
    t = x * x * _GELU_B + _GELU_A
    return x / (1.0 + jnp.exp2(t * x))


def _softmax2_with_sink(s, sink):
    m = jnp.maximum(jnp.max(s, axis=-1, keepdims=True), sink)
    e = jnp.exp2(s - m)
    den = jnp.sum(e, axis=-1, keepdims=True) + jnp.exp2(sink - m)
    return e / den


def _attn_prompt_kernel(sink_ref, q_ref, kvc_ref, kvp_ref, os_ref, o_ref, *,
                        blocks_per_seq, n_prompt_steps, per_step):
    step = pl.program_id(0)
    w = WINDOW

    @pl.when(step >= n_prompt_steps)
    def _():
        o_ref[...] = os_ref[...]

    @pl.when(step < n_prompt_steps)
    def _():
        for b in range(per_step):
            rows = pl.ds(b * w, w)
            prev_ref = kvp_ref if b == 0 else kvc_ref.at[pl.ds((b - 1) * w, w)]
            _attn_prompt_block(sink_ref, q_ref.at[rows], kvc_ref.at[rows], prev_ref,
                               o_ref.at[rows], (step * per_step + b) % blocks_per_seq)


def _attn_prompt_block(sink_ref, q_ref, kvc_ref, kvp_ref, o_ref, n):
    w = WINDOW
    row = lax.broadcasted_iota(jnp.int32, (w, 2 * w), 0)
    col = lax.broadcasted_iota(jnp.int32, (w, 2 * w), 1)
    prev_ok = jnp.logical_and(jnp.logical_and(col < w, col > row), n > 0)
    own_ok = jnp.logical_and(col >= w, col - w <= row)
    valid = jnp.logical_or(prev_ok, own_ok)
    for kvh in range(N_KV_HEADS):
        ks = slice(kvh * HEAD_DIM, (kvh + 1) * HEAD_DIM)
        vs = slice(KV_COLS + kvh * HEAD_DIM, KV_COLS + (kvh + 1) * HEAD_DIM)
        k = jnp.concatenate([kvp_ref[:, ks], kvc_ref[:, ks]], axis=0).astype(BF16)
        v = jnp.concatenate([kvp_ref[:, vs], kvc_ref[:, vs]], axis=0).astype(BF16)
        v_ext = jnp.concatenate(
            [v, jnp.zeros((2 * w, LANES - HEAD_DIM), BF16), jnp.ones((2 * w, LANES), BF16)],
            axis=1,
        )
        heads = [kvh * GQA_GROUP + g for g in range(GQA_GROUP)]
        qs = jnp.concatenate(
            [q_ref[:, h * HEAD_DIM:(h + 1) * HEAD_DIM] for h in heads], axis=0
        )
        s = lax.dot_general(
            qs, k, (((1,), (1,)), ((), ())), preferred_element_type=F32
        )
        es, sink_terms = [], []
        for g, h in enumerate(heads):
            sink = sink_ref[h] * LOG2_E
            sg = jnp.where(valid, s[g * w:(g + 1) * w], -jnp.inf)
            m = jnp.maximum(jnp.max(sg, axis=-1, keepdims=True), sink)
            es.append(jnp.exp2(sg - m).astype(BF16))
            sink_terms.append(jnp.exp2(sink - m))
        o = jnp.dot(jnp.concatenate(es, axis=0), v_ext, preferred_element_type=F32)
        for g, h in enumerate(heads):
            og = o[g * w:(g + 1) * w]
            out = og[:, :LANES] / (og[:, LANES:] + sink_terms[g])
            o_ref[:, h * HEAD_DIM:(h + 1) * HEAD_DIM] = out[:, :HEAD_DIM].astype(
                o_ref.dtype
            )


def attn_prompt(q, kv, sinks, att_sample, *, m_total, m_prompt, seq, per_step):
    w = WINDOW
    rows = per_step * w
    assert m_prompt % rows == 0 and att_sample.shape[0] % rows == 0
    nsteps = m_prompt // rows
    last = nsteps - 1
    last_prev = m_prompt // w - 1
    return pl.pallas_call(
        functools.partial(_attn_prompt_kernel, blocks_per_seq=seq // w,
                          n_prompt_steps=nsteps, per_step=per_step),
        grid=(m_total // rows,),
        in_specs=[
            pl.BlockSpec(memory_space=pltpu.SMEM),
            pl.BlockSpec((rows, Q_COLS), lambda r: (jnp.minimum(r, last), 0)),
            pl.BlockSpec((rows, 2 * KV_COLS), lambda r: (jnp.minimum(r, last), 0)),
            pl.BlockSpec((w, 2 * KV_COLS),
                         lambda r: (jnp.clip(per_step * r - 1, 0, last_prev), 0)),
            pl.BlockSpec((rows, Q_COLS), lambda r: (jnp.maximum(r - nsteps, 0), 0)),
        ],
        out_specs=pl.BlockSpec((rows, Q_COLS), lambda r: (r, 0)),
        out_shape=jax.ShapeDtypeStruct((m_total, Q_COLS), BF16),
        compiler_params=_params(("arbitrary",), 40),
        name="attn_prompt",
    )(sinks, q, kv, kv, att_sample)


def _attn_sample_kernel(sink_ref, q_ref, k_ref, v_ref, o_ref, *, t_new):
    nb, nkv, rows, d = q_ref.shape
    keys = k_ref.shape[2]
    q = q_ref[...].reshape(nb * nkv, rows, d)
    k = k_ref[...].reshape(nb * nkv, keys, d)
    v = v_ref[...].reshape(nb * nkv, keys, d)
    s = jnp.einsum("bqd,bkd->bqk", q, k, preferred_element_type=F32)
    t = lax.broadcasted_iota(jnp.int32, (rows, keys), 0) // GQA_GROUP
    col = lax.broadcasted_iota(jnp.int32, (rows, keys), 1)
    valid = jnp.logical_and(col > t, col <= WINDOW + t)
    s = jnp.where(valid[None], s, -jnp.inf)
    sink = jnp.broadcast_to(sink_ref[...][None], (nb, nkv, rows, 1)).reshape(
        nb * nkv, rows, 1
    )
    p = _softmax2_with_sink(s, sink * LOG2_E).astype(BF16)
    o = jnp.einsum("bqk,bkd->bqd", p, v, preferred_element_type=F32)
    o_ref[...] = o.reshape(nb, nkv, rows, d).astype(o_ref.dtype)


def attn_sample(q_s, k_new, v_new, k_buf, v_buf, sinks, *, nb):
    b, l = k_buf.shape[0], k_buf.shape[1]
    t_new = q_s.shape[0] // b
    rows = t_new * GQA_GROUP
    keys = 2 * WINDOW
    q4 = q_s.reshape(b, t_new, N_KV_HEADS, GQA_GROUP, HEAD_DIM)
    q4 = q4.transpose(0, 2, 1, 3, 4).reshape(b, N_KV_HEADS, rows, HEAD_DIM)

    def keys_layout(buf, new):
        cat = jnp.concatenate(
            [buf, new.reshape(b, t_new, N_KV_HEADS, HEAD_DIM)], axis=1
        )
        cat = jnp.pad(cat, ((0, 0), (0, keys - l - t_new), (0, 0), (0, 0)))
        return cat.transpose(0, 2, 1, 3).astype(BF16)

    k4 = keys_layout(k_buf, k_new)
    v4 = keys_layout(v_buf, v_new)
    sink_rows = jnp.broadcast_to(
        sinks.reshape(N_KV_HEADS, 1, GQA_GROUP), (N_KV_HEADS, t_new, GQA_GROUP)
    ).reshape(N_KV_HEADS, rows, 1)
    o4 = pl.pallas_call(
        functools.partial(_attn_sample_kernel, t_new=t_new),
        grid=(b // nb,),
        in_specs=[
            pl.BlockSpec((N_KV_HEADS, rows, 1), lambda i: (0, 0, 0)),
            pl.BlockSpec((nb, N_KV_HEADS, rows, HEAD_DIM), lambda i: (i, 0, 0, 0)),
            pl.BlockSpec((nb, N_KV_HEADS, keys, HEAD_DIM), lambda i: (i, 0, 0, 0)),
            pl.BlockSpec((nb, N_KV_HEADS, keys, HEAD_DIM), lambda i: (i, 0, 0, 0)),
        ],
        out_specs=pl.BlockSpec(
            (nb, N_KV_HEADS, rows, HEAD_DIM), lambda i: (i, 0, 0, 0)
        ),
        out_shape=jax.ShapeDtypeStruct((b, N_KV_HEADS, rows, HEAD_DIM), BF16),
        compiler_params=_params(("parallel",), 40),
        name="attn_sample",
    )(sink_rows, q4, k4, v4)
    o = o4.reshape(b, N_KV_HEADS, t_new, GQA_GROUP, HEAD_DIM).transpose(0, 2, 1, 3, 4)
    return o.reshape(b * t_new, Q_COLS)


def _place_rows_kernel(rows_ref, buf_ref, o_ref):
    del buf_ref
    o_ref[...] = rows_ref[...]


def place_rows(buf, rows, *, row_block):
    r, c = rows.shape
    return pl.pallas_call(
        _place_rows_kernel,
        grid=(1,),
        in_specs=[
            pl.BlockSpec((r, c), lambda i: (0, 0)),
            pl.BlockSpec(memory_space=pl.ANY),
        ],
        out_specs=pl.BlockSpec((r, c), lambda i: (row_block, 0)),
        out_shape=jax.ShapeDtypeStruct(buf.shape, buf.dtype),
        input_output_aliases={1: 0},
        compiler_params=_params(("arbitrary",), 16),
        name="place_rows",
    )(rows, buf)


def _gmlp_prompt_kernel(u_ref, v_ref, w_ref, bt_ref, gn_ref, os_ref, o_ref, *,
                        n_prompt_steps, per_step):
    @pl.when(pl.program_id(0) >= n_prompt_steps)
    def _():
        o_ref[...] = os_ref[...]

    @pl.when(pl.program_id(0) < n_prompt_steps)
    def _():
        for b in range(per_step):
            rows = pl.ds(b * CHUNK, CHUNK)
            _gmlp_prompt_chunk(u_ref.at[rows], v_ref.at[rows], w_ref, bt_ref, gn_ref,
                               o_ref.at[rows])


def _gmlp_prompt_chunk(u_ref, v_ref, w_ref, bt_ref, gn_ref, o_ref):
    gv = _rms(v_ref[...].astype(F32), gn_ref[...])
    c = CHUNK
    row = lax.broadcasted_iota(jnp.int32, (c, c), 0)
    col = lax.broadcasted_iota(jnp.int32, (c, c), 1)
    causal = col <= row
    for h in range(GM_HEADS):
        cs = slice(h * GM_HEAD_DIM, (h + 1) * GM_HEAD_DIM)
        w = jnp.where(causal, w_ref[h], 0.0).astype(BF16)
        mixed = jnp.dot(w, gv[:, cs].astype(BF16), preferred_element_type=F32)
        mixed = mixed + bt_ref[:, h:h + 1]
        o_ref[:, cs] = (u_ref[:, cs].astype(F32) * mixed).astype(o_ref.dtype)


def gmlp_prompt(gm, w_s, b_t, g_norm, gated_sample, *, m_total, m_prompt, per_step):
    c = CHUNK
    rows = per_step * c
    assert m_prompt % rows == 0 and gated_sample.shape[0] % rows == 0
    nsteps = m_prompt // rows
    last = nsteps - 1
    return pl.pallas_call(
        functools.partial(_gmlp_prompt_kernel, n_prompt_steps=nsteps, per_step=per_step),
        grid=(m_total // rows,),
        in_specs=[
            pl.BlockSpec((rows, GM_WIDTH), lambda r: (jnp.minimum(r, last), 0)),
            pl.BlockSpec((rows, GM_WIDTH), lambda r: (jnp.minimum(r, last), 1)),
            pl.BlockSpec((GM_HEADS, c, c), lambda r: (0, 0, 0)),
            pl.BlockSpec((c, GM_HEADS), lambda r: (0, 0)),
            pl.BlockSpec((1, GM_WIDTH), lambda r: (0, 0)),
            pl.BlockSpec((rows, GM_WIDTH), lambda r: (jnp.maximum(r - nsteps, 0), 0)),
        ],
        out_specs=pl.BlockSpec((rows, GM_WIDTH), lambda r: (r, 0)),
        out_shape=jax.ShapeDtypeStruct((m_total, GM_WIDTH), BF16),
        compiler_params=_params(("arbitrary",), 40),
        name="gmlp_prompt",
    )(gm, gm, w_s, b_t, g_norm, gated_sample)


def _gmlp_sample_kernel(u_ref, v_ref, wc_ref, bc_ref, gn_ref, o_ref, gv_ref, *, t_new):
    rows = u_ref.shape[0]
    nseq = rows // t_new
    gv = _rms(v_ref[...].astype(F32), gn_ref[...])
    gv_ref[...] = gv
    t = lax.broadcasted_iota(jnp.int32, (t_new, GM_HEAD_DIM), 0)
    for h in range(GM_HEADS):
        cs = slice(h * GM_HEAD_DIM, (h + 1) * GM_HEAD_DIM)
        x = gv[:, cs].reshape(nseq, t_new, GM_HEAD_DIM)
        acc = jnp.zeros((nseq, t_new, GM_HEAD_DIM), F32)
        for s in range(t_new):
            coef = jnp.where(t >= s, wc_ref[h, s], 0.0)
            acc = acc + coef[None] * x[:, s:s + 1, :]
        mixed = acc + bc_ref[h][None]
        u = u_ref[:, cs].astype(F32).reshape(nseq, t_new, GM_HEAD_DIM)
        o_ref[:, cs] = (u * mixed).reshape(rows, GM_HEAD_DIM).astype(o_ref.dtype)


def gmlp_sample(gm, w_s, b_s, g_norm, *, m_prompt, m_sample, t_new):
    lanes = GM_HEAD_DIM
    w8 = w_s[:, :t_new, :t_new]
    wc = jnp.broadcast_to(
        w8.transpose(0, 2, 1)[..., None], (GM_HEADS, t_new, t_new, lanes)
    )
    bc = jnp.broadcast_to(b_s[:, :t_new, None], (GM_HEADS, t_new, lanes))
    blk = m_prompt // m_sample
    return pl.pallas_call(
        functools.partial(_gmlp_sample_kernel, t_new=t_new),
        grid=(1,),
        in_specs=[
            pl.BlockSpec((m_sample, GM_WIDTH), lambda i: (blk, 0)),
            pl.BlockSpec((m_sample, GM_WIDTH), lambda i: (blk, 1)),
            pl.BlockSpec((GM_HEADS, t_new, t_new, lanes), lambda i: (0, 0, 0, 0)),
            pl.BlockSpec((GM_HEADS, t_new, lanes), lambda i: (0, 0, 0)),
            pl.BlockSpec((1, GM_WIDTH), lambda i: (0, 0)),
        ],
        out_specs=[
            pl.BlockSpec((m_sample, GM_WIDTH), lambda i: (0, 0)),
            pl.BlockSpec((m_sample, GM_WIDTH), lambda i: (0, 0)),
        ],
        out_shape=[
            jax.ShapeDtypeStruct((m_sample, GM_WIDTH), BF16),
            jax.ShapeDtypeStruct((m_sample, GM_WIDTH), F32),
        ],
        compiler_params=_params(("arbitrary",), 40),
        name="gmlp_sample",
    )(gm, gm, wc, bc, g_norm)


def _conv_gelu_gate(g, g1, g2, up, cw_ref, cb_ref):
    c = cb_ref[...] + cw_ref[0:1, :] * g2
    c = c + cw_ref[1:2, :] * g1
    c = c + cw_ref[2:3, :] * g
    return _gelu_times(c, up)


def _ffn_a_kernel(h_ref, hs_ref, wg_ref, wu_ref, cw_ref, cb_ref, st_ref, wd_ref, a_ref,
                  tail_ref, as_ref, gs_ref, wdb_ref, *, chunk, t_new, cast_blocks):
    i = pl.program_id(0)
    _side_cast(wd_ref, wdb_ref, cast_blocks)
    tm = h_ref.shape[0]
    tf = wg_ref.shape[1]
    wg = wg_ref[...].astype(BF16)
    wu = wu_ref[...].astype(BF16)

    prev8 = jnp.zeros((8, tf), F32)
    for r0 in range(0, tm, chunk):
        h = h_ref[r0:r0 + chunk, :]
        g = jnp.dot(h, wg, preferred_element_type=F32)
        up = jnp.dot(h, wu, preferred_element_type=F32)
        ext = jnp.concatenate([prev8, g], axis=0)
        g1 = pltpu.roll(ext, 1, 0)[8:]
        g2 = pltpu.roll(ext, 2, 0)[8:]
        a_ref[r0:r0 + chunk, :] = _conv_gelu_gate(g, g1, g2, up, cw_ref, cb_ref).astype(
            a_ref.dtype
        )
        prev8 = g[chunk - 8:, :]
    tail_ref[...] = prev8

    @pl.when(i == pl.num_programs(0) - 1)
    def _():
        hs = hs_ref[...]
        rows = hs.shape[0]
        g = jnp.dot(hs, wg, preferred_element_type=F32)
        up = jnp.dot(hs, wu, preferred_element_type=F32)
        gs_ref[...] = g
        t = lax.broadcasted_iota(jnp.int32, g.shape, 0) % t_new
        st2 = st_ref[...]
        st1 = pltpu.roll(st2, rows - 1, 0)
        g1 = jnp.where(t >= 1, pltpu.roll(g, 1, 0), st1)
        g2 = jnp.where(t >= 2, pltpu.roll(g, 2, 0), st2)
        as_ref[...] = _conv_gelu_gate(g, g1, g2, up, cw_ref, cb_ref).astype(as_ref.dtype)


def ffn_a(h, w_gate, w_up, w_down, conv_w, conv_b, conv_state, layer, *, m_prompt,
          m_sample, seq, t_new, tf, chunk, cast_rows):
    m_total, d = h.shape
    f = w_gate.shape[-1]
    nf = f // tf
    ni = m_prompt // seq
    assert t_new % 8 == 0 and t_new >= CONV_W - 1
    st = jnp.pad(conv_state, ((0, 0), (0, t_new - (CONV_W - 1)), (0, 0)))
    st = st.reshape(m_sample, f)
    sample_col = lambda i, j: (0, jnp.where(i == ni - 1, j, 0))
    wd_in, wd_out, wd_shape, cast_blocks = _side_cast_specs(
        w_down, layer, cast_rows, ni * nf, nf
    )
    return pl.pallas_call(
        functools.partial(_ffn_a_kernel, chunk=chunk, t_new=t_new,
                          cast_blocks=cast_blocks),
        grid=(ni, nf),
        in_specs=[
            pl.BlockSpec((seq, d), lambda i, j: (i, 0), pipeline_mode=pl.Buffered(1)),
            pl.BlockSpec((m_sample, d), lambda i, j: (m_prompt // m_sample, 0)),
            pl.BlockSpec((None, d, tf), lambda i, j: (layer, 0, j)),
            pl.BlockSpec((None, d, tf), lambda i, j: (layer, 0, j)),
            pl.BlockSpec((None, CONV_W, tf), lambda i, j: (layer, 0, j)),
            pl.BlockSpec((None, 1, tf), lambda i, j: (layer, 0, j)),
            pl.BlockSpec((m_sample, tf), lambda i, j: (0, j)),
            wd_in,
        ],
        out_specs=[
            pl.BlockSpec((seq, tf), lambda i, j: (i, j)),
            pl.BlockSpec((None, 8, tf), lambda i, j: (i, 0, j)),
            pl.BlockSpec((m_sample, tf), sample_col),
            pl.BlockSpec((m_sample, tf), sample_col),
            wd_out,
        ],
        out_shape=[
            jax.ShapeDtypeStruct((m_total, f), BF16),
            jax.ShapeDtypeStruct((ni, 8, f), F32),
            jax.ShapeDtypeStruct((m_sample, f), BF16),
            jax.ShapeDtypeStruct((m_sample, f), F32),
            wd_shape,
        ],
        compiler_params=_params(("arbitrary", "arbitrary"), 56),
        name="ffn_a",
    )(h, h, w_gate, w_up, conv_w, conv_b, st, w_down)


def kernel(x_prompt, x_sample, state_swa_k, state_swa_v, state_conv, w_in, w_out,
           attn_sinks, gm_spatial, gm_bias, gm_norm, norm_pre_mix, norm_post_mix,
           norm_pre_ffn, norm_post_ffn, w_ffn_gate, w_ffn_up, w_ffn_down, conv_w,
           conv_b):
    batch, seq, d = x_prompt.shape
    dec_batch, dec_seq, _ = x_sample.shape
    depth = w_in.shape[0]
    f = w_ffn_gate.shape[-1]
    mp = batch * seq
    ms = dec_batch * dec_seq
    m = mp + ms
    keep = min(WINDOW, seq)

    tm_mm = m // 8
    tm_down = m // 12
    tf = 256
    tm_rows = ms
    assert mp % tm_rows == 0

    w_in_b = w_in[0].astype(BF16)[None]
    conv_b3 = conv_b.reshape(depth, 1, f)
    cast_rows = 64

    x = (x_prompt.reshape(mp, d), x_sample.reshape(ms, d))
    h = rms_rows(x, norm_pre_mix[0].reshape(1, d), tm=tm_rows)

    p_k, p_v, p_c, s_k, s_v, s_c, s_g = [], [], [], [], [], [], []
    for l in range(depth):
        last = l + 1 == depth
        tn = 512
        q, _ = matmul_cols(h, w_in_b, 0, col_block0=0, n=Q_COLS, tm=tm_mm, tn=2 * tn,
                           out_dtype=BF16, epilogue=_scale_q, name="mm_q")
        kv, _ = matmul_cols(h, w_in_b, 0, col_block0=Q_COLS // tn, n=2 * KV_COLS,
                            tm=tm_mm, tn=tn, out_dtype=F32, epilogue=_identity,
                            name="mm_kv")
        gm, w_out_b = matmul_cols(
            h, w_in_b, 0, col_block0=(Q_COLS + 2 * KV_COLS) // tn, n=2 * GM_WIDTH,
            tm=tm_mm, tn=tn, out_dtype=BF16, epilogue=_gelu, name="mm_gm",
            cast=(w_out, l, cast_rows))
        w_out_b = w_out_b.reshape(1, 2, Q_COLS, d)

        kv_p = jnp.stack([kv[(b + 1) * seq - keep:(b + 1) * seq] for b in range(batch)])
        kv_s = kv[mp:]
        k_s, v_s = kv_s[:, :KV_COLS], kv_s[:, KV_COLS:]
        p_k.append(kv_p[..., :KV_COLS].reshape(batch, keep, N_KV_HEADS, HEAD_DIM))
        p_v.append(kv_p[..., KV_COLS:].reshape(batch, keep, N_KV_HEADS, HEAD_DIM))
        s_k.append(k_s.reshape(dec_batch, dec_seq, N_KV_HEADS, HEAD_DIM))
        s_v.append(v_s.reshape(dec_batch, dec_seq, N_KV_HEADS, HEAD_DIM))

        att_s = attn_sample(q[mp:], k_s, v_s, state_swa_k[l], state_swa_v[l],
                            attn_sinks[l], nb=8)
        att = attn_prompt(q, kv, attn_sinks[l], att_s, m_total=m, m_prompt=mp, seq=seq,
                          per_step=2)

        gmo_s, gv_s = gmlp_sample(gm, gm_spatial[l], gm_bias[l], gm_norm[l].reshape(1, -1),
                                  m_prompt=mp, m_sample=ms, t_new=dec_seq)
        gmo = gmlp_prompt(gm, gm_spatial[l], gm_bias[l].T, gm_norm[l].reshape(1, -1),
                          gmo_s, m_total=m, m_prompt=mp, per_step=2)
        s_g.append(gv_s.reshape(dec_batch, dec_seq, GM_HEADS, GM_HEAD_DIM))

        mix, w_in_next = matmul_two(att, gmo, w_out_b, 0, tm=tm_mm, tn=1024,
                                    out_dtype=BF16, name="mm_out",
                                    cast=None if last else (w_in, l + 1, 2 * cast_rows))
        x, h = post_rows(mix, x, norm_post_mix[l].reshape(1, d),
                         norm_pre_ffn[l].reshape(1, d), tm=tm_rows)

        a, tail, a_s, g_s, w_down_b = ffn_a(
            h, w_ffn_gate, w_ffn_up, w_ffn_down, conv_w, conv_b3, state_conv[l], l,
            m_prompt=mp, m_sample=ms, seq=seq, t_new=dec_seq, tf=tf, chunk=512,
            cast_rows=2 * cast_rows)
        a = place_rows(a, a_s, row_block=mp // ms)
        p_c.append(tail[:, 8 - (CONV_W - 1):])
        s_c.append(g_s.reshape(dec_batch, dec_seq, f)[:, dec_seq - (CONV_W - 1):])

        ffn, _ = matmul_cols(a, w_down_b[None], 0, col_block0=0, n=d, tm=tm_down, tn=512,
                             out_dtype=BF16, epilogue=_identity, name="mm_down",
                             vmem_mib=58)
        x, h = post_rows(ffn, x, norm_post_ffn[l].reshape(1, d),
                         None if last else norm_pre_mix[l + 1].reshape(1, d),
                         tm=tm_rows, split_out=(mp, ms) if last else None)
        if not last:
            w_in_b = w_in_next[None]

    x_p, x_s = x
    return (x_p.reshape(batch, seq, d), x_s.reshape(dec_batch, dec_seq, d),
            jnp.stack(p_k), jnp.stack(p_v), jnp.stack(p_c),
            jnp.stack(s_k), jnp.stack(s_v), jnp.stack(s_c), jnp.stack(s_g))
```

```python
import functools
import math

import jax
import jax.numpy as jnp
from jax import lax
from jax.experimental import pallas as pl
from jax.experimental.pallas import tpu as pltpu

F32 = jnp.float32
BF16 = jnp.bfloat16

D_MODEL = 4096
HEAD_DIM = 64
N_HEADS = 32
N_KV_HEADS = 4
GQA_GROUP = 8
WINDOW = 128
CHUNK = 128
GM_HEAD_DIM = 128
GM_HEADS = 16
GM_WIDTH = 2048
Q_COLS = 2048
KV_COLS = 256
CONV_W = 3
EPS = 1e-6
LANES = 128
LOG2_E = math.log2(math.e)

MIB = 1024 * 1024


def _params(semantics, vmem_mib):
    return pltpu.CompilerParams(
        dimension_semantics=semantics, vmem_limit_bytes=vmem_mib * MIB
    )


def _rms(x, g):
    ms = jnp.mean(x * x, axis=-1, keepdims=True)
    return x * lax.rsqrt(ms + EPS) * g


def _split_specs(tm, d, n_prompt_tiles):
    return [
        pl.BlockSpec((tm, d), lambda i: (jnp.minimum(i, n_prompt_tiles - 1), 0)),
        pl.BlockSpec((tm, d), lambda i: (0, 0)),
    ]


def _load_rows(refs, n_prompt_tiles):
    if len(refs) == 1:
        return refs[0][...]
    is_prompt = pl.program_id(0) < n_prompt_tiles
    return jnp.where(is_prompt, refs[0][...], refs[1][...])


def _store_rows(refs, value, n_prompt_tiles):
    if len(refs) == 1:
        refs[0][...] = value
        return

    @pl.when(pl.program_id(0) < n_prompt_tiles)
    def _():
        refs[0][...] = value

    @pl.when(pl.program_id(0) >= n_prompt_tiles)
    def _():
        refs[1][...] = value


def _rms_kernel(*refs, n_in, n_prompt_tiles):
    x = _load_rows(refs[:n_in], n_prompt_tiles)
    g_ref, o_ref = refs[n_in:]
    o_ref[...] = _rms(x, g_ref[...]).astype(o_ref.dtype)


def rms_rows(x_parts, g, *, tm):
    d = x_parts[0].shape[1]
    m = sum(p.shape[0] for p in x_parts)
    npt = x_parts[0].shape[0] // tm
    row = pl.BlockSpec((tm, d), lambda i: (i, 0))
    x_specs = [row] if len(x_parts) == 1 else _split_specs(tm, d, npt)
    return pl.pallas_call(
        functools.partial(_rms_kernel, n_in=len(x_parts), n_prompt_tiles=npt),
        grid=(m // tm,),
        in_specs=x_specs + [pl.BlockSpec((1, d), lambda i: (0, 0))],
        out_specs=row,
        out_shape=jax.ShapeDtypeStruct((m, d), BF16),
        compiler_params=_params(("arbitrary",), 40),
        name="rms_rows",
    )(*x_parts, g)


def _post_kernel(*refs, n_in, n_out, has_next, n_prompt_tiles):
    m_ref = refs[0]
    x = _load_rows(refs[1:1 + n_in], n_prompt_tiles)
    rest = refs[1 + n_in:]
    xn = x + _rms(m_ref[...].astype(F32), rest[0][...])
    rest = rest[1:]
    if has_next:
        gnext_ref, rest = rest[0], rest[1:]
    _store_rows(rest[:n_out], xn, n_prompt_tiles)
    if has_next:
        ho_ref = rest[n_out]
        ho_ref[...] = _rms(xn, gnext_ref[...]).astype(ho_ref.dtype)


def post_rows(m_in, x_parts, g_post, g_next, *, tm, split_out=None):
    m, d = m_in.shape
    npt = (x_parts[0].shape[0] if split_out is None else split_out[0]) // tm
    if len(x_parts) == 1 and split_out is None:
        npt = m // tm
    row = pl.BlockSpec((tm, d), lambda i: (i, 0))
    vec = pl.BlockSpec((1, d), lambda i: (0, 0))
    x_specs = [row] if len(x_parts) == 1 else _split_specs(tm, d, npt)
    if split_out is None:
        xo_specs = [row]
        xo_shapes = [jax.ShapeDtypeStruct((m, d), F32)]
    else:
        xo_specs = _split_specs(tm, d, npt)
        xo_shapes = [jax.ShapeDtypeStruct((r, d), F32) for r in split_out]
    has_next = g_next is not None
    gains = [g_post, g_next] if has_next else [g_post]
    out = pl.pallas_call(
        functools.partial(_post_kernel, n_in=len(x_parts), n_out=len(xo_specs),
                          has_next=has_next, n_prompt_tiles=npt),
        grid=(m // tm,),
        in_specs=[row] + x_specs + [vec] * len(gains),
        out_specs=xo_specs + ([row] if has_next else []),
        out_shape=xo_shapes
        + ([jax.ShapeDtypeStruct((m, d), BF16)] if has_next else []),
        compiler_params=_params(("arbitrary",), 48),
        name="post_rows",
    )(m_in, *x_parts, *gains)
    x_new = out[:len(xo_specs)]
    return (x_new, out[-1]) if has_next else (x_new, None)


def _grid_step(grid_rank):
    if grid_rank == 1:
        return pl.program_id(0)
    return pl.program_id(0) * pl.num_programs(1) + pl.program_id(1)


def _side_cast(src_ref, dst_ref, n_blocks, grid_rank=2):
    @pl.when(_grid_step(grid_rank) < n_blocks)
    def _():
        dst_ref[...] = src_ref[...].astype(dst_ref.dtype)


def _side_cast_specs(src, layer, block_rows, n_steps, n_cols_grid=None):
    _, rows, cols = src.shape
    n_blocks = rows // block_rows
    assert rows % block_rows == 0 and n_blocks <= n_steps

    def block(*idx):
        step = idx[0] if n_cols_grid is None else idx[0] * n_cols_grid + idx[1]
        return jnp.minimum(step, n_blocks - 1)

    in_spec = pl.BlockSpec((None, block_rows, cols), lambda *idx: (layer, block(*idx), 0))
    out_spec = pl.BlockSpec((block_rows, cols), lambda *idx: (block(*idx), 0))
    return in_spec, out_spec, jax.ShapeDtypeStruct((rows, cols), BF16), n_blocks


def _split_refs(rest, cast_blocks):
    n = len(cast_blocks)
    return rest[n], list(zip(rest[:n], rest[n + 1:], cast_blocks))


def _mm_kernel(a_ref, b_ref, *rest, epilogue, cast_blocks):
    o_ref, casts = _split_refs(rest, cast_blocks)
    acc = jnp.dot(a_ref[...], b_ref[...], preferred_element_type=F32)
    o_ref[...] = epilogue(acc).astype(o_ref.dtype)
    for src_ref, dst_ref, n_blocks in casts:
        _side_cast(src_ref, dst_ref, n_blocks)


def _mm2_kernel(a1_ref, a2_ref, b1_ref, b2_ref, *rest, cast_blocks):
    o_ref, casts = _split_refs(rest, cast_blocks)
    acc = jnp.dot(a1_ref[...], b1_ref[...], preferred_element_type=F32)
    acc = acc + jnp.dot(a2_ref[...], b2_ref[...], preferred_element_type=F32)
    o_ref[...] = acc.astype(o_ref.dtype)
    for src_ref, dst_ref, n_blocks in casts:
        _side_cast(src_ref, dst_ref, n_blocks)


def _with_side_casts(casts, grid, in_specs, out_specs, out_shapes, operands):
    counts = []
    for src, layer, block_rows in casts:
        i_spec, o_spec, o_shape, n_blocks = _side_cast_specs(
            src, layer, block_rows, grid[0] * grid[1], grid[1]
        )
        in_specs.append(i_spec)
        out_specs.append(o_spec)
        out_shapes.append(o_shape)
        operands.append(src)
        counts.append(n_blocks)
    return tuple(counts)


def matmul_cols(a, w, layer, *, col_block0, n, tm, tn, out_dtype, epilogue, name,
                casts=(), vmem_mib=56):
    m, k = a.shape
    grid = (m // tm, n // tn)
    in_specs = [
        pl.BlockSpec((tm, k), lambda i, j: (i, 0)),
        pl.BlockSpec((None, k, tn), lambda i, j: (layer, 0, j + col_block0)),
    ]
    out_specs = [pl.BlockSpec((tm, tn), lambda i, j: (i, j))]
    out_shapes = [jax.ShapeDtypeStruct((m, n), out_dtype)]
    operands = [a, w]
    cast_blocks = _with_side_casts(casts, grid, in_specs, out_specs, out_shapes, operands)
    out = pl.pallas_call(
        functools.partial(_mm_kernel, epilogue=epilogue, cast_blocks=cast_blocks),
        grid=grid,
        in_specs=in_specs,
        out_specs=out_specs,
        out_shape=out_shapes,
        compiler_params=_params(("arbitrary", "arbitrary"), vmem_mib),
        name=name,
    )(*operands)
    return out[0], list(out[1:])


def matmul_two(a1, a2, w, layer, *, tm, tn, out_dtype, name, casts=()):
    m, k1 = a1.shape
    k2 = a2.shape[1]
    assert k1 == k2
    n = w.shape[-1]
    grid = (m // tm, n // tn)
    in_specs = [
        pl.BlockSpec((tm, k1), lambda i, j: (i, 0)),
        pl.BlockSpec((tm, k2), lambda i, j: (i, 0)),
        pl.BlockSpec((None, None, k1, tn), lambda i, j: (layer, 0, 0, j)),
        pl.BlockSpec((None, None, k2, tn), lambda i, j: (layer, 1, 0, j)),
    ]
    out_specs = [pl.BlockSpec((tm, tn), lambda i, j: (i, j))]
    out_shapes = [jax.ShapeDtypeStruct((m, n), out_dtype)]
    operands = [a1, a2, w, w]
    cast_blocks = _with_side_casts(casts, grid, in_specs, out_specs, out_shapes, operands)
    out = pl.pallas_call(
        functools.partial(_mm2_kernel, cast_blocks=cast_blocks),
        grid=grid,
        in_specs=in_specs,
        out_specs=out_specs,
        out_shape=out_shapes,
        compiler_params=_params(("arbitrary", "arbitrary"), 56),
        name=name,
    )(*operands)
    return out[0], list(out[1:])


def _identity(x):
    return x


def _scale_q(x):
    return x * (HEAD_DIM ** -0.5 * LOG2_E)


_GELU_A = -2.0 * LOG2_E * math.sqrt(2.0 / math.pi)
_GELU_B = _GELU_A * 0.044715


def _gelu_times(x, y):
    t = x * x * _GELU_B + _GELU_A
    return (x * y) / (1.0 + jnp.exp2(t * x))


def _gelu(x):
    t = x * x * _GELU_B + _GELU_A
    return x / (1.0 + jnp.exp2(t * x))


def _softmax2_with_sink(s, sink):
    m = jnp.maximum(jnp.max(s, axis=-1, keepdims=True), sink)
    e = jnp.exp2(s - m)
    den = jnp.sum(e, axis=-1, keepdims=True) + jnp.exp2(sink - m)
    return e / den


def _attn_prompt_kernel(sink_ref, q_ref, kvc_ref, kvp_ref, os_ref, src_ref, o_ref, dst_ref,
                        *, blocks_per_seq, n_prompt_steps, per_step, cast_blocks):
    step = pl.program_id(0)
    w = WINDOW
    _side_cast(src_ref, dst_ref, cast_blocks, grid_rank=1)

    @pl.when(step >= n_prompt_steps)
    def _():
        o_ref[...] = os_ref[...]

    @pl.when(step < n_prompt_steps)
    def _():
        for b in range(per_step):
            rows = pl.ds(b * w, w)
            prev_ref = kvp_ref if b == 0 else kvc_ref.at[pl.ds((b - 1) * w, w)]
            _attn_prompt_block(sink_ref, q_ref.at[rows], kvc_ref.at[rows], prev_ref,
                               o_ref.at[rows], (step * per_step + b) % blocks_per_seq)


def _attn_prompt_block(sink_ref, q_ref, kvc_ref, kvp_ref, o_ref, n):
    w = WINDOW
    row = lax.broadcasted_iota(jnp.int32, (w, 2 * w), 0)
    col = lax.broadcasted_iota(jnp.int32, (w, 2 * w), 1)
    prev_ok = jnp.logical_and(jnp.logical_and(col < w, col > row), n > 0)
    own_ok = jnp.logical_and(col >= w, col - w <= row)
    valid = jnp.logical_or(prev_ok, own_ok)
    for kvh in range(N_KV_HEADS):
        ks = slice(kvh * HEAD_DIM, (kvh + 1) * HEAD_DIM)
        vs = slice(KV_COLS + kvh * HEAD_DIM, KV_COLS + (kvh + 1) * HEAD_DIM)
        k = jnp.concatenate([kvp_ref[:, ks], kvc_ref[:, ks]], axis=0).astype(BF16)
        v = jnp.concatenate([kvp_ref[:, vs], kvc_ref[:, vs]], axis=0).astype(BF16)
        v_ext = jnp.concatenate(
            [v, jnp.zeros((2 * w, LANES - HEAD_DIM), BF16), jnp.ones((2 * w, LANES), BF16)],
            axis=1,
        )
        heads = [kvh * GQA_GROUP + g for g in range(GQA_GROUP)]
        qs = jnp.concatenate(
            [q_ref[:, h * HEAD_DIM:(h + 1) * HEAD_DIM] for h in heads], axis=0
        )
        s = lax.dot_general(
            qs, k, (((1,), (1,)), ((), ())), preferred_element_type=F32
        )
        es, sink_terms = [], []
        for g, h in enumerate(heads):
            sink = sink_ref[h] * LOG2_E
            sg = jnp.where(valid, s[g * w:(g + 1) * w], -jnp.inf)
            m = jnp.maximum(jnp.max(sg, axis=-1, keepdims=True), sink)
            es.append(jnp.exp2(sg - m).astype(BF16))
            sink_terms.append(jnp.exp2(sink - m))
        o = jnp.dot(jnp.concatenate(es, axis=0), v_ext, preferred_element_type=F32)
        for g, h in enumerate(heads):
            og = o[g * w:(g + 1) * w]
            out = og[:, :LANES] / (og[:, LANES:] + sink_terms[g])
            o_ref[:, h * HEAD_DIM:(h + 1) * HEAD_DIM] = out[:, :HEAD_DIM].astype(
                o_ref.dtype
            )


def attn_prompt(q, kv, sinks, att_sample, cast, *, m_total, m_prompt, seq, per_step):
    w = WINDOW
    rows = per_step * w
    assert m_prompt % rows == 0 and att_sample.shape[0] % rows == 0
    nsteps = m_prompt // rows
    last = nsteps - 1
    last_prev = m_prompt // w - 1
    src, layer, block_rows = cast
    src_spec, dst_spec, dst_shape, cast_blocks = _side_cast_specs(
        src, layer, block_rows, m_total // rows
    )
    return pl.pallas_call(
        functools.partial(_attn_prompt_kernel, blocks_per_seq=seq // w,
                          n_prompt_steps=nsteps, per_step=per_step,
                          cast_blocks=cast_blocks),
        grid=(m_total // rows,),
        in_specs=[
            pl.BlockSpec(memory_space=pltpu.SMEM),
            pl.BlockSpec((rows, Q_COLS), lambda r: (jnp.minimum(r, last), 0)),
            pl.BlockSpec((rows, 2 * KV_COLS), lambda r: (jnp.minimum(r, last), 0)),
            pl.BlockSpec((w, 2 * KV_COLS),
                         lambda r: (jnp.clip(per_step * r - 1, 0, last_prev), 0)),
            pl.BlockSpec((rows, Q_COLS), lambda r: (jnp.maximum(r - nsteps, 0), 0)),
            src_spec,
        ],
        out_specs=[pl.BlockSpec((rows, Q_COLS), lambda r: (r, 0)), dst_spec],
        out_shape=[jax.ShapeDtypeStruct((m_total, Q_COLS), BF16), dst_shape],
        compiler_params=_params(("arbitrary",), 48),
        name="attn_prompt",
    )(sinks, q, kv, kv, att_sample, src)


def _attn_sample_kernel(sink_ref, q_ref, k_ref, v_ref, o_ref, *, t_new):
    nb, nkv, rows, d = q_ref.shape
    keys = k_ref.shape[2]
    q = q_ref[...].reshape(nb * nkv, rows, d)
    k = k_ref[...].reshape(nb * nkv, keys, d)
    v = v_ref[...].reshape(nb * nkv, keys, d)
    s = jnp.einsum("bqd,bkd->bqk", q, k, preferred_element_type=F32)
    t = lax.broadcasted_iota(jnp.int32, (rows, keys), 0) // GQA_GROUP
    col = lax.broadcasted_iota(jnp.int32, (rows, keys), 1)
    valid = jnp.logical_and(col > t, col <= WINDOW + t)
    s = jnp.where(valid[None], s, -jnp.inf)
    sink = jnp.broadcast_to(sink_ref[...][None], (nb, nkv, rows, 1)).reshape(
        nb * nkv, rows, 1
    )
    p = _softmax2_with_sink(s, sink * LOG2_E).astype(BF16)
    o = jnp.einsum("bqk,bkd->bqd", p, v, preferred_element_type=F32)
    o_ref[...] = o.reshape(nb, nkv, rows, d).astype(o_ref.dtype)


def attn_sample(q_s, k_new, v_new, k_buf, v_buf, sinks, *, nb):
    b, l = k_buf.shape[0], k_buf.shape[1]
    t_new = q_s.shape[0] // b
    rows = t_new * GQA_GROUP
    keys = 2 * WINDOW
    q4 = q_s.reshape(b, t_new, N_KV_HEADS, GQA_GROUP, HEAD_DIM)
    q4 = q4.transpose(0, 2, 1, 3, 4).reshape(b, N_KV_HEADS, rows, HEAD_DIM)

    def keys_layout(buf, new):
        cat = jnp.concatenate(
            [buf, new.reshape(b, t_new, N_KV_HEADS, HEAD_DIM)], axis=1
        )
        cat = jnp.pad(cat, ((0, 0), (0, keys - l - t_new), (0, 0), (0, 0)))
        return cat.transpose(0, 2, 1, 3).astype(BF16)

    k4 = keys_layout(k_buf, k_new)
    v4 = keys_layout(v_buf, v_new)
    sink_rows = jnp.broadcast_to(
        sinks.reshape(N_KV_HEADS, 1, GQA_GROUP), (N_KV_HEADS, t_new, GQA_GROUP)
    ).reshape(N_KV_HEADS, rows, 1)
    o4 = pl.pallas_call(
        functools.partial(_attn_sample_kernel, t_new=t_new),
        grid=(b // nb,),
        in_specs=[
            pl.BlockSpec((N_KV_HEADS, rows, 1), lambda i: (0, 0, 0)),
            pl.BlockSpec((nb, N_KV_HEADS, rows, HEAD_DIM), lambda i: (i, 0, 0, 0)),
            pl.BlockSpec((nb, N_KV_HEADS, keys, HEAD_DIM), lambda i: (i, 0, 0, 0)),
            pl.BlockSpec((nb, N_KV_HEADS, keys, HEAD_DIM), lambda i: (i, 0, 0, 0)),
        ],
        out_specs=pl.BlockSpec(
            (nb, N_KV_HEADS, rows, HEAD_DIM), lambda i: (i, 0, 0, 0)
        ),
        out_shape=jax.ShapeDtypeStruct((b, N_KV_HEADS, rows, HEAD_DIM), BF16),
        compiler_params=_params(("parallel",), 40),
        name="attn_sample",
    )(sink_rows, q4, k4, v4)
    o = o4.reshape(b, N_KV_HEADS, t_new, GQA_GROUP, HEAD_DIM).transpose(0, 2, 1, 3, 4)
    return o.reshape(b * t_new, Q_COLS)


def _place_rows_kernel(rows_ref, buf_ref, o_ref):
    del buf_ref
    o_ref[...] = rows_ref[...]


def place_rows(buf, rows, *, row_block):
    r, c = rows.shape
    return pl.pallas_call(
        _place_rows_kernel,
        grid=(1,),
        in_specs=[
            pl.BlockSpec((r, c), lambda i: (0, 0)),
            pl.BlockSpec(memory_space=pl.ANY),
        ],
        out_specs=pl.BlockSpec((r, c), lambda i: (row_block, 0)),
        out_shape=jax.ShapeDtypeStruct(buf.shape, buf.dtype),
        input_output_aliases={1: 0},
        compiler_params=_params(("arbitrary",), 16),
        name="place_rows",
    )(rows, buf)


def _gmlp_prompt_kernel(u_ref, v_ref, w_ref, bt_ref, gn_ref, os_ref, o_ref, *,
                        n_prompt_steps, per_step):
    @pl.when(pl.program_id(0) >= n_prompt_steps)
    def _():
        o_ref[...] = os_ref[...]

    @pl.when(pl.program_id(0) < n_prompt_steps)
    def _():
        for b in range(per_step):
            rows = pl.ds(b * CHUNK, CHUNK)
            _gmlp_prompt_chunk(u_ref.at[rows], v_ref.at[rows], w_ref, bt_ref, gn_ref,
                               o_ref.at[rows])


def _gmlp_prompt_chunk(u_ref, v_ref, w_ref, bt_ref, gn_ref, o_ref):
    gv = _rms(v_ref[...].astype(F32), gn_ref[...])
    c = CHUNK
    row = lax.broadcasted_iota(jnp.int32, (c, c), 0)
    col = lax.broadcasted_iota(jnp.int32, (c, c), 1)
    causal = col <= row
    for h in range(GM_HEADS):
        cs = slice(h * GM_HEAD_DIM, (h + 1) * GM_HEAD_DIM)
        w = jnp.where(causal, w_ref[h], 0.0).astype(BF16)
        mixed = jnp.dot(w, gv[:, cs].astype(BF16), preferred_element_type=F32)
        mixed = mixed + bt_ref[:, h:h + 1]
        o_ref[:, cs] = (u_ref[:, cs].astype(F32) * mixed).astype(o_ref.dtype)


def gmlp_prompt(gm, w_s, b_t, g_norm, gated_sample, *, m_total, m_prompt, per_step):
    c = CHUNK
    rows = per_step * c
    assert m_prompt % rows == 0 and gated_sample.shape[0] % rows == 0
    nsteps = m_prompt // rows
    last = nsteps - 1
    return pl.pallas_call(
        functools.partial(_gmlp_prompt_kernel, n_prompt_steps=nsteps, per_step=per_step),
        grid=(m_total // rows,),
        in_specs=[
            pl.BlockSpec((rows, GM_WIDTH), lambda r: (jnp.minimum(r, last), 0)),
            pl.BlockSpec((rows, GM_WIDTH), lambda r: (jnp.minimum(r, last), 1)),
            pl.BlockSpec((GM_HEADS, c, c), lambda r: (0, 0, 0)),
            pl.BlockSpec((c, GM_HEADS), lambda r: (0, 0)),
            pl.BlockSpec((1, GM_WIDTH), lambda r: (0, 0)),
            pl.BlockSpec((rows, GM_WIDTH), lambda r: (jnp.maximum(r - nsteps, 0), 0)),
        ],
        out_specs=pl.BlockSpec((rows, GM_WIDTH), lambda r: (r, 0)),
        out_shape=jax.ShapeDtypeStruct((m_total, GM_WIDTH), BF16),
        compiler_params=_params(("arbitrary",), 40),
        name="gmlp_prompt",
    )(gm, gm, w_s, b_t, g_norm, gated_sample)


def _gmlp_sample_kernel(u_ref, v_ref, wc_ref, bc_ref, gn_ref, o_ref, gv_ref, *, t_new):
    rows = u_ref.shape[0]
    nseq = rows // t_new
    gv = _rms(v_ref[...].astype(F32), gn_ref[...])
    gv_ref[...] = gv
    t = lax.broadcasted_iota(jnp.int32, (t_new, GM_HEAD_DIM), 0)
    for h in range(GM_HEADS):
        cs = slice(h * GM_HEAD_DIM, (h + 1) * GM_HEAD_DIM)
        x = gv[:, cs].reshape(nseq, t_new, GM_HEAD_DIM)
        acc = jnp.zeros((nseq, t_new, GM_HEAD_DIM), F32)
        for s in range(t_new):
            coef = jnp.where(t >= s, wc_ref[h, s], 0.0)
            acc = acc + coef[None] * x[:, s:s + 1, :]
        mixed = acc + bc_ref[h][None]
        u = u_ref[:, cs].astype(F32).reshape(nseq, t_new, GM_HEAD_DIM)
        o_ref[:, cs] = (u * mixed).reshape(rows, GM_HEAD_DIM).astype(o_ref.dtype)


def gmlp_sample(gm, w_s, b_s, g_norm, *, m_prompt, m_sample, t_new):
    lanes = GM_HEAD_DIM
    w8 = w_s[:, :t_new, :t_new]
    wc = jnp.broadcast_to(
        w8.transpose(0, 2, 1)[..., None], (GM_HEADS, t_new, t_new, lanes)
    )
    bc = jnp.broadcast_to(b_s[:, :t_new, None], (GM_HEADS, t_new, lanes))
    blk = m_prompt // m_sample
    return pl.pallas_call(
        functools.partial(_gmlp_sample_kernel, t_new=t_new),
        grid=(1,),
        in_specs=[
            pl.BlockSpec((m_sample, GM_WIDTH), lambda i: (blk, 0)),
            pl.BlockSpec((m_sample, GM_WIDTH), lambda i: (blk, 1)),
            pl.BlockSpec((GM_HEADS, t_new, t_new, lanes), lambda i: (0, 0, 0, 0)),
            pl.BlockSpec((GM_HEADS, t_new, lanes), lambda i: (0, 0, 0)),
            pl.BlockSpec((1, GM_WIDTH), lambda i: (0, 0)),
        ],
        out_specs=[
            pl.BlockSpec((m_sample, GM_WIDTH), lambda i: (0, 0)),
            pl.BlockSpec((m_sample, GM_WIDTH), lambda i: (0, 0)),
        ],
        out_shape=[
            jax.ShapeDtypeStruct((m_sample, GM_WIDTH), BF16),
            jax.ShapeDtypeStruct((m_sample, GM_WIDTH), F32),
        ],
        compiler_params=_params(("arbitrary",), 40),
        name="gmlp_sample",
    )(gm, gm, wc, bc, g_norm)


def _conv_gelu_gate(g, g1, g2, up, cw_ref, cb_ref):
    c = cb_ref[...] + cw_ref[0:1, :] * g2
    c = c + cw_ref[1:2, :] * g1
    c = c + cw_ref[2:3, :] * g
    return _gelu_times(c, up)


def _ffn_a_kernel(h_ref, hs_ref, wg_ref, wu_ref, cw_ref, cb_ref, st_ref, wd_ref, a_ref,
                  tail_ref, as_ref, gs_ref, wdb_ref, *, seq, chunk, t_new, cast_blocks):
    i = pl.program_id(0)
    _side_cast(wd_ref, wdb_ref, cast_blocks)
    tm = h_ref.shape[0]
    tf = wg_ref.shape[1]
    wg = wg_ref[...]
    wu = wu_ref[...]

    for r0 in range(0, tm, chunk):
        if r0 % seq == 0:
            prev8 = jnp.zeros((8, tf), F32)
        h = h_ref[r0:r0 + chunk, :]
        g = jnp.dot(h, wg, preferred_element_type=F32)
        up = jnp.dot(h, wu, preferred_element_type=F32)
        ext = jnp.concatenate([prev8, g], axis=0)
        g1 = pltpu.roll(ext, 1, 0)[8:]
        g2 = pltpu.roll(ext, 2, 0)[8:]
        a_ref[r0:r0 + chunk, :] = _conv_gelu_gate(g, g1, g2, up, cw_ref, cb_ref).astype(
            a_ref.dtype
        )
        prev8 = g[chunk - 8:, :]
        if (r0 + chunk) % seq == 0:
            tail_ref[r0 // seq] = prev8

    @pl.when(i == pl.num_programs(0) - 1)
    def _():
        hs = hs_ref[...]
        rows = hs.shape[0]
        g = jnp.dot(hs, wg, preferred_element_type=F32)
        up = jnp.dot(hs, wu, preferred_element_type=F32)
        gs_ref[...] = g
        t = lax.broadcasted_iota(jnp.int32, g.shape, 0) % t_new
        st2 = st_ref[...]
        st1 = pltpu.roll(st2, rows - 1, 0)
        g1 = jnp.where(t >= 1, pltpu.roll(g, 1, 0), st1)
        g2 = jnp.where(t >= 2, pltpu.roll(g, 2, 0), st2)
        as_ref[...] = _conv_gelu_gate(g, g1, g2, up, cw_ref, cb_ref).astype(as_ref.dtype)


def ffn_a(h, w_gate_b, w_up_b, w_down, conv_w, conv_b, conv_state, layer, *, m_prompt,
          m_sample, seq, seqs_per_tile, t_new, tf, chunk, cast_rows):
    m_total, d = h.shape
    f = w_gate_b.shape[-1]
    nf = f // tf
    tm = seq * seqs_per_tile
    ni = m_prompt // tm
    assert t_new % 8 == 0 and t_new >= CONV_W - 1 and seq % chunk == 0
    st = jnp.pad(conv_state, ((0, 0), (0, t_new - (CONV_W - 1)), (0, 0)))
    st = st.reshape(m_sample, f)
    sample_col = lambda i, j: (0, jnp.where(i == ni - 1, j, 0))
    wd_in, wd_out, wd_shape, cast_blocks = _side_cast_specs(
        w_down, layer, cast_rows, ni * nf, nf
    )
    once = pl.Buffered(1)
    return pl.pallas_call(
        functools.partial(_ffn_a_kernel, seq=seq, chunk=chunk, t_new=t_new,
                          cast_blocks=cast_blocks),
        grid=(ni, nf),
        in_specs=[
            pl.BlockSpec((tm, d), lambda i, j: (i, 0), pipeline_mode=once),
            pl.BlockSpec((m_sample, d), lambda i, j: (m_prompt // m_sample, 0),
                         pipeline_mode=once),
            pl.BlockSpec((d, tf), lambda i, j: (0, j)),
            pl.BlockSpec((d, tf), lambda i, j: (0, j)),
            pl.BlockSpec((None, CONV_W, tf), lambda i, j: (layer, 0, j)),
            pl.BlockSpec((None, 1, tf), lambda i, j: (layer, 0, j)),
            pl.BlockSpec((m_sample, tf), lambda i, j: (0, j)),
            wd_in,
        ],
        out_specs=[
            pl.BlockSpec((tm, tf), lambda i, j: (i, j)),
            pl.BlockSpec((seqs_per_tile, 8, tf), lambda i, j: (i, 0, j)),
            pl.BlockSpec((m_sample, tf), sample_col),
            pl.BlockSpec((m_sample, tf), sample_col),
            wd_out,
        ],
        out_shape=[
            jax.ShapeDtypeStruct((m_total, f), BF16),
            jax.ShapeDtypeStruct((m_prompt // seq, 8, f), F32),
            jax.ShapeDtypeStruct((m_sample, f), BF16),
            jax.ShapeDtypeStruct((m_sample, f), F32),
            wd_shape,
        ],
        compiler_params=_params(("arbitrary", "arbitrary"), 58),
        name="ffn_a",
    )(h, h, w_gate_b, w_up_b, conv_w, conv_b, st, w_down)


def kernel(x_prompt, x_sample, state_swa_k, state_swa_v, state_conv, w_in, w_out,
           attn_sinks, gm_spatial, gm_bias, gm_norm, norm_pre_mix, norm_post_mix,
           norm_pre_ffn, norm_post_ffn, w_ffn_gate, w_ffn_up, w_ffn_down, conv_w,
           conv_b):
    batch, seq, d = x_prompt.shape
    dec_batch, dec_seq, _ = x_sample.shape
    depth = w_in.shape[0]
    f = w_ffn_gate.shape[-1]
    mp = batch * seq
    ms = dec_batch * dec_seq
    m = mp + ms
    keep = min(WINDOW, seq)

    tm_mm = m // 8
    tm_down = m // 12
    tf = 256
    tm_rows = ms
    assert mp % tm_rows == 0

    w_in_b = w_in[0].astype(BF16)[None]
    conv_b3 = conv_b.reshape(depth, 1, f)
    cast_rows = 64

    x = (x_prompt.reshape(mp, d), x_sample.reshape(ms, d))
    h = rms_rows(x, norm_pre_mix[0].reshape(1, d), tm=tm_rows)

    p_k, p_v, p_c, s_k, s_v, s_c, s_g = [], [], [], [], [], [], []
    for l in range(depth):
        last = l + 1 == depth
        tn = 512
        q, _ = matmul_cols(h, w_in_b, 0, col_block0=0, n=Q_COLS, tm=tm_mm, tn=2 * tn,
                           out_dtype=BF16, epilogue=_scale_q, name="mm_q")
        kv, _ = matmul_cols(h, w_in_b, 0, col_block0=Q_COLS // tn, n=2 * KV_COLS,
                            tm=tm_mm, tn=tn, out_dtype=F32, epilogue=_identity,
                            name="mm_kv")
        gm, (w_out_b, w_gate_b) = matmul_cols(
            h, w_in_b, 0, col_block0=(Q_COLS + 2 * KV_COLS) // tn, n=2 * GM_WIDTH,
            tm=tm_mm, tn=tn, out_dtype=BF16, epilogue=_gelu, name="mm_gm",
            casts=[(w_out, l, cast_rows), (w_ffn_gate, l, cast_rows)])
        w_out_b = w_out_b.reshape(1, 2, Q_COLS, d)

        kv_p = jnp.stack([kv[(b + 1) * seq - keep:(b + 1) * seq] for b in range(batch)])
        kv_s = kv[mp:]
        k_s, v_s = kv_s[:, :KV_COLS], kv_s[:, KV_COLS:]
        p_k.append(kv_p[..., :KV_COLS].reshape(batch, keep, N_KV_HEADS, HEAD_DIM))
        p_v.append(kv_p[..., KV_COLS:].reshape(batch, keep, N_KV_HEADS, HEAD_DIM))
        s_k.append(k_s.reshape(dec_batch, dec_seq, N_KV_HEADS, HEAD_DIM))
        s_v.append(v_s.reshape(dec_batch, dec_seq, N_KV_HEADS, HEAD_DIM))

        att_s = attn_sample(q[mp:], k_s, v_s, state_swa_k[l], state_swa_v[l],
                            attn_sinks[l], nb=8)
        att, w_up_b = attn_prompt(q, kv, attn_sinks[l], att_s,
                                  (w_ffn_up, l, 2 * cast_rows), m_total=m, m_prompt=mp,
                                  seq=seq, per_step=2)

        gmo_s, gv_s = gmlp_sample(gm, gm_spatial[l], gm_bias[l], gm_norm[l].reshape(1, -1),
                                  m_prompt=mp, m_sample=ms, t_new=dec_seq)
        gmo = gmlp_prompt(gm, gm_spatial[l], gm_bias[l].T, gm_norm[l].reshape(1, -1),
                          gmo_s, m_total=m, m_prompt=mp, per_step=2)
        s_g.append(gv_s.reshape(dec_batch, dec_seq, GM_HEADS, GM_HEAD_DIM))

        mix, cast_out = matmul_two(
            att, gmo, w_out_b, 0, tm=tm_mm, tn=1024, out_dtype=BF16, name="mm_out",
            casts=[] if last else [(w_in, l + 1, 2 * cast_rows)])
        x, h = post_rows(mix, x, norm_post_mix[l].reshape(1, d),
                         norm_pre_ffn[l].reshape(1, d), tm=tm_rows)

        a, tail, a_s, g_s, w_down_b = ffn_a(
            h, w_gate_b, w_up_b, w_ffn_down, conv_w, conv_b3, state_conv[l], l,
            m_prompt=mp, m_sample=ms, seq=seq, seqs_per_tile=2, t_new=dec_seq, tf=tf,
            chunk=512, cast_rows=2 * cast_rows)
        a = place_rows(a, a_s, row_block=mp // ms)
        p_c.append(tail[:, 8 - (CONV_W - 1):])
        s_c.append(g_s.reshape(dec_batch, dec_seq, f)[:, dec_seq - (CONV_W - 1):])

        ffn, _ = matmul_cols(a, w_down_b[None], 0, col_block0=0, n=d, tm=tm_down, tn=512,
                             out_dtype=BF16, epilogue=_identity, name="mm_down",
                             vmem_mib=58)
        x, h = post_rows(ffn, x, norm_post_ffn[l].reshape(1, d),
                         None if last else norm_pre_mix[l + 1].reshape(1, d),
                         tm=tm_rows, split_out=(mp, ms) if last else None)
        if not last:
            w_in_b = cast_out[0][None]

    x_p, x_s = x
    return (x_p.reshape(batch, seq, d), x_s.reshape(dec_batch, dec_seq, d),
            jnp.stack(p_k), jnp.stack(p_v), jnp.stack(p_c),
            jnp.stack(s_k), jnp.stack(s_v), jnp.stack(s_c), jnp.stack(s_g))
```

```python
import functools
import math

import jax
import jax.numpy as jnp
from jax import lax
from jax.experimental import pallas as pl
from jax.experimental.pallas import tpu as pltpu

F32 = jnp.float32
BF16 = jnp.bfloat16

D_MODEL = 4096
HEAD_DIM = 64
N_HEADS = 32
N_KV_HEADS = 4
GQA_GROUP = 8
WINDOW = 128
CHUNK = 128
GM_HEAD_DIM = 128
GM_HEADS = 16
GM_WIDTH = 2048
Q_COLS = 2048
KV_COLS = 256
CONV_W = 3
EPS = 1e-6
LANES = 128
LOG2_E = math.log2(math.e)

MIB = 1024 * 1024


def _params(semantics, vmem_mib):
    return pltpu.CompilerParams(
        dimension_semantics=semantics, vmem_limit_bytes=vmem_mib * MIB
    )


def _rms(x, g):
    ms = jnp.mean(x * x, axis=-1, keepdims=True)
    return x * lax.rsqrt(ms + EPS) * g


def _split_specs(tm, d, n_prompt_tiles):
    return [
        pl.BlockSpec((tm, d), lambda i: (jnp.minimum(i, n_prompt_tiles - 1), 0)),
        pl.BlockSpec((tm, d), lambda i: (0, 0)),
    ]


def _load_rows(refs, n_prompt_tiles):
    if len(refs) == 1:
        return refs[0][...]
    is_prompt = pl.program_id(0) < n_prompt_tiles
    return jnp.where(is_prompt, refs[0][...], refs[1][...])


def _store_rows(refs, value, n_prompt_tiles):
    if len(refs) == 1:
        refs[0][...] = value
        return

    @pl.when(pl.program_id(0) < n_prompt_tiles)
    def _():
        refs[0][...] = value

    @pl.when(pl.program_id(0) >= n_prompt_tiles)
    def _():
        refs[1][...] = value


def _rms_kernel(*refs, n_in, n_prompt_tiles):
    x = _load_rows(refs[:n_in], n_prompt_tiles)
    g_ref, o_ref = refs[n_in:]
    o_ref[...] = _rms(x, g_ref[...]).astype(o_ref.dtype)


def rms_rows(x_parts, g, *, tm):
    d = x_parts[0].shape[1]
    m = sum(p.shape[0] for p in x_parts)
    npt = x_parts[0].shape[0] // tm
    row = pl.BlockSpec((tm, d), lambda i: (i, 0))
    x_specs = [row] if len(x_parts) == 1 else _split_specs(tm, d, npt)
    return pl.pallas_call(
        functools.partial(_rms_kernel, n_in=len(x_parts), n_prompt_tiles=npt),
        grid=(m // tm,),
        in_specs=x_specs + [pl.BlockSpec((1, d), lambda i: (0, 0))],
        out_specs=row,
        out_shape=jax.ShapeDtypeStruct((m, d), BF16),
        compiler_params=_params(("arbitrary",), 40),
        name="rms_rows",
    )(*x_parts, g)


def _post_kernel(*refs, n_in, n_out, has_next, n_prompt_tiles):
    m_ref = refs[0]
    x = _load_rows(refs[1:1 + n_in], n_prompt_tiles)
    rest = refs[1 + n_in:]
    xn = x + _rms(m_ref[...].astype(F32), rest[0][...])
    rest = rest[1:]
    if has_next:
        gnext_ref, rest = rest[0], rest[1:]
    _store_rows(rest[:n_out], xn, n_prompt_tiles)
    if has_next:
        ho_ref = rest[n_out]
        ho_ref[...] = _rms(xn, gnext_ref[...]).astype(ho_ref.dtype)


def post_rows(m_in, x_parts, g_post, g_next, *, tm, split_out=None):
    m, d = m_in.shape
    npt = (x_parts[0].shape[0] if split_out is None else split_out[0]) // tm
    if len(x_parts) == 1 and split_out is None:
        npt = m // tm
    row = pl.BlockSpec((tm, d), lambda i: (i, 0))
    vec = pl.BlockSpec((1, d), lambda i: (0, 0))
    x_specs = [row] if len(x_parts) == 1 else _split_specs(tm, d, npt)
    if split_out is None:
        xo_specs = [row]
        xo_shapes = [jax.ShapeDtypeStruct((m, d), F32)]
    else:
        xo_specs = _split_specs(tm, d, npt)
        xo_shapes = [jax.ShapeDtypeStruct((r, d), F32) for r in split_out]
    has_next = g_next is not None
    gains = [g_post, g_next] if has_next else [g_post]
    out = pl.pallas_call(
        functools.partial(_post_kernel, n_in=len(x_parts), n_out=len(xo_specs),
                          has_next=has_next, n_prompt_tiles=npt),
        grid=(m // tm,),
        in_specs=[row] + x_specs + [vec] * len(gains),
        out_specs=xo_specs + ([row] if has_next else []),
        out_shape=xo_shapes
        + ([jax.ShapeDtypeStruct((m, d), BF16)] if has_next else []),
        compiler_params=_params(("arbitrary",), 48),
        name="post_rows",
    )(m_in, *x_parts, *gains)
    x_new = out[:len(xo_specs)]
    return (x_new, out[-1]) if has_next else (x_new, None)


def _grid_step():
    return pl.program_id(0) * pl.num_programs(1) + pl.program_id(1)


def _side_cast(src_ref, dst_ref, n_blocks):
    @pl.when(_grid_step() < n_blocks)
    def _():
        dst_ref[...] = src_ref[...].astype(dst_ref.dtype)


def _side_cast_specs(src, layer, block_rows, n_steps, n_cols_grid):
    _, rows, cols = src.shape
    n_blocks = rows // block_rows
    assert rows % block_rows == 0 and n_blocks <= n_steps

    def block(i, j):
        return jnp.minimum(i * n_cols_grid + j, n_blocks - 1)

    in_spec = pl.BlockSpec((None, block_rows, cols), lambda i, j: (layer, block(i, j), 0))
    out_spec = pl.BlockSpec((block_rows, cols), lambda i, j: (block(i, j), 0))
    return in_spec, out_spec, jax.ShapeDtypeStruct((rows, cols), BF16), n_blocks


def _split_refs(rest, cast_blocks):
    n = len(cast_blocks)
    return rest[n], list(zip(rest[:n], rest[n + 1:], cast_blocks))


def _mm_kernel(a_ref, b_ref, *rest, epilogue, cast_blocks):
    o_ref, casts = _split_refs(rest, cast_blocks)
    acc = jnp.dot(a_ref[...], b_ref[...], preferred_element_type=F32)
    o_ref[...] = epilogue(acc).astype(o_ref.dtype)
    for src_ref, dst_ref, n_blocks in casts:
        _side_cast(src_ref, dst_ref, n_blocks)


def _mm2_kernel(a1_ref, a2_ref, b1_ref, b2_ref, *rest, cast_blocks):
    o_ref, casts = _split_refs(rest, cast_blocks)
    acc = jnp.dot(a1_ref[...], b1_ref[...], preferred_element_type=F32)
    acc = acc + jnp.dot(a2_ref[...], b2_ref[...], preferred_element_type=F32)
    o_ref[...] = acc.astype(o_ref.dtype)
    for src_ref, dst_ref, n_blocks in casts:
        _side_cast(src_ref, dst_ref, n_blocks)


def _with_side_casts(casts, grid, in_specs, out_specs, out_shapes, operands):
    counts = []
    for src, layer, block_rows in casts:
        i_spec, o_spec, o_shape, n_blocks = _side_cast_specs(
            src, layer, block_rows, grid[0] * grid[1], grid[1]
        )
        in_specs.append(i_spec)
        out_specs.append(o_spec)
        out_shapes.append(o_shape)
        operands.append(src)
        counts.append(n_blocks)
    return tuple(counts)


def matmul_cols(a, w, layer, *, col_block0, n, tm, tn, out_dtype, epilogue, name,
                casts=(), vmem_mib=56):
    m, k = a.shape
    grid = (m // tm, n // tn)
    in_specs = [
        pl.BlockSpec((tm, k), lambda i, j: (i, 0)),
        pl.BlockSpec((None, k, tn), lambda i, j: (layer, 0, j + col_block0)),
    ]
    out_specs = [pl.BlockSpec((tm, tn), lambda i, j: (i, j))]
    out_shapes = [jax.ShapeDtypeStruct((m, n), out_dtype)]
    operands = [a, w]
    cast_blocks = _with_side_casts(casts, grid, in_specs, out_specs, out_shapes, operands)
    out = pl.pallas_call(
        functools.partial(_mm_kernel, epilogue=epilogue, cast_blocks=cast_blocks),
        grid=grid,
        in_specs=in_specs,
        out_specs=out_specs,
        out_shape=out_shapes,
        compiler_params=_params(("arbitrary", "arbitrary"), vmem_mib),
        name=name,
    )(*operands)
    return out[0], list(out[1:])


def matmul_two(a1, a2, w, layer, *, tm, tn, out_dtype, name, casts=()):
    m, k1 = a1.shape
    k2 = a2.shape[1]
    assert k1 == k2
    n = w.shape[-1]
    grid = (m // tm, n // tn)
    in_specs = [
        pl.BlockSpec((tm, k1), lambda i, j: (i, 0)),
        pl.BlockSpec((tm, k2), lambda i, j: (i, 0)),
        pl.BlockSpec((None, None, k1, tn), lambda i, j: (layer, 0, 0, j)),
        pl.BlockSpec((None, None, k2, tn), lambda i, j: (layer, 1, 0, j)),
    ]
    out_specs = [pl.BlockSpec((tm, tn), lambda i, j: (i, j))]
    out_shapes = [jax.ShapeDtypeStruct((m, n), out_dtype)]
    operands = [a1, a2, w, w]
    cast_blocks = _with_side_casts(casts, grid, in_specs, out_specs, out_shapes, operands)
    out = pl.pallas_call(
        functools.partial(_mm2_kernel, cast_blocks=cast_blocks),
        grid=grid,
        in_specs=in_specs,
        out_specs=out_specs,
        out_shape=out_shapes,
        compiler_params=_params(("arbitrary", "arbitrary"), 56),
        name=name,
    )(*operands)
    return out[0], list(out[1:])


def _identity(x):
    return x


def _scale_q(x):
    return x * (HEAD_DIM ** -0.5 * LOG2_E)


_GELU_A = -2.0 * LOG2_E * math.sqrt(2.0 / math.pi)
_GELU_B = _GELU_A * 0.044715


def _gelu_times(x, y):
    t = x * x * _GELU_B + _GELU_A
    return (x * y) / (1.0 + jnp.exp2(t * x))


def _gelu(x):
    t = x * x * _GELU_B + _GELU_A
    return x / (1.0 + jnp.exp2(t * x))


def _softmax2_with_sink(s, sink):
    m = jnp.maximum(jnp.max(s, axis=-1, keepdims=True), sink)
    e = jnp.exp2(s - m)
    den = jnp.sum(e, axis=-1, keepdims=True) + jnp.exp2(sink - m)
    return e / den


def _attn_prompt_kernel(sink_ref, q_ref, kvc_ref, kvp_ref, os_ref, o_ref, *,
                        blocks_per_seq, n_prompt_steps, per_step):
    step = pl.program_id(0)
    w = WINDOW

    @pl.when(step >= n_prompt_steps)
    def _():
        o_ref[...] = os_ref[...]

    @pl.when(step < n_prompt_steps)
    def _():
        for b in range(per_step):
            rows = pl.ds(b * w, w)
            prev_ref = kvp_ref if b == 0 else kvc_ref.at[pl.ds((b - 1) * w, w)]
            _attn_prompt_block(sink_ref, q_ref.at[rows], kvc_ref.at[rows], prev_ref,
                               o_ref.at[rows], (step * per_step + b) % blocks_per_seq)


def _attn_prompt_block(sink_ref, q_ref, kvc_ref, kvp_ref, o_ref, n):
    w = WINDOW
    row = lax.broadcasted_iota(jnp.int32, (w, 2 * w), 0)
    col = lax.broadcasted_iota(jnp.int32, (w, 2 * w), 1)
    prev_ok = jnp.logical_and(jnp.logical_and(col < w, col > row), n > 0)
    own_ok = jnp.logical_and(col >= w, col - w <= row)
    valid = jnp.logical_or(prev_ok, own_ok)
    for kvh in range(N_KV_HEADS):
        ks = slice(kvh * HEAD_DIM, (kvh + 1) * HEAD_DIM)
        vs = slice(KV_COLS + kvh * HEAD_DIM, KV_COLS + (kvh + 1) * HEAD_DIM)
        k = jnp.concatenate([kvp_ref[:, ks], kvc_ref[:, ks]], axis=0).astype(BF16)
        v = jnp.concatenate([kvp_ref[:, vs], kvc_ref[:, vs]], axis=0).astype(BF16)
        v_ext = jnp.concatenate(
            [v, jnp.zeros((2 * w, LANES - HEAD_DIM), BF16), jnp.ones((2 * w, LANES), BF16)],
            axis=1,
        )
        heads = [kvh * GQA_GROUP + g for g in range(GQA_GROUP)]
        qs = jnp.concatenate(
            [q_ref[:, h * HEAD_DIM:(h + 1) * HEAD_DIM] for h in heads], axis=0
        )
        s = lax.dot_general(
            qs, k, (((1,), (1,)), ((), ())), preferred_element_type=F32
        )
        es, sink_terms = [], []
        for g, h in enumerate(heads):
            sink = sink_ref[h] * LOG2_E
            sg = jnp.where(valid, s[g * w:(g + 1) * w], -jnp.inf)
            m = jnp.maximum(jnp.max(sg, axis=-1, keepdims=True), sink)
            es.append(jnp.exp2(sg - m).astype(BF16))
            sink_terms.append(jnp.exp2(sink - m))
        o = jnp.dot(jnp.concatenate(es, axis=0), v_ext, preferred_element_type=F32)
        for g, h in enumerate(heads):
            og = o[g * w:(g + 1) * w]
            out = og[:, :LANES] / (og[:, LANES:] + sink_terms[g])
            o_ref[:, h * HEAD_DIM:(h + 1) * HEAD_DIM] = out[:, :HEAD_DIM].astype(
                o_ref.dtype
            )


def attn_prompt(q, kv, sinks, att_sample, *, m_total, m_prompt, seq, per_step):
    w = WINDOW
    rows = per_step * w
    assert m_prompt % rows == 0 and att_sample.shape[0] % rows == 0
    nsteps = m_prompt // rows
    last = nsteps - 1
    last_prev = m_prompt // w - 1
    return pl.pallas_call(
        functools.partial(_attn_prompt_kernel, blocks_per_seq=seq // w,
                          n_prompt_steps=nsteps, per_step=per_step),
        grid=(m_total // rows,),
        in_specs=[
            pl.BlockSpec(memory_space=pltpu.SMEM),
            pl.BlockSpec((rows, Q_COLS), lambda r: (jnp.minimum(r, last), 0)),
            pl.BlockSpec((rows, 2 * KV_COLS), lambda r: (jnp.minimum(r, last), 0)),
            pl.BlockSpec((w, 2 * KV_COLS),
                         lambda r: (jnp.clip(per_step * r - 1, 0, last_prev), 0)),
            pl.BlockSpec((rows, Q_COLS), lambda r: (jnp.maximum(r - nsteps, 0), 0)),
        ],
        out_specs=pl.BlockSpec((rows, Q_COLS), lambda r: (r, 0)),
        out_shape=jax.ShapeDtypeStruct((m_total, Q_COLS), BF16),
        compiler_params=_params(("arbitrary",), 40),
        name="attn_prompt",
    )(sinks, q, kv, kv, att_sample)


def _attn_sample_kernel(sink_ref, q_ref, k_ref, v_ref, o_ref, *, t_new):
    nb, nkv, rows, d = q_ref.shape
    keys = k_ref.shape[2]
    q = q_ref[...].reshape(nb * nkv, rows, d)
    k = k_ref[...].reshape(nb * nkv, keys, d)
    v = v_ref[...].reshape(nb * nkv, keys, d)
    s = jnp.einsum("bqd,bkd->bqk", q, k, preferred_element_type=F32)
    t = lax.broadcasted_iota(jnp.int32, (rows, keys), 0) // GQA_GROUP
    col = lax.broadcasted_iota(jnp.int32, (rows, keys), 1)
    valid = jnp.logical_and(col > t, col <= WINDOW + t)
    s = jnp.where(valid[None], s, -jnp.inf)
    sink = jnp.broadcast_to(sink_ref[...][None], (nb, nkv, rows, 1)).reshape(
        nb * nkv, rows, 1
    )
    p = _softmax2_with_sink(s, sink * LOG2_E).astype(BF16)
    o = jnp.einsum("bqk,bkd->bqd", p, v, preferred_element_type=F32)
    o_ref[...] = o.reshape(nb, nkv, rows, d).astype(o_ref.dtype)


def attn_sample(q_s, k_new, v_new, k_buf, v_buf, sinks, *, nb):
    b, l = k_buf.shape[0], k_buf.shape[1]
    t_new = q_s.shape[0] // b
    rows = t_new * GQA_GROUP
    keys = 2 * WINDOW
    q4 = q_s.reshape(b, t_new, N_KV_HEADS, GQA_GROUP, HEAD_DIM)
    q4 = q4.transpose(0, 2, 1, 3, 4).reshape(b, N_KV_HEADS, rows, HEAD_DIM)

    def keys_layout(buf, new):
        cat = jnp.concatenate(
            [buf, new.reshape(b, t_new, N_KV_HEADS, HEAD_DIM)], axis=1
        )
        cat = jnp.pad(cat, ((0, 0), (0, keys - l - t_new), (0, 0), (0, 0)))
        return cat.transpose(0, 2, 1, 3).astype(BF16)

    k4 = keys_layout(k_buf, k_new)
    v4 = keys_layout(v_buf, v_new)
    sink_rows = jnp.broadcast_to(
        sinks.reshape(N_KV_HEADS, 1, GQA_GROUP), (N_KV_HEADS, t_new, GQA_GROUP)
    ).reshape(N_KV_HEADS, rows, 1)
    o4 = pl.pallas_call(
        functools.partial(_attn_sample_kernel, t_new=t_new),
        grid=(b // nb,),
        in_specs=[
            pl.BlockSpec((N_KV_HEADS, rows, 1), lambda i: (0, 0, 0)),
            pl.BlockSpec((nb, N_KV_HEADS, rows, HEAD_DIM), lambda i: (i, 0, 0, 0)),
            pl.BlockSpec((nb, N_KV_HEADS, keys, HEAD_DIM), lambda i: (i, 0, 0, 0)),
            pl.BlockSpec((nb, N_KV_HEADS, keys, HEAD_DIM), lambda i: (i, 0, 0, 0)),
        ],
        out_specs=pl.BlockSpec(
            (nb, N_KV_HEADS, rows, HEAD_DIM), lambda i: (i, 0, 0, 0)
        ),
        out_shape=jax.ShapeDtypeStruct((b, N_KV_HEADS, rows, HEAD_DIM), BF16),
        compiler_params=_params(("parallel",), 40),
        name="attn_sample",
    )(sink_rows, q4, k4, v4)
    o = o4.reshape(b, N_KV_HEADS, t_new, GQA_GROUP, HEAD_DIM).transpose(0, 2, 1, 3, 4)
    return o.reshape(b * t_new, Q_COLS)


def _place_rows_kernel(rows_ref, buf_ref, o_ref):
    del buf_ref
    o_ref[...] = rows_ref[...]


def place_rows(buf, rows, *, row_block):
    r, c = rows.shape
    return pl.pallas_call(
        _place_rows_kernel,
        grid=(1,),
        in_specs=[
            pl.BlockSpec((r, c), lambda i: (0, 0)),
            pl.BlockSpec(memory_space=pl.ANY),
        ],
        out_specs=pl.BlockSpec((r, c), lambda i: (row_block, 0)),
        out_shape=jax.ShapeDtypeStruct(buf.shape, buf.dtype),
        input_output_aliases={1: 0},
        compiler_params=_params(("arbitrary",), 16),
        name="place_rows",
    )(rows, buf)


def _gmlp_prompt_kernel(u_ref, v_ref, w_ref, bt_ref, gn_ref, os_ref, o_ref, *,
                        n_prompt_steps, per_step):
    @pl.when(pl.program_id(0) >= n_prompt_steps)
    def _():
        o_ref[...] = os_ref[...]

    @pl.when(pl.program_id(0) < n_prompt_steps)
    def _():
        for b in range(per_step):
            rows = pl.ds(b * CHUNK, CHUNK)
            _gmlp_prompt_chunk(u_ref.at[rows], v_ref.at[rows], w_ref, bt_ref, gn_ref,
                               o_ref.at[rows])


def _gmlp_prompt_chunk(u_ref, v_ref, w_ref, bt_ref, gn_ref, o_ref):
    gv = _rms(v_ref[...].astype(F32), gn_ref[...])
    c = CHUNK
    row = lax.broadcasted_iota(jnp.int32, (c, c), 0)
    col = lax.broadcasted_iota(jnp.int32, (c, c), 1)
    causal = col <= row
    for h in range(GM_HEADS):
        cs = slice(h * GM_HEAD_DIM, (h + 1) * GM_HEAD_DIM)
        w = jnp.where(causal, w_ref[h], 0.0).astype(BF16)
        mixed = jnp.dot(w, gv[:, cs].astype(BF16), preferred_element_type=F32)
        mixed = mixed + bt_ref[:, h:h + 1]
        o_ref[:, cs] = (u_ref[:, cs].astype(F32) * mixed).astype(o_ref.dtype)


def gmlp_prompt(gm, w_s, b_t, g_norm, gated_sample, *, m_total, m_prompt, per_step):
    c = CHUNK
    rows = per_step * c
    assert m_prompt % rows == 0 and gated_sample.shape[0] % rows == 0
    nsteps = m_prompt // rows
    last = nsteps - 1
    return pl.pallas_call(
        functools.partial(_gmlp_prompt_kernel, n_prompt_steps=nsteps, per_step=per_step),
        grid=(m_total // rows,),
        in_specs=[
            pl.BlockSpec((rows, GM_WIDTH), lambda r: (jnp.minimum(r, last), 0)),
            pl.BlockSpec((rows, GM_WIDTH), lambda r: (jnp.minimum(r, last), 1)),
            pl.BlockSpec((GM_HEADS, c, c), lambda r: (0, 0, 0)),
            pl.BlockSpec((c, GM_HEADS), lambda r: (0, 0)),
            pl.BlockSpec((1, GM_WIDTH), lambda r: (0, 0)),
            pl.BlockSpec((rows, GM_WIDTH), lambda r: (jnp.maximum(r - nsteps, 0), 0)),
        ],
        out_specs=pl.BlockSpec((rows, GM_WIDTH), lambda r: (r, 0)),
        out_shape=jax.ShapeDtypeStruct((m_total, GM_WIDTH), BF16),
        compiler_params=_params(("arbitrary",), 40),
        name="gmlp_prompt",
    )(gm, gm, w_s, b_t, g_norm, gated_sample)


def _gmlp_sample_kernel(u_ref, v_ref, wc_ref, bc_ref, gn_ref, o_ref, gv_ref, *, t_new):
    rows = u_ref.shape[0]
    nseq = rows // t_new
    gv = _rms(v_ref[...].astype(F32), gn_ref[...])
    gv_ref[...] = gv
    t = lax.broadcasted_iota(jnp.int32, (t_new, GM_HEAD_DIM), 0)
    for h in range(GM_HEADS):
        cs = slice(h * GM_HEAD_DIM, (h + 1) * GM_HEAD_DIM)
        x = gv[:, cs].reshape(nseq, t_new, GM_HEAD_DIM)
        acc = jnp.zeros((nseq, t_new, GM_HEAD_DIM), F32)
        for s in range(t_new):
            coef = jnp.where(t >= s, wc_ref[h, s], 0.0)
            acc = acc + coef[None] * x[:, s:s + 1, :]
        mixed = acc + bc_ref[h][None]
        u = u_ref[:, cs].astype(F32).reshape(nseq, t_new, GM_HEAD_DIM)
        o_ref[:, cs] = (u * mixed).reshape(rows, GM_HEAD_DIM).astype(o_ref.dtype)


def gmlp_sample(gm, w_s, b_s, g_norm, *, m_prompt, m_sample, t_new):
    lanes = GM_HEAD_DIM
    w8 = w_s[:, :t_new, :t_new]
    wc = jnp.broadcast_to(
        w8.transpose(0, 2, 1)[..., None], (GM_HEADS, t_new, t_new, lanes)
    )
    bc = jnp.broadcast_to(b_s[:, :t_new, None], (GM_HEADS, t_new, lanes))
    blk = m_prompt // m_sample
    return pl.pallas_call(
        functools.partial(_gmlp_sample_kernel, t_new=t_new),
        grid=(1,),
        in_specs=[
            pl.BlockSpec((m_sample, GM_WIDTH), lambda i: (blk, 0)),
            pl.BlockSpec((m_sample, GM_WIDTH), lambda i: (blk, 1)),
            pl.BlockSpec((GM_HEADS, t_new, t_new, lanes), lambda i: (0, 0, 0, 0)),
            pl.BlockSpec((GM_HEADS, t_new, lanes), lambda i: (0, 0, 0)),
            pl.BlockSpec((1, GM_WIDTH), lambda i: (0, 0)),
        ],
        out_specs=[
            pl.BlockSpec((m_sample, GM_WIDTH), lambda i: (0, 0)),
            pl.BlockSpec((m_sample, GM_WIDTH), lambda i: (0, 0)),
        ],
        out_shape=[
            jax.ShapeDtypeStruct((m_sample, GM_WIDTH), BF16),
            jax.ShapeDtypeStruct((m_sample, GM_WIDTH), F32),
        ],
        compiler_params=_params(("arbitrary",), 40),
        name="gmlp_sample",
    )(gm, gm, wc, bc, g_norm)


def _conv_gelu_gate(g, g1, g2, up, cw_ref, cb_ref):
    c = cb_ref[...] + cw_ref[0:1, :] * g2
    c = c + cw_ref[1:2, :] * g1
    c = c + cw_ref[2:3, :] * g
    return _gelu_times(c, up)


def _ffn_a_kernel(h_ref, hs_ref, wg_ref, wu_ref, cw_ref, cb_ref, st_ref, wd_ref, a_ref,
                  tail_ref, as_ref, gs_ref, wdb_ref, *, chunk, t_new, cast_blocks):
    i = pl.program_id(0)
    _side_cast(wd_ref, wdb_ref, cast_blocks)
    tm = h_ref.shape[0]
    tf = wg_ref.shape[1]
    wg = wg_ref[...].astype(BF16)
    wu = wu_ref[...].astype(BF16)

    prev8 = jnp.zeros((8, tf), F32)
    for r0 in range(0, tm, chunk):
        h = h_ref[r0:r0 + chunk, :]
        g = jnp.dot(h, wg, preferred_element_type=F32)
        up = jnp.dot(h, wu, preferred_element_type=F32)
        ext = jnp.concatenate([prev8, g], axis=0)
        g1 = pltpu.roll(ext, 1, 0)[8:]
        g2 = pltpu.roll(ext, 2, 0)[8:]
        a_ref[r0:r0 + chunk, :] = _conv_gelu_gate(g, g1, g2, up, cw_ref, cb_ref).astype(
            a_ref.dtype
        )
        prev8 = g[chunk - 8:, :]
    tail_ref[...] = prev8

    @pl.when(i == pl.num_programs(0) - 1)
    def _():
        hs = hs_ref[...]
        rows = hs.shape[0]
        g = jnp.dot(hs, wg, preferred_element_type=F32)
        up = jnp.dot(hs, wu, preferred_element_type=F32)
        gs_ref[...] = g
        t = lax.broadcasted_iota(jnp.int32, g.shape, 0) % t_new
        st2 = st_ref[...]
        st1 = pltpu.roll(st2, rows - 1, 0)
        g1 = jnp.where(t >= 1, pltpu.roll(g, 1, 0), st1)
        g2 = jnp.where(t >= 2, pltpu.roll(g, 2, 0), st2)
        as_ref[...] = _conv_gelu_gate(g, g1, g2, up, cw_ref, cb_ref).astype(as_ref.dtype)


def ffn_a(h, w_gate, w_up, w_down, conv_w, conv_b, conv_state, layer, *, m_prompt,
          m_sample, seq, t_new, tf, chunk, cast_rows):
    m_total, d = h.shape
    f = w_gate.shape[-1]
    nf = f // tf
    ni = m_prompt // seq
    assert t_new % 8 == 0 and t_new >= CONV_W - 1
    st = jnp.pad(conv_state, ((0, 0), (0, t_new - (CONV_W - 1)), (0, 0)))
    st = st.reshape(m_sample, f)
    sample_col = lambda i, j: (0, jnp.where(i == ni - 1, j, 0))
    wd_in, wd_out, wd_shape, cast_blocks = _side_cast_specs(
        w_down, layer, cast_rows, ni * nf, nf
    )
    return pl.pallas_call(
        functools.partial(_ffn_a_kernel, chunk=chunk, t_new=t_new,
                          cast_blocks=cast_blocks),
        grid=(ni, nf),
        in_specs=[
            pl.BlockSpec((seq, d), lambda i, j: (i, 0), pipeline_mode=pl.Buffered(1)),
            pl.BlockSpec((m_sample, d), lambda i, j: (m_prompt // m_sample, 0)),
            pl.BlockSpec((None, d, tf), lambda i, j: (layer, 0, j)),
            pl.BlockSpec((None, d, tf), lambda i, j: (layer, 0, j)),
            pl.BlockSpec((None, CONV_W, tf), lambda i, j: (layer, 0, j)),
            pl.BlockSpec((None, 1, tf), lambda i, j: (layer, 0, j)),
            pl.BlockSpec((m_sample, tf), lambda i, j: (0, j)),
            wd_in,
        ],
        out_specs=[
            pl.BlockSpec((seq, tf), lambda i, j: (i, j)),
            pl.BlockSpec((None, 8, tf), lambda i, j: (i, 0, j)),
            pl.BlockSpec((m_sample, tf), sample_col),
            pl.BlockSpec((m_sample, tf), sample_col),
            wd_out,
        ],
        out_shape=[
            jax.ShapeDtypeStruct((m_total, f), BF16),
            jax.ShapeDtypeStruct((ni, 8, f), F32),
            jax.ShapeDtypeStruct((m_sample, f), BF16),
            jax.ShapeDtypeStruct((m_sample, f), F32),
            wd_shape,
        ],
        compiler_params=_params(("arbitrary", "arbitrary"), 56),
        name="ffn_a",
    )(h, h, w_gate, w_up, conv_w, conv_b, st, w_down)


def kernel(x_prompt, x_sample, state_swa_k, state_swa_v, state_conv, w_in, w_out,
           attn_sinks, gm_spatial, gm_bias, gm_norm, norm_pre_mix, norm_post_mix,
           norm_pre_ffn, norm_post_ffn, w_ffn_gate, w_ffn_up, w_ffn_down, conv_w,
           conv_b):
    batch, seq, d = x_prompt.shape
    dec_batch, dec_seq, _ = x_sample.shape
    depth = w_in.shape[0]
    f = w_ffn_gate.shape[-1]
    mp = batch * seq
    ms = dec_batch * dec_seq
    m = mp + ms
    keep = min(WINDOW, seq)

    tm_mm = m // 8
    tm_down = m // 12
    tf = 256
    tm_rows = ms
    assert mp % tm_rows == 0

    w_in_b = w_in[0].astype(BF16)[None]
    conv_b3 = conv_b.reshape(depth, 1, f)
    cast_rows = 64

    x = (x_prompt.reshape(mp, d), x_sample.reshape(ms, d))
    h = rms_rows(x, norm_pre_mix[0].reshape(1, d), tm=tm_rows)

    p_k, p_v, p_c, s_k, s_v, s_c, s_g = [], [], [], [], [], [], []
    for l in range(depth):
        last = l + 1 == depth
        tn = 512
        q, _ = matmul_cols(h, w_in_b, 0, col_block0=0, n=Q_COLS, tm=tm_mm, tn=2 * tn,
                           out_dtype=BF16, epilogue=_scale_q, name="mm_q")
        kv, _ = matmul_cols(h, w_in_b, 0, col_block0=Q_COLS // tn, n=2 * KV_COLS,
                            tm=tm_mm, tn=tn, out_dtype=F32, epilogue=_identity,
                            name="mm_kv")
        gm, (w_out_b,) = matmul_cols(
            h, w_in_b, 0, col_block0=(Q_COLS + 2 * KV_COLS) // tn, n=2 * GM_WIDTH,
            tm=2 * tm_mm, tn=tn, out_dtype=BF16, epilogue=_gelu, name="mm_gm",
            casts=[(w_out, l, 2 * cast_rows)], vmem_mib=58)
        w_out_b = w_out_b.reshape(1, 2, Q_COLS, d)

        kv_p = jnp.stack([kv[(b + 1) * seq - keep:(b + 1) * seq] for b in range(batch)])
        kv_s = kv[mp:]
        k_s, v_s = kv_s[:, :KV_COLS], kv_s[:, KV_COLS:]
        p_k.append(kv_p[..., :KV_COLS].reshape(batch, keep, N_KV_HEADS, HEAD_DIM))
        p_v.append(kv_p[..., KV_COLS:].reshape(batch, keep, N_KV_HEADS, HEAD_DIM))
        s_k.append(k_s.reshape(dec_batch, dec_seq, N_KV_HEADS, HEAD_DIM))
        s_v.append(v_s.reshape(dec_batch, dec_seq, N_KV_HEADS, HEAD_DIM))

        att_s = attn_sample(q[mp:], k_s, v_s, state_swa_k[l], state_swa_v[l],
                            attn_sinks[l], nb=8)
        att = attn_prompt(q, kv, attn_sinks[l], att_s, m_total=m, m_prompt=mp, seq=seq,
                          per_step=2)

        gmo_s, gv_s = gmlp_sample(gm, gm_spatial[l], gm_bias[l], gm_norm[l].reshape(1, -1),
                                  m_prompt=mp, m_sample=ms, t_new=dec_seq)
        gmo = gmlp_prompt(gm, gm_spatial[l], gm_bias[l].T, gm_norm[l].reshape(1, -1),
                          gmo_s, m_total=m, m_prompt=mp, per_step=2)
        s_g.append(gv_s.reshape(dec_batch, dec_seq, GM_HEADS, GM_HEAD_DIM))

        mix, cast_out = matmul_two(
            att, gmo, w_out_b, 0, tm=tm_mm, tn=1024, out_dtype=BF16, name="mm_out",
            casts=[] if last else [(w_in, l + 1, 2 * cast_rows)])
        x, h = post_rows(mix, x, norm_post_mix[l].reshape(1, d),
                         norm_pre_ffn[l].reshape(1, d), tm=tm_rows)

        a, tail, a_s, g_s, w_down_b = ffn_a(
            h, w_ffn_gate, w_ffn_up, w_ffn_down, conv_w, conv_b3, state_conv[l], l,
            m_prompt=mp, m_sample=ms, seq=seq, t_new=dec_seq, tf=tf, chunk=512,
            cast_rows=2 * cast_rows)
        a = place_rows(a, a_s, row_block=mp // ms)
        p_c.append(tail[:, 8 - (CONV_W - 1):])
        s_c.append(g_s.reshape(dec_batch, dec_seq, f)[:, dec_seq - (CONV_W - 1):])

        ffn, _ = matmul_cols(a, w_down_b[None], 0, col_block0=0, n=d, tm=tm_down, tn=512,
                             out_dtype=BF16, epilogue=_identity, name="mm_down",
                             vmem_mib=58)
        x, h = post_rows(ffn, x, norm_post_ffn[l].reshape(1, d),
                         None if last else norm_pre_mix[l + 1].reshape(1, d),
                         tm=tm_rows, split_out=(mp, ms) if last else None)
        if not last:
            w_in_b = cast_out[0][None]

    x_p, x_s = x
    return (x_p.reshape(batch, seq, d), x_s.reshape(dec_batch, dec_seq, d),
            jnp.stack(p_k), jnp.stack(p_v), jnp.stack(p_c),
            jnp.stack(s_k), jnp.stack(s_v), jnp.stack(s_c), jnp.stack(s_g))
```

```python
import functools
import math

import jax
import jax.numpy as jnp
from jax import lax
from jax.experimental import pallas as pl
from jax.experimental.pallas import tpu as pltpu

F32 = jnp.float32
BF16 = jnp.bfloat16

D_MODEL = 4096
HEAD_DIM = 64
N_HEADS = 32
N_KV_HEADS = 4
GQA_GROUP = 8
WINDOW = 128
CHUNK = 128
GM_HEAD_DIM = 128
GM_HEADS = 16
GM_WIDTH = 2048
Q_COLS = 2048
KV_COLS = 256
CONV_W = 3
EPS = 1e-6
LANES = 128
LOG2_E = math.log2(math.e)

MIB = 1024 * 1024


def _params(semantics, vmem_mib):
    return pltpu.CompilerParams(
        dimension_semantics=semantics, vmem_limit_bytes=vmem_mib * MIB
    )


def _rms(x, g):
    ms = jnp.mean(x * x, axis=-1, keepdims=True)
    return x * lax.rsqrt(ms + EPS) * g


def _split_specs(tm, d, n_prompt_tiles):
    return [
        pl.BlockSpec((tm, d), lambda i: (jnp.minimum(i, n_prompt_tiles - 1), 0)),
        pl.BlockSpec((tm, d), lambda i: (0, 0)),
    ]


def _load_rows(refs, n_prompt_tiles):
    if len(refs) == 1:
        return refs[0][...]
    is_prompt = pl.program_id(0) < n_prompt_tiles
    return jnp.where(is_prompt, refs[0][...], refs[1][...])


def _store_rows(refs, value, n_prompt_tiles):
    if len(refs) == 1:
        refs[0][...] = value
        return

    @pl.when(pl.program_id(0) < n_prompt_tiles)
    def _():
        refs[0][...] = value

    @pl.when(pl.program_id(0) >= n_prompt_tiles)
    def _():
        refs[1][...] = value


def _rms_kernel(*refs, n_in, n_prompt_tiles):
    x = _load_rows(refs[:n_in], n_prompt_tiles)
    g_ref, o_ref = refs[n_in:]
    o_ref[...] = _rms(x, g_ref[...]).astype(o_ref.dtype)


def rms_rows(x_parts, g, *, tm):
    d = x_parts[0].shape[1]
    m = sum(p.shape[0] for p in x_parts)
    npt = x_parts[0].shape[0] // tm
    row = pl.BlockSpec((tm, d), lambda i: (i, 0))
    x_specs = [row] if len(x_parts) == 1 else _split_specs(tm, d, npt)
    return pl.pallas_call(
        functools.partial(_rms_kernel, n_in=len(x_parts), n_prompt_tiles=npt),
        grid=(m // tm,),
        in_specs=x_specs + [pl.BlockSpec((1, d), lambda i: (0, 0))],
        out_specs=row,
        out_shape=jax.ShapeDtypeStruct((m, d), BF16),
        compiler_params=_params(("arbitrary",), 40),
        name="rms_rows",
    )(*x_parts, g)


def _post_kernel(*refs, n_in, n_out, has_next, n_prompt_tiles):
    m_ref = refs[0]
    x = _load_rows(refs[1:1 + n_in], n_prompt_tiles)
    rest = refs[1 + n_in:]
    xn = x + _rms(m_ref[...].astype(F32), rest[0][...])
    rest = rest[1:]
    if has_next:
        gnext_ref, rest = rest[0], rest[1:]
    _store_rows(rest[:n_out], xn, n_prompt_tiles)
    if has_next:
        ho_ref = rest[n_out]
        ho_ref[...] = _rms(xn, gnext_ref[...]).astype(ho_ref.dtype)


def post_rows(m_in, x_parts, g_post, g_next, *, tm, split_out=None):
    m, d = m_in.shape
    npt = (x_parts[0].shape[0] if split_out is None else split_out[0]) // tm
    if len(x_parts) == 1 and split_out is None:
        npt = m // tm
    row = pl.BlockSpec((tm, d), lambda i: (i, 0))
    vec = pl.BlockSpec((1, d), lambda i: (0, 0))
    x_specs = [row] if len(x_parts) == 1 else _split_specs(tm, d, npt)
    if split_out is None:
        xo_specs = [row]
        xo_shapes = [jax.ShapeDtypeStruct((m, d), F32)]
    else:
        xo_specs = _split_specs(tm, d, npt)
        xo_shapes = [jax.ShapeDtypeStruct((r, d), F32) for r in split_out]
    has_next = g_next is not None
    gains = [g_post, g_next] if has_next else [g_post]
    out = pl.pallas_call(
        functools.partial(_post_kernel, n_in=len(x_parts), n_out=len(xo_specs),
                          has_next=has_next, n_prompt_tiles=npt),
        grid=(m // tm,),
        in_specs=[row] + x_specs + [vec] * len(gains),
        out_specs=xo_specs + ([row] if has_next else []),
        out_shape=xo_shapes
        + ([jax.ShapeDtypeStruct((m, d), BF16)] if has_next else []),
        compiler_params=_params(("arbitrary",), 48),
        name="post_rows",
    )(m_in, *x_parts, *gains)
    x_new = out[:len(xo_specs)]
    return (x_new, out[-1]) if has_next else (x_new, None)


def _grid_step():
    return pl.program_id(0) * pl.num_programs(1) + pl.program_id(1)


def _side_cast(src_ref, dst_ref, n_blocks):
    @pl.when(_grid_step() < n_blocks)
    def _():
        dst_ref[...] = src_ref[...].astype(dst_ref.dtype)


def _side_cast_specs(src, layer, block_rows, n_steps, n_cols_grid):
    _, rows, cols = src.shape
    n_blocks = rows // block_rows
    assert rows % block_rows == 0 and n_blocks <= n_steps

    def block(i, j):
        return jnp.minimum(i * n_cols_grid + j, n_blocks - 1)

    in_spec = pl.BlockSpec((None, block_rows, cols), lambda i, j: (layer, block(i, j), 0))
    out_spec = pl.BlockSpec((block_rows, cols), lambda i, j: (block(i, j), 0))
    return in_spec, out_spec, jax.ShapeDtypeStruct((rows, cols), BF16), n_blocks


def _mm_kernel(a_ref, b_ref, *rest, epilogue, cast_blocks):
    o_ref = rest[-2] if cast_blocks else rest[-1]
    acc = jnp.dot(a_ref[...], b_ref[...], preferred_element_type=F32)
    o_ref[...] = epilogue(acc).astype(o_ref.dtype)
    if cast_blocks:
        _side_cast(rest[0], rest[-1], cast_blocks)


def _mm2_kernel(a1_ref, a2_ref, b1_ref, b2_ref, *rest, cast_blocks):
    o_ref = rest[-2] if cast_blocks else rest[-1]
    acc = jnp.dot(a1_ref[...], b1_ref[...], preferred_element_type=F32)
    acc = acc + jnp.dot(a2_ref[...], b2_ref[...], preferred_element_type=F32)
    o_ref[...] = acc.astype(o_ref.dtype)
    if cast_blocks:
        _side_cast(rest[0], rest[-1], cast_blocks)


def _with_side_cast(cast, grid, in_specs, out_specs, out_shapes, operands):
    if cast is None:
        return 0
    src, layer, block_rows = cast
    i_spec, o_spec, o_shape, n_blocks = _side_cast_specs(
        src, layer, block_rows, grid[0] * grid[1], grid[1]
    )
    in_specs.append(i_spec)
    out_specs.append(o_spec)
    out_shapes.append(o_shape)
    operands.append(src)
    return n_blocks


def matmul_cols(a, w, layer, *, col_block0, n, tm, tn, out_dtype, epilogue, name,
                cast=None, vmem_mib=56):
    m, k = a.shape
    grid = (m // tm, n // tn)
    in_specs = [
        pl.BlockSpec((tm, k), lambda i, j: (i, 0)),
        pl.BlockSpec((None, k, tn), lambda i, j: (layer, 0, j + col_block0)),
    ]
    out_specs = [pl.BlockSpec((tm, tn), lambda i, j: (i, j))]
    out_shapes = [jax.ShapeDtypeStruct((m, n), out_dtype)]
    operands = [a, w]
    cast_blocks = _with_side_cast(cast, grid, in_specs, out_specs, out_shapes, operands)
    out = pl.pallas_call(
        functools.partial(_mm_kernel, epilogue=epilogue, cast_blocks=cast_blocks),
        grid=grid,
        in_specs=in_specs,
        out_specs=out_specs,
        out_shape=out_shapes,
        compiler_params=_params(("arbitrary", "arbitrary"), vmem_mib),
        name=name,
    )(*operands)
    return out[0], (out[1] if cast_blocks else None)


def matmul_two(a1, a2, w, layer, *, tm, tn, out_dtype, name, cast=None):
    m, k1 = a1.shape
    k2 = a2.shape[1]
    assert k1 == k2
    n = w.shape[-1]
    grid = (m // tm, n // tn)
    in_specs = [
        pl.BlockSpec((tm, k1), lambda i, j: (i, 0)),
        pl.BlockSpec((tm, k2), lambda i, j: (i, 0)),
        pl.BlockSpec((None, None, k1, tn), lambda i, j: (layer, 0, 0, j)),
        pl.BlockSpec((None, None, k2, tn), lambda i, j: (layer, 1, 0, j)),
    ]
    out_specs = [pl.BlockSpec((tm, tn), lambda i, j: (i, j))]
    out_shapes = [jax.ShapeDtypeStruct((m, n), out_dtype)]
    operands = [a1, a2, w, w]
    cast_blocks = _with_side_cast(cast, grid, in_specs, out_specs, out_shapes, operands)
    out = pl.pallas_call(
        functools.partial(_mm2_kernel, cast_blocks=cast_blocks),
        grid=grid,
        in_specs=in_specs,
        out_specs=out_specs,
        out_shape=out_shapes,
        compiler_params=_params(("arbitrary", "arbitrary"), 56),
        name=name,
    )(*operands)
    return out[0], (out[1] if cast_blocks else None)


def _identity(x):
    return x


def _scale_q(x):
    return x * (HEAD_DIM ** -0.5 * LOG2_E)


_GELU_A = -2.0 * LOG2_E * math.sqrt(2.0 / math.pi)
_GELU_B = _GELU_A * 0.044715


def _gelu_times(x, y):
    t = x * x * _GELU_B + _GELU_A
    return (x * y) / (1.0 + jnp.exp2(t * x))


def _gelu(x):
    t = x * x * _GELU_B + _GELU_A
    return x / (1.0 + jnp.exp2(t * x))


def _softmax2_with_sink(s, sink):
    m = jnp.maximum(jnp.max(s, axis=-1, keepdims=True), sink)
    e = jnp.exp2(s - m)
    den = jnp.sum(e, axis=-1, keepdims=True) + jnp.exp2(sink - m)
    return e / den


def _attn_prompt_kernel(sink_ref, q_ref, kvc_ref, kvp_ref, os_ref, o_ref, *,
                        blocks_per_seq, n_prompt_steps, per_step):
    step = pl.program_id(0)
    w = WINDOW

    @pl.when(step >= n_prompt_steps)
    def _():
        o_ref[...] = os_ref[...]

    @pl.when(step < n_prompt_steps)
    def _():
        for b in range(per_step):
            rows = pl.ds(b * w, w)
            prev_ref = kvp_ref if b == 0 else kvc_ref.at[pl.ds((b - 1) * w, w)]
            _attn_prompt_block(sink_ref, q_ref.at[rows], kvc_ref.at[rows], prev_ref,
                               o_ref.at[rows], (step * per_step + b) % blocks_per_seq)


def _attn_prompt_block(sink_ref, q_ref, kvc_ref, kvp_ref, o_ref, n):
    w = WINDOW
    row = lax.broadcasted_iota(jnp.int32, (w, 2 * w), 0)
    col = lax.broadcasted_iota(jnp.int32, (w, 2 * w), 1)
    prev_ok = jnp.logical_and(jnp.logical_and(col < w, col > row), n > 0)
    own_ok = jnp.logical_and(col >= w, col - w <= row)
    valid = jnp.logical_or(prev_ok, own_ok)
    for kvh in range(N_KV_HEADS):
        ks = slice(kvh * HEAD_DIM, (kvh + 1) * HEAD_DIM)
        vs = slice(KV_COLS + kvh * HEAD_DIM, KV_COLS + (kvh + 1) * HEAD_DIM)
        k = jnp.concatenate([kvp_ref[:, ks], kvc_ref[:, ks]], axis=0).astype(BF16)
        v = jnp.concatenate([kvp_ref[:, vs], kvc_ref[:, vs]], axis=0).astype(BF16)
        v_ext = jnp.concatenate(
            [v, jnp.zeros((2 * w, LANES - HEAD_DIM), BF16), jnp.ones((2 * w, LANES), BF16)],
            axis=1,
        )
        heads = [kvh * GQA_GROUP + g for g in range(GQA_GROUP)]
        qs = jnp.concatenate(
            [q_ref[:, h * HEAD_DIM:(h + 1) * HEAD_DIM] for h in heads], axis=0
        )
        s = lax.dot_general(
            qs, k, (((1,), (1,)), ((), ())), preferred_element_type=F32
        )
        es, sink_terms = [], []
        for g, h in enumerate(heads):
            sink = sink_ref[h] * LOG2_E
            sg = jnp.where(valid, s[g * w:(g + 1) * w], -jnp.inf)
            m = jnp.maximum(jnp.max(sg, axis=-1, keepdims=True), sink)
            es.append(jnp.exp2(sg - m).astype(BF16))
            sink_terms.append(jnp.exp2(sink - m))
        o = jnp.dot(jnp.concatenate(es, axis=0), v_ext, preferred_element_type=F32)
        for g, h in enumerate(heads):
            og = o[g * w:(g + 1) * w]
            out = og[:, :LANES] / (og[:, LANES:] + sink_terms[g])
            o_ref[:, h * HEAD_DIM:(h + 1) * HEAD_DIM] = out[:, :HEAD_DIM].astype(
                o_ref.dtype
            )


def attn_prompt(q, kv, sinks, att_sample, *, m_total, m_prompt, seq, per_step):
    w = WINDOW
    rows = per_step * w
    assert m_prompt % rows == 0 and att_sample.shape[0] % rows == 0
    nsteps = m_prompt // rows
    last = nsteps - 1
    last_prev = m_prompt // w - 1
    return pl.pallas_call(
        functools.partial(_attn_prompt_kernel, blocks_per_seq=seq // w,
                          n_prompt_steps=nsteps, per_step=per_step),
        grid=(m_total // rows,),
        in_specs=[
            pl.BlockSpec(memory_space=pltpu.SMEM),
            pl.BlockSpec((rows, Q_COLS), lambda r: (jnp.minimum(r, last), 0)),
            pl.BlockSpec((rows, 2 * KV_COLS), lambda r: (jnp.minimum(r, last), 0)),
            pl.BlockSpec((w, 2 * KV_COLS),
                         lambda r: (jnp.clip(per_step * r - 1, 0, last_prev), 0)),
            pl.BlockSpec((rows, Q_COLS), lambda r: (jnp.maximum(r - nsteps, 0), 0)),
        ],
        out_specs=pl.BlockSpec((rows, Q_COLS), lambda r: (r, 0)),
        out_shape=jax.ShapeDtypeStruct((m_total, Q_COLS), BF16),
        compiler_params=_params(("arbitrary",), 40),
        name="attn_prompt",
    )(sinks, q, kv, kv, att_sample)


def _attn_sample_kernel(sink_ref, q_ref, k_ref, v_ref, o_ref, *, t_new):
    nb, nkv, rows, d = q_ref.shape
    keys = k_ref.shape[2]
    q = q_ref[...].reshape(nb * nkv, rows, d)
    k = k_ref[...].reshape(nb * nkv, keys, d)
    v = v_ref[...].reshape(nb * nkv, keys, d)
    s = jnp.einsum("bqd,bkd->bqk", q, k, preferred_element_type=F32)
    t = lax.broadcasted_iota(jnp.int32, (rows, keys), 0) // GQA_GROUP
    col = lax.broadcasted_iota(jnp.int32, (rows, keys), 1)
    valid = jnp.logical_and(col > t, col <= WINDOW + t)
    s = jnp.where(valid[None], s, -jnp.inf)
    sink = jnp.broadcast_to(sink_ref[...][None], (nb, nkv, rows, 1)).reshape(
        nb * nkv, rows, 1
    )
    p = _softmax2_with_sink(s, sink * LOG2_E).astype(BF16)
    o = jnp.einsum("bqk,bkd->bqd", p, v, preferred_element_type=F32)
    o_ref[...] = o.reshape(nb, nkv, rows, d).astype(o_ref.dtype)


def attn_sample(q_s, k_new, v_new, k_buf, v_buf, sinks, *, nb):
    b, l = k_buf.shape[0], k_buf.shape[1]
    t_new = q_s.shape[0] // b
    rows = t_new * GQA_GROUP
    keys = 2 * WINDOW
    q4 = q_s.reshape(b, t_new, N_KV_HEADS, GQA_GROUP, HEAD_DIM)
    q4 = q4.transpose(0, 2, 1, 3, 4).reshape(b, N_KV_HEADS, rows, HEAD_DIM)

    def keys_layout(buf, new):
        cat = jnp.concatenate(
            [buf, new.reshape(b, t_new, N_KV_HEADS, HEAD_DIM)], axis=1
        )
        cat = jnp.pad(cat, ((0, 0), (0, keys - l - t_new), (0, 0), (0, 0)))
        return cat.transpose(0, 2, 1, 3).astype(BF16)

    k4 = keys_layout(k_buf, k_new)
    v4 = keys_layout(v_buf, v_new)
    sink_rows = jnp.broadcast_to(
        sinks.reshape(N_KV_HEADS, 1, GQA_GROUP), (N_KV_HEADS, t_new, GQA_GROUP)
    ).reshape(N_KV_HEADS, rows, 1)
    o4 = pl.pallas_call(
        functools.partial(_attn_sample_kernel, t_new=t_new),
        grid=(b // nb,),
        in_specs=[
            pl.BlockSpec((N_KV_HEADS, rows, 1), lambda i: (0, 0, 0)),
            pl.BlockSpec((nb, N_KV_HEADS, rows, HEAD_DIM), lambda i: (i, 0, 0, 0)),
            pl.BlockSpec((nb, N_KV_HEADS, keys, HEAD_DIM), lambda i: (i, 0, 0, 0)),
            pl.BlockSpec((nb, N_KV_HEADS, keys, HEAD_DIM), lambda i: (i, 0, 0, 0)),
        ],
        out_specs=pl.BlockSpec(
            (nb, N_KV_HEADS, rows, HEAD_DIM), lambda i: (i, 0, 0, 0)
        ),
        out_shape=jax.ShapeDtypeStruct((b, N_KV_HEADS, rows, HEAD_DIM), BF16),
        compiler_params=_params(("parallel",), 40),
        name="attn_sample",
    )(sink_rows, q4, k4, v4)
    o = o4.reshape(b, N_KV_HEADS, t_new, GQA_GROUP, HEAD_DIM).transpose(0, 2, 1, 3, 4)
    return o.reshape(b * t_new, Q_COLS)


def _place_rows_kernel(rows_ref, buf_ref, o_ref):
    del buf_ref
    o_ref[...] = rows_ref[...]


def place_rows(buf, rows, *, row_block):
    r, c = rows.shape
    return pl.pallas_call(
        _place_rows_kernel,
        grid=(1,),
        in_specs=[
            pl.BlockSpec((r, c), lambda i: (0, 0)),
            pl.BlockSpec(memory_space=pl.ANY),
        ],
        out_specs=pl.BlockSpec((r, c), lambda i: (row_block, 0)),
        out_shape=jax.ShapeDtypeStruct(buf.shape, buf.dtype),
        input_output_aliases={1: 0},
        compiler_params=_params(("arbitrary",), 16),
        name="place_rows",
    )(rows, buf)


def _gmlp_prompt_kernel(u_ref, v_ref, w_ref, bt_ref, gn_ref, os_ref, o_ref, *,
                        n_prompt_steps, per_step):
    @pl.when(pl.program_id(0) >= n_prompt_steps)
    def _():
        o_ref[...] = os_ref[...]

    @pl.when(pl.program_id(0) < n_prompt_steps)
    def _():
        for b in range(per_step):
            rows = pl.ds(b * CHUNK, CHUNK)
            _gmlp_prompt_chunk(u_ref.at[rows], v_ref.at[rows], w_ref, bt_ref, gn_ref,
                               o_ref.at[rows])


def _gmlp_prompt_chunk(u_ref, v_ref, w_ref, bt_ref, gn_ref, o_ref):
    gv = _rms(v_ref[...].astype(F32), gn_ref[...])
    c = CHUNK
    row = lax.broadcasted_iota(jnp.int32, (c, c), 0)
    col = lax.broadcasted_iota(jnp.int32, (c, c), 1)
    causal = col <= row
    for h in range(GM_HEADS):
        cs = slice(h * GM_HEAD_DIM, (h + 1) * GM_HEAD_DIM)
        w = jnp.where(causal, w_ref[h], 0.0).astype(BF16)
        mixed = jnp.dot(w, gv[:, cs].astype(BF16), preferred_element_type=F32)
        mixed = mixed + bt_ref[:, h:h + 1]
        o_ref[:, cs] = (u_ref[:, cs].astype(F32) * mixed).astype(o_ref.dtype)


def gmlp_prompt(gm, w_s, b_t, g_norm, gated_sample, *, m_total, m_prompt, per_step):
    c = CHUNK
    rows = per_step * c
    assert m_prompt % rows == 0 and gated_sample.shape[0] % rows == 0
    nsteps = m_prompt // rows
    last = nsteps - 1
    return pl.pallas_call(
        functools.partial(_gmlp_prompt_kernel, n_prompt_steps=nsteps, per_step=per_step),
        grid=(m_total // rows,),
        in_specs=[
            pl.BlockSpec((rows, GM_WIDTH), lambda r: (jnp.minimum(r, last), 0)),
            pl.BlockSpec((rows, GM_WIDTH), lambda r: (jnp.minimum(r, last), 1)),
            pl.BlockSpec((GM_HEADS, c, c), lambda r: (0, 0, 0)),
            pl.BlockSpec((c, GM_HEADS), lambda r: (0, 0)),
            pl.BlockSpec((1, GM_WIDTH), lambda r: (0, 0)),
            pl.BlockSpec((rows, GM_WIDTH), lambda r: (jnp.maximum(r - nsteps, 0), 0)),
        ],
        out_specs=pl.BlockSpec((rows, GM_WIDTH), lambda r: (r, 0)),
        out_shape=jax.ShapeDtypeStruct((m_total, GM_WIDTH), BF16),
        compiler_params=_params(("arbitrary",), 40),
        name="gmlp_prompt",
    )(gm, gm, w_s, b_t, g_norm, gated_sample)


def _gmlp_sample_kernel(u_ref, v_ref, wc_ref, bc_ref, gn_ref, o_ref, gv_ref, *, t_new):
    rows = u_ref.shape[0]
    nseq = rows // t_new
    gv = _rms(v_ref[...].astype(F32), gn_ref[...])
    gv_ref[...] = gv
    t = lax.broadcasted_iota(jnp.int32, (t_new, GM_HEAD_DIM), 0)
    for h in range(GM_HEADS):
        cs = slice(h * GM_HEAD_DIM, (h + 1) * GM_HEAD_DIM)
        x = gv[:, cs].reshape(nseq, t_new, GM_HEAD_DIM)
        acc = jnp.zeros((nseq, t_new, GM_HEAD_DIM), F32)
        for s in range(t_new):
            coef = jnp.where(t >= s, wc_ref[h, s], 0.0)
            acc = acc + coef[None] * x[:, s:s + 1, :]
        mixed = acc + bc_ref[h][None]
        u = u_ref[:, cs].astype(F32).reshape(nseq, t_new, GM_HEAD_DIM)
        o_ref[:, cs] = (u * mixed).reshape(rows, GM_HEAD_DIM).astype(o_ref.dtype)


def gmlp_sample(gm, w_s, b_s, g_norm, *, m_prompt, m_sample, t_new):
    lanes = GM_HEAD_DIM
    w8 = w_s[:, :t_new, :t_new]
    wc = jnp.broadcast_to(
        w8.transpose(0, 2, 1)[..., None], (GM_HEADS, t_new, t_new, lanes)
    )
    bc = jnp.broadcast_to(b_s[:, :t_new, None], (GM_HEADS, t_new, lanes))
    blk = m_prompt // m_sample
    return pl.pallas_call(
        functools.partial(_gmlp_sample_kernel, t_new=t_new),
        grid=(1,),
        in_specs=[
            pl.BlockSpec((m_sample, GM_WIDTH), lambda i: (blk, 0)),
            pl.BlockSpec((m_sample, GM_WIDTH), lambda i: (blk, 1)),
            pl.BlockSpec((GM_HEADS, t_new, t_new, lanes), lambda i: (0, 0, 0, 0)),
            pl.BlockSpec((GM_HEADS, t_new, lanes), lambda i: (0, 0, 0)),
            pl.BlockSpec((1, GM_WIDTH), lambda i: (0, 0)),
        ],
        out_specs=[
            pl.BlockSpec((m_sample, GM_WIDTH), lambda i: (0, 0)),
            pl.BlockSpec((m_sample, GM_WIDTH), lambda i: (0, 0)),
        ],
        out_shape=[
            jax.ShapeDtypeStruct((m_sample, GM_WIDTH), BF16),
            jax.ShapeDtypeStruct((m_sample, GM_WIDTH), F32),
        ],
        compiler_params=_params(("arbitrary",), 40),
        name="gmlp_sample",
    )(gm, gm, wc, bc, g_norm)


def _conv_gelu_gate(g, g1, g2, up, cw_ref, cb_ref):
    c = cb_ref[...] + cw_ref[0:1, :] * g2
    c = c + cw_ref[1:2, :] * g1
    c = c + cw_ref[2:3, :] * g
    return _gelu_times(c, up)


def _ffn_a_kernel(h_ref, hs_ref, wg_ref, wu_ref, cw_ref, cb_ref, st_ref, wd_ref, a_ref,
                  tail_ref, as_ref, gs_ref, wdb_ref, s1_ref, s2_ref, *, chunk, t_new,
                  cast_blocks):
    i = pl.program_id(0)
    _side_cast(wd_ref, wdb_ref, cast_blocks)
    tm = h_ref.shape[0]
    tf = wg_ref.shape[1]
    wg = wg_ref[...].astype(BF16)
    wu = wu_ref[...].astype(BF16)

    prev8 = jnp.zeros((8, tf), F32)
    for r0 in range(0, tm, chunk):
        h = h_ref[r0:r0 + chunk, :]
        g = jnp.dot(h, wg, preferred_element_type=F32)
        up = jnp.dot(h, wu, preferred_element_type=F32)
        s1_ref[1:9, :] = prev8
        s1_ref[9:9 + chunk, :] = g
        s2_ref[2:10, :] = prev8
        s2_ref[10:10 + chunk, :] = g
        g1 = s1_ref[8:8 + chunk, :]
        g2 = s2_ref[8:8 + chunk, :]
        a_ref[r0:r0 + chunk, :] = _conv_gelu_gate(g, g1, g2, up, cw_ref, cb_ref).astype(
            a_ref.dtype
        )
        prev8 = g[chunk - 8:, :]
    tail_ref[...] = prev8

    @pl.when(i == pl.num_programs(0) - 1)
    def _():
        hs = hs_ref[...]
        rows = hs.shape[0]
        g = jnp.dot(hs, wg, preferred_element_type=F32)
        up = jnp.dot(hs, wu, preferred_element_type=F32)
        gs_ref[...] = g
        t = lax.broadcasted_iota(jnp.int32, g.shape, 0) % t_new
        st2 = st_ref[...]
        st1 = pltpu.roll(st2, rows - 1, 0)
        g1 = jnp.where(t >= 1, pltpu.roll(g, 1, 0), st1)
        g2 = jnp.where(t >= 2, pltpu.roll(g, 2, 0), st2)
        as_ref[...] = _conv_gelu_gate(g, g1, g2, up, cw_ref, cb_ref).astype(as_ref.dtype)


def ffn_a(h, w_gate, w_up, w_down, conv_w, conv_b, conv_state, layer, *, m_prompt,
          m_sample, seq, t_new, tf, chunk, cast_rows):
    m_total, d = h.shape
    f = w_gate.shape[-1]
    nf = f // tf
    ni = m_prompt // seq
    assert t_new % 8 == 0 and t_new >= CONV_W - 1
    st = jnp.pad(conv_state, ((0, 0), (0, t_new - (CONV_W - 1)), (0, 0)))
    st = st.reshape(m_sample, f)
    sample_col = lambda i, j: (0, jnp.where(i == ni - 1, j, 0))
    wd_in, wd_out, wd_shape, cast_blocks = _side_cast_specs(
        w_down, layer, cast_rows, ni * nf, nf
    )
    return pl.pallas_call(
        functools.partial(_ffn_a_kernel, chunk=chunk, t_new=t_new,
                          cast_blocks=cast_blocks),
        grid=(ni, nf),
        in_specs=[
            pl.BlockSpec((seq, d), lambda i, j: (i, 0), pipeline_mode=pl.Buffered(1)),
            pl.BlockSpec((m_sample, d), lambda i, j: (m_prompt // m_sample, 0)),
            pl.BlockSpec((None, d, tf), lambda i, j: (layer, 0, j)),
            pl.BlockSpec((None, d, tf), lambda i, j: (layer, 0, j)),
            pl.BlockSpec((None, CONV_W, tf), lambda i, j: (layer, 0, j)),
            pl.BlockSpec((None, 1, tf), lambda i, j: (layer, 0, j)),
            pl.BlockSpec((m_sample, tf), lambda i, j: (0, j)),
            wd_in,
        ],
        out_specs=[
            pl.BlockSpec((seq, tf), lambda i, j: (i, j)),
            pl.BlockSpec((None, 8, tf), lambda i, j: (i, 0, j)),
            pl.BlockSpec((m_sample, tf), sample_col),
            pl.BlockSpec((m_sample, tf), sample_col),
            wd_out,
        ],
        out_shape=[
            jax.ShapeDtypeStruct((m_total, f), BF16),
            jax.ShapeDtypeStruct((ni, 8, f), F32),
            jax.ShapeDtypeStruct((m_sample, f), BF16),
            jax.ShapeDtypeStruct((m_sample, f), F32),
            wd_shape,
        ],
        scratch_shapes=[pltpu.VMEM((chunk + 16, tf), F32), pltpu.VMEM((chunk + 16, tf), F32)],
        compiler_params=_params(("arbitrary", "arbitrary"), 56),
        name="ffn_a",
    )(h, h, w_gate, w_up, conv_w, conv_b, st, w_down)


def kernel(x_prompt, x_sample, state_swa_k, state_swa_v, state_conv, w_in, w_out,
           attn_sinks, gm_spatial, gm_bias, gm_norm, norm_pre_mix, norm_post_mix,
           norm_pre_ffn, norm_post_ffn, w_ffn_gate, w_ffn_up, w_ffn_down, conv_w,
           conv_b):
    batch, seq, d = x_prompt.shape
    dec_batch, dec_seq, _ = x_sample.shape
    depth = w_in.shape[0]
    f = w_ffn_gate.shape[-1]
    mp = batch * seq
    ms = dec_batch * dec_seq
    m = mp + ms
    keep = min(WINDOW, seq)

    tm_mm = m // 8
    tm_down = m // 12
    tf = 256
    tm_rows = ms
    assert mp % tm_rows == 0

    w_in_b = w_in[0].astype(BF16)[None]
    conv_b3 = conv_b.reshape(depth, 1, f)
    cast_rows = 64

    x = (x_prompt.reshape(mp, d), x_sample.reshape(ms, d))
    h = rms_rows(x, norm_pre_mix[0].reshape(1, d), tm=tm_rows)

    p_k, p_v, p_c, s_k, s_v, s_c, s_g = [], [], [], [], [], [], []
    for l in range(depth):
        last = l + 1 == depth
        tn = 512
        q, _ = matmul_cols(h, w_in_b, 0, col_block0=0, n=Q_COLS, tm=tm_mm, tn=2 * tn,
                           out_dtype=BF16, epilogue=_scale_q, name="mm_q")
        kv, _ = matmul_cols(h, w_in_b, 0, col_block0=Q_COLS // tn, n=2 * KV_COLS,
                            tm=tm_mm, tn=tn, out_dtype=F32, epilogue=_identity,
                            name="mm_kv")
        gm, w_out_b = matmul_cols(
            h, w_in_b, 0, col_block0=(Q_COLS + 2 * KV_COLS) // tn, n=2 * GM_WIDTH,
            tm=tm_mm, tn=tn, out_dtype=BF16, epilogue=_gelu, name="mm_gm",
            cast=(w_out, l, cast_rows))
        w_out_b = w_out_b.reshape(1, 2, Q_COLS, d)

        kv_p = jnp.stack([kv[(b + 1) * seq - keep:(b + 1) * seq] for b in range(batch)])
        kv_s = kv[mp:]
        k_s, v_s = kv_s[:, :KV_COLS], kv_s[:, KV_COLS:]
        p_k.append(kv_p[..., :KV_COLS].reshape(batch, keep, N_KV_HEADS, HEAD_DIM))
        p_v.append(kv_p[..., KV_COLS:].reshape(batch, keep, N_KV_HEADS, HEAD_DIM))
        s_k.append(k_s.reshape(dec_batch, dec_seq, N_KV_HEADS, HEAD_DIM))
        s_v.append(v_s.reshape(dec_batch, dec_seq, N_KV_HEADS, HEAD_DIM))

        att_s = attn_sample(q[mp:], k_s, v_s, state_swa_k[l], state_swa_v[l],
                            attn_sinks[l], nb=8)
        att = attn_prompt(q, kv, attn_sinks[l], att_s, m_total=m, m_prompt=mp, seq=seq,
                          per_step=2)

        gmo_s, gv_s = gmlp_sample(gm, gm_spatial[l], gm_bias[l], gm_norm[l].reshape(1, -1),
                                  m_prompt=mp, m_sample=ms, t_new=dec_seq)
        gmo = gmlp_prompt(gm, gm_spatial[l], gm_bias[l].T, gm_norm[l].reshape(1, -1),
                          gmo_s, m_total=m, m_prompt=mp, per_step=2)
        s_g.append(gv_s.reshape(dec_batch, dec_seq, GM_HEADS, GM_HEAD_DIM))

        mix, w_in_next = matmul_two(att, gmo, w_out_b, 0, tm=tm_mm, tn=1024,
                                    out_dtype=BF16, name="mm_out",
                                    cast=None if last else (w_in, l + 1, 2 * cast_rows))
        x, h = post_rows(mix, x, norm_post_mix[l].reshape(1, d),
                         norm_pre_ffn[l].reshape(1, d), tm=tm_rows)

        a, tail, a_s, g_s, w_down_b = ffn_a(
            h, w_ffn_gate, w_ffn_up, w_ffn_down, conv_w, conv_b3, state_conv[l], l,
            m_prompt=mp, m_sample=ms, seq=seq, t_new=dec_seq, tf=tf, chunk=512,
            cast_rows=2 * cast_rows)
        a = place_rows(a, a_s, row_block=mp // ms)
        p_c.append(tail[:, 8 - (CONV_W - 1):])
        s_c.append(g_s.reshape(dec_batch, dec_seq, f)[:, dec_seq - (CONV_W - 1):])

        ffn, _ = matmul_cols(a, w_down_b[None], 0, col_block0=0, n=d, tm=tm_down, tn=512,
                             out_dtype=BF16, epilogue=_identity, name="mm_down",
                             vmem_mib=58)
        x, h = post_rows(ffn, x, norm_post_ffn[l].reshape(1, d),
                         None if last else norm_pre_mix[l + 1].reshape(1, d),
                         tm=tm_rows, split_out=(mp, ms) if last else None)
        if not last:
            w_in_b = w_in_next[None]

    x_p, x_s = x
    return (x_p.reshape(batch, seq, d), x_s.reshape(dec_batch, dec_seq, d),
            jnp.stack(p_k), jnp.stack(p_v), jnp.stack(p_c),
            jnp.stack(s_k), jnp.stack(s_v), jnp.stack(s_c), jnp.stack(s_g))
```

```python
import functools
import math

import jax
import jax.numpy as jnp
from jax import lax
from jax.experimental import pallas as pl
from jax.experimental.pallas import tpu as pltpu

F32 = jnp.float32
BF16 = jnp.bfloat16

D_MODEL = 4096
HEAD_DIM = 64
N_HEADS = 32
N_KV_HEADS = 4
GQA_GROUP = 8
WINDOW = 128
CHUNK = 128
GM_HEAD_DIM = 128
GM_HEADS = 16
GM_WIDTH = 2048
Q_COLS = 2048
KV_COLS = 256
CONV_W = 3
EPS = 1e-6
LANES = 128
LOG2_E = math.log2(math.e)

MIB = 1024 * 1024


def _params(semantics, vmem_mib):
    return pltpu.CompilerParams(
        dimension_semantics=semantics, vmem_limit_bytes=vmem_mib * MIB
    )


def _rms(x, g):
    ms = jnp.mean(x * x, axis=-1, keepdims=True)
    return x * lax.rsqrt(ms + EPS) * g


def _split_specs(tm, d, n_prompt_tiles):
    return [
        pl.BlockSpec((tm, d), lambda i: (jnp.minimum(i, n_prompt_tiles - 1), 0)),
        pl.BlockSpec((tm, d), lambda i: (0, 0)),
    ]


def _load_rows(refs, n_prompt_tiles):
    if len(refs) == 1:
        return refs[0][...]
    is_prompt = pl.program_id(0) < n_prompt_tiles
    return jnp.where(is_prompt, refs[0][...], refs[1][...])


def _store_rows(refs, value, n_prompt_tiles):
    if len(refs) == 1:
        refs[0][...] = value
        return

    @pl.when(pl.program_id(0) < n_prompt_tiles)
    def _():
        refs[0][...] = value

    @pl.when(pl.program_id(0) >= n_prompt_tiles)
    def _():
        refs[1][...] = value


def _rms_kernel(*refs, n_in, n_prompt_tiles):
    x = _load_rows(refs[:n_in], n_prompt_tiles)
    g_ref, o_ref = refs[n_in:]
    o_ref[...] = _rms(x, g_ref[...]).astype(o_ref.dtype)


def rms_rows(x_parts, g, *, tm):
    d = x_parts[0].shape[1]
    m = sum(p.shape[0] for p in x_parts)
    npt = x_parts[0].shape[0] // tm
    row = pl.BlockSpec((tm, d), lambda i: (i, 0))
    x_specs = [row] if len(x_parts) == 1 else _split_specs(tm, d, npt)
    return pl.pallas_call(
        functools.partial(_rms_kernel, n_in=len(x_parts), n_prompt_tiles=npt),
        grid=(m // tm,),
        in_specs=x_specs + [pl.BlockSpec((1, d), lambda i: (0, 0))],
        out_specs=row,
        out_shape=jax.ShapeDtypeStruct((m, d), BF16),
        compiler_params=_params(("arbitrary",), 40),
        name="rms_rows",
    )(*x_parts, g)


def _post_kernel(*refs, n_in, n_out, has_next, n_prompt_tiles):
    m_ref = refs[0]
    x = _load_rows(refs[1:1 + n_in], n_prompt_tiles)
    rest = refs[1 + n_in:]
    xn = x + _rms(m_ref[...].astype(F32), rest[0][...])
    rest = rest[1:]
    if has_next:
        gnext_ref, rest = rest[0], rest[1:]
    _store_rows(rest[:n_out], xn, n_prompt_tiles)
    if has_next:
        ho_ref = rest[n_out]
        ho_ref[...] = _rms(xn, gnext_ref[...]).astype(ho_ref.dtype)


def post_rows(m_in, x_parts, g_post, g_next, *, tm, split_out=None):
    m, d = m_in.shape
    npt = (x_parts[0].shape[0] if split_out is None else split_out[0]) // tm
    if len(x_parts) == 1 and split_out is None:
        npt = m // tm
    row = pl.BlockSpec((tm, d), lambda i: (i, 0))
    vec = pl.BlockSpec((1, d), lambda i: (0, 0))
    x_specs = [row] if len(x_parts) == 1 else _split_specs(tm, d, npt)
    if split_out is None:
        xo_specs = [row]
        xo_shapes = [jax.ShapeDtypeStruct((m, d), F32)]
    else:
        xo_specs = _split_specs(tm, d, npt)
        xo_shapes = [jax.ShapeDtypeStruct((r, d), F32) for r in split_out]
    has_next = g_next is not None
    gains = [g_post, g_next] if has_next else [g_post]
    out = pl.pallas_call(
        functools.partial(_post_kernel, n_in=len(x_parts), n_out=len(xo_specs),
                          has_next=has_next, n_prompt_tiles=npt),
        grid=(m // tm,),
        in_specs=[row] + x_specs + [vec] * len(gains),
        out_specs=xo_specs + ([row] if has_next else []),
        out_shape=xo_shapes
        + ([jax.ShapeDtypeStruct((m, d), BF16)] if has_next else []),
        compiler_params=_params(("arbitrary",), 48),
        name="post_rows",
    )(m_in, *x_parts, *gains)
    x_new = out[:len(xo_specs)]
    return (x_new, out[-1]) if has_next else (x_new, None)


def _grid_step():
    return pl.program_id(0) * pl.num_programs(1) + pl.program_id(1)


def _side_cast(src_ref, dst_ref, n_blocks):
    @pl.when(_grid_step() < n_blocks)
    def _():
        dst_ref[...] = src_ref[...].astype(dst_ref.dtype)


def _side_cast_specs(src, layer, block_rows, n_steps, n_cols_grid):
    _, rows, cols = src.shape
    n_blocks = rows // block_rows
    assert rows % block_rows == 0 and n_blocks <= n_steps

    def block(i, j):
        return jnp.minimum(i * n_cols_grid + j, n_blocks - 1)

    in_spec = pl.BlockSpec((None, block_rows, cols), lambda i, j: (layer, block(i, j), 0))
    out_spec = pl.BlockSpec((block_rows, cols), lambda i, j: (block(i, j), 0))
    return in_spec, out_spec, jax.ShapeDtypeStruct((rows, cols), BF16), n_blocks


def _mm_kernel(a_ref, b_ref, *rest, epilogue, cast_blocks):
    o_ref = rest[-2] if cast_blocks else rest[-1]
    acc = jnp.dot(a_ref[...], b_ref[...], preferred_element_type=F32)
    o_ref[...] = epilogue(acc).astype(o_ref.dtype)
    if cast_blocks:
        _side_cast(rest[0], rest[-1], cast_blocks)


def _mm2_kernel(a1_ref, a2_ref, b1_ref, b2_ref, *rest, cast_blocks):
    o_ref = rest[-2] if cast_blocks else rest[-1]
    acc = jnp.dot(a1_ref[...], b1_ref[...], preferred_element_type=F32)
    acc = acc + jnp.dot(a2_ref[...], b2_ref[...], preferred_element_type=F32)
    o_ref[...] = acc.astype(o_ref.dtype)
    if cast_blocks:
        _side_cast(rest[0], rest[-1], cast_blocks)


def _with_side_cast(cast, grid, in_specs, out_specs, out_shapes, operands):
    if cast is None:
        return 0
    src, layer, block_rows = cast
    i_spec, o_spec, o_shape, n_blocks = _side_cast_specs(
        src, layer, block_rows, grid[0] * grid[1], grid[1]
    )
    in_specs.append(i_spec)
    out_specs.append(o_spec)
    out_shapes.append(o_shape)
    operands.append(src)
    return n_blocks


def matmul_cols(a, w, layer, *, col_block0, n, tm, tn, out_dtype, epilogue, name,
                cast=None, vmem_mib=56):
    m, k = a.shape
    grid = (m // tm, n // tn)
    in_specs = [
        pl.BlockSpec((tm, k), lambda i, j: (i, 0)),
        pl.BlockSpec((None, k, tn), lambda i, j: (layer, 0, j + col_block0)),
    ]
    out_specs = [pl.BlockSpec((tm, tn), lambda i, j: (i, j))]
    out_shapes = [jax.ShapeDtypeStruct((m, n), out_dtype)]
    operands = [a, w]
    cast_blocks = _with_side_cast(cast, grid, in_specs, out_specs, out_shapes, operands)
    out = pl.pallas_call(
        functools.partial(_mm_kernel, epilogue=epilogue, cast_blocks=cast_blocks),
        grid=grid,
        in_specs=in_specs,
        out_specs=out_specs,
        out_shape=out_shapes,
        compiler_params=_params(("arbitrary", "arbitrary"), vmem_mib),
        name=name,
    )(*operands)
    return out[0], (out[1] if cast_blocks else None)


def matmul_two(a1, a2, w, layer, *, tm, tn, out_dtype, name, cast=None):
    m, k1 = a1.shape
    k2 = a2.shape[1]
    assert k1 == k2
    n = w.shape[-1]
    grid = (m // tm, n // tn)
    in_specs = [
        pl.BlockSpec((tm, k1), lambda i, j: (i, 0)),
        pl.BlockSpec((tm, k2), lambda i, j: (i, 0)),
        pl.BlockSpec((None, None, k1, tn), lambda i, j: (layer, 0, 0, j)),
        pl.BlockSpec((None, None, k2, tn), lambda i, j: (layer, 1, 0, j)),
    ]
    out_specs = [pl.BlockSpec((tm, tn), lambda i, j: (i, j))]
    out_shapes = [jax.ShapeDtypeStruct((m, n), out_dtype)]
    operands = [a1, a2, w, w]
    cast_blocks = _with_side_cast(cast, grid, in_specs, out_specs, out_shapes, operands)
    out = pl.pallas_call(
        functools.partial(_mm2_kernel, cast_blocks=cast_blocks),
        grid=grid,
        in_specs=in_specs,
        out_specs=out_specs,
        out_shape=out_shapes,
        compiler_params=_params(("arbitrary", "arbitrary"), 56),
        name=name,
    )(*operands)
    return out[0], (out[1] if cast_blocks else None)


def _identity(x):
    return x


def _scale_q(x):
    return x * (HEAD_DIM ** -0.5 * LOG2_E)


_GELU_A = -2.0 * LOG2_E * math.sqrt(2.0 / math.pi)
_GELU_B = _GELU_A * 0.044715


def _gelu_times(x, y):
    t = x * x * _GELU_B + _GELU_A
    return (x * y) / (1.0 + jnp.exp2(t * x))


def _gelu(x):
    t = x * x * _GELU_B + _GELU_A
    return x / (1.0 + jnp.exp2(t * x))


def _softmax2_with_sink(s, sink):
    m = jnp.maximum(jnp.max(s, axis=-1, keepdims=True), sink)
    e = jnp.exp2(s - m)
    den = jnp.sum(e, axis=-1, keepdims=True) + jnp.exp2(sink - m)
    return e / den


def _attn_prompt_kernel(sink_ref, q_ref, kvc_ref, kvp_ref, os_ref, o_ref, *,
                        blocks_per_seq, n_prompt_steps, per_step):
    step = pl.program_id(0)
    w = WINDOW

    @pl.when(step >= n_prompt_steps)
    def _():
        o_ref[...] = os_ref[...]

    @pl.when(step < n_prompt_steps)
    def _():
        for b in range(per_step):
            rows = pl.ds(b * w, w)
            prev_ref = kvp_ref if b == 0 else kvc_ref.at[pl.ds((b - 1) * w, w)]
            _attn_prompt_block(sink_ref, q_ref.at[rows], kvc_ref.at[rows], prev_ref,
                               o_ref.at[rows], (step * per_step + b) % blocks_per_seq)


def _attn_prompt_block(sink_ref, q_ref, kvc_ref, kvp_ref, o_ref, n):
    w = WINDOW
    row = lax.broadcasted_iota(jnp.int32, (w, 2 * w), 0)
    col = lax.broadcasted_iota(jnp.int32, (w, 2 * w), 1)
    prev_ok = jnp.logical_and(jnp.logical_and(col < w, col > row), n > 0)
    own_ok = jnp.logical_and(col >= w, col - w <= row)
    valid = jnp.logical_or(prev_ok, own_ok)
    for kvh in range(N_KV_HEADS):
        ks = slice(kvh * HEAD_DIM, (kvh + 1) * HEAD_DIM)
        vs = slice(KV_COLS + kvh * HEAD_DIM, KV_COLS + (kvh + 1) * HEAD_DIM)
        k = jnp.concatenate([kvp_ref[:, ks], kvc_ref[:, ks]], axis=0).astype(BF16)
        v = jnp.concatenate([kvp_ref[:, vs], kvc_ref[:, vs]], axis=0).astype(BF16)
        v_ext = jnp.concatenate(
            [v, jnp.zeros((2 * w, LANES - HEAD_DIM), BF16), jnp.ones((2 * w, LANES), BF16)],
            axis=1,
        )
        heads = [kvh * GQA_GROUP + g for g in range(GQA_GROUP)]
        qs = jnp.concatenate(
            [q_ref[:, h * HEAD_DIM:(h + 1) * HEAD_DIM] for h in heads], axis=0
        )
        s = lax.dot_general(
            qs, k, (((1,), (1,)), ((), ())), preferred_element_type=F32
        )
        es, sink_terms = [], []
        for g, h in enumerate(heads):
            sink = sink_ref[h] * LOG2_E
            sg = jnp.where(valid, s[g * w:(g + 1) * w], -jnp.inf)
            m = jnp.maximum(jnp.max(sg, axis=-1, keepdims=True), sink)
            es.append(jnp.exp2(sg - m).astype(BF16))
            sink_terms.append(jnp.exp2(sink - m))
        o = jnp.dot(jnp.concatenate(es, axis=0), v_ext, preferred_element_type=F32)
        for g, h in enumerate(heads):
            og = o[g * w:(g + 1) * w]
            out = og[:, :LANES] / (og[:, LANES:] + sink_terms[g])
            o_ref[:, h * HEAD_DIM:(h + 1) * HEAD_DIM] = out[:, :HEAD_DIM].astype(
                o_ref.dtype
            )


def attn_prompt(q, kv, sinks, att_sample, *, m_total, m_prompt, seq, per_step):
    w = WINDOW
    rows = per_step * w
    assert m_prompt % rows == 0 and att_sample.shape[0] % rows == 0
    nsteps = m_prompt // rows
    last = nsteps - 1
    last_prev = m_prompt // w - 1
    return pl.pallas_call(
        functools.partial(_attn_prompt_kernel, blocks_per_seq=seq // w,
                          n_prompt_steps=nsteps, per_step=per_step),
        grid=(m_total // rows,),
        in_specs=[
            pl.BlockSpec(memory_space=pltpu.SMEM),
            pl.BlockSpec((rows, Q_COLS), lambda r: (jnp.minimum(r, last), 0)),
            pl.BlockSpec((rows, 2 * KV_COLS), lambda r: (jnp.minimum(r, last), 0)),
            pl.BlockSpec((w, 2 * KV_COLS),
                         lambda r: (jnp.clip(per_step * r - 1, 0, last_prev), 0)),
            pl.BlockSpec((rows, Q_COLS), lambda r: (jnp.maximum(r - nsteps, 0), 0)),
        ],
        out_specs=pl.BlockSpec((rows, Q_COLS), lambda r: (r, 0)),
        out_shape=jax.ShapeDtypeStruct((m_total, Q_COLS), BF16),
        compiler_params=_params(("arbitrary",), 40),
        name="attn_prompt",
    )(sinks, q, kv, kv, att_sample)


def _attn_sample_kernel(sink_ref, q_ref, k_ref, v_ref, o_ref, *, t_new):
    nb, nkv, rows, d = q_ref.shape
    keys = k_ref.shape[2]
    q = q_ref[...].reshape(nb * nkv, rows, d)
    k = k_ref[...].reshape(nb * nkv, keys, d)
    v = v_ref[...].reshape(nb * nkv, keys, d)
    s = jnp.einsum("bqd,bkd->bqk", q, k, preferred_element_type=F32)
    t = lax.broadcasted_iota(jnp.int32, (rows, keys), 0) // GQA_GROUP
    col = lax.broadcasted_iota(jnp.int32, (rows, keys), 1)
    valid = jnp.logical_and(col > t, col <= WINDOW + t)
    s = jnp.where(valid[None], s, -jnp.inf)
    sink = jnp.broadcast_to(sink_ref[...][None], (nb, nkv, rows, 1)).reshape(
        nb * nkv, rows, 1
    )
    p = _softmax2_with_sink(s, sink * LOG2_E).astype(BF16)
    o = jnp.einsum("bqk,bkd->bqd", p, v, preferred_element_type=F32)
    o_ref[...] = o.reshape(nb, nkv, rows, d).astype(o_ref.dtype)


def attn_sample(q_s, k_new, v_new, k_buf, v_buf, sinks, *, nb):
    b, l = k_buf.shape[0], k_buf.shape[1]
    t_new = q_s.shape[0] // b
    rows = t_new * GQA_GROUP
    keys = 2 * WINDOW
    q4 = q_s.reshape(b, t_new, N_KV_HEADS, GQA_GROUP, HEAD_DIM)
    q4 = q4.transpose(0, 2, 1, 3, 4).reshape(b, N_KV_HEADS, rows, HEAD_DIM)

    def keys_layout(buf, new):
        cat = jnp.concatenate(
            [buf, new.reshape(b, t_new, N_KV_HEADS, HEAD_DIM)], axis=1
        )
        cat = jnp.pad(cat, ((0, 0), (0, keys - l - t_new), (0, 0), (0, 0)))
        return cat.transpose(0, 2, 1, 3).astype(BF16)

    k4 = keys_layout(k_buf, k_new)
    v4 = keys_layout(v_buf, v_new)
    sink_rows = jnp.broadcast_to(
        sinks.reshape(N_KV_HEADS, 1, GQA_GROUP), (N_KV_HEADS, t_new, GQA_GROUP)
    ).reshape(N_KV_HEADS, rows, 1)
    o4 = pl.pallas_call(
        functools.partial(_attn_sample_kernel, t_new=t_new),
        grid=(b // nb,),
        in_specs=[
            pl.BlockSpec((N_KV_HEADS, rows, 1), lambda i: (0, 0, 0)),
            pl.BlockSpec((nb, N_KV_HEADS, rows, HEAD_DIM), lambda i: (i, 0, 0, 0)),
            pl.BlockSpec((nb, N_KV_HEADS, keys, HEAD_DIM), lambda i: (i, 0, 0, 0)),
            pl.BlockSpec((nb, N_KV_HEADS, keys, HEAD_DIM), lambda i: (i, 0, 0, 0)),
        ],
        out_specs=pl.BlockSpec(
            (nb, N_KV_HEADS, rows, HEAD_DIM), lambda i: (i, 0, 0, 0)
        ),
        out_shape=jax.ShapeDtypeStruct((b, N_KV_HEADS, rows, HEAD_DIM), BF16),
        compiler_params=_params(("parallel",), 40),
        name="attn_sample",
    )(sink_rows, q4, k4, v4)
    o = o4.reshape(b, N_KV_HEADS, t_new, GQA_GROUP, HEAD_DIM).transpose(0, 2, 1, 3, 4)
    return o.reshape(b * t_new, Q_COLS)


def _place_rows_kernel(rows_ref, buf_ref, o_ref):
    del buf_ref
    o_ref[...] = rows_ref[...]


def place_rows(buf, rows, *, row_block):
    r, c = rows.shape
    return pl.pallas_call(
        _place_rows_kernel,
        grid=(1,),
        in_specs=[
            pl.BlockSpec((r, c), lambda i: (0, 0)),
            pl.BlockSpec(memory_space=pl.ANY),
        ],
        out_specs=pl.BlockSpec((r, c), lambda i: (row_block, 0)),
        out_shape=jax.ShapeDtypeStruct(buf.shape, buf.dtype),
        input_output_aliases={1: 0},
        compiler_params=_params(("arbitrary",), 16),
        name="place_rows",
    )(rows, buf)


def _gmlp_prompt_kernel(u_ref, v_ref, w_ref, bt_ref, gn_ref, os_ref, o_ref, *,
                        n_prompt_steps, per_step):
    @pl.when(pl.program_id(0) >= n_prompt_steps)
    def _():
        o_ref[...] = os_ref[...]

    @pl.when(pl.program_id(0) < n_prompt_steps)
    def _():
        for b in range(per_step):
            rows = pl.ds(b * CHUNK, CHUNK)
            _gmlp_prompt_chunk(u_ref.at[rows], v_ref.at[rows], w_ref, bt_ref, gn_ref,
                               o_ref.at[rows])


def _gmlp_prompt_chunk(u_ref, v_ref, w_ref, bt_ref, gn_ref, o_ref):
    gv = _rms(v_ref[...].astype(F32), gn_ref[...])
    c = CHUNK
    row = lax.broadcasted_iota(jnp.int32, (c, c), 0)
    col = lax.broadcasted_iota(jnp.int32, (c, c), 1)
    causal = col <= row
    for h in range(GM_HEADS):
        cs = slice(h * GM_HEAD_DIM, (h + 1) * GM_HEAD_DIM)
        w = jnp.where(causal, w_ref[h], 0.0).astype(BF16)
        mixed = jnp.dot(w, gv[:, cs].astype(BF16), preferred_element_type=F32)
        mixed = mixed + bt_ref[:, h:h + 1]
        o_ref[:, cs] = (u_ref[:, cs].astype(F32) * mixed).astype(o_ref.dtype)


def gmlp_prompt(gm, w_s, b_t, g_norm, gated_sample, *, m_total, m_prompt, per_step):
    c = CHUNK
    rows = per_step * c
    assert m_prompt % rows == 0 and gated_sample.shape[0] % rows == 0
    nsteps = m_prompt // rows
    last = nsteps - 1
    return pl.pallas_call(
        functools.partial(_gmlp_prompt_kernel, n_prompt_steps=nsteps, per_step=per_step),
        grid=(m_total // rows,),
        in_specs=[
            pl.BlockSpec((rows, GM_WIDTH), lambda r: (jnp.minimum(r, last), 0)),
            pl.BlockSpec((rows, GM_WIDTH), lambda r: (jnp.minimum(r, last), 1)),
            pl.BlockSpec((GM_HEADS, c, c), lambda r: (0, 0, 0)),
            pl.BlockSpec((c, GM_HEADS), lambda r: (0, 0)),
            pl.BlockSpec((1, GM_WIDTH), lambda r: (0, 0)),
            pl.BlockSpec((rows, GM_WIDTH), lambda r: (jnp.maximum(r - nsteps, 0), 0)),
        ],
        out_specs=pl.BlockSpec((rows, GM_WIDTH), lambda r: (r, 0)),
        out_shape=jax.ShapeDtypeStruct((m_total, GM_WIDTH), BF16),
        compiler_params=_params(("arbitrary",), 40),
        name="gmlp_prompt",
    )(gm, gm, w_s, b_t, g_norm, gated_sample)


def _gmlp_sample_kernel(u_ref, v_ref, wc_ref, bc_ref, gn_ref, o_ref, gv_ref, *, t_new):
    rows = u_ref.shape[0]
    nseq = rows // t_new
    gv = _rms(v_ref[...].astype(F32), gn_ref[...])
    gv_ref[...] = gv
    t = lax.broadcasted_iota(jnp.int32, (t_new, GM_HEAD_DIM), 0)
    for h in range(GM_HEADS):
        cs = slice(h * GM_HEAD_DIM, (h + 1) * GM_HEAD_DIM)
        x = gv[:, cs].reshape(nseq, t_new, GM_HEAD_DIM)
        acc = jnp.zeros((nseq, t_new, GM_HEAD_DIM), F32)
        for s in range(t_new):
            coef = jnp.where(t >= s, wc_ref[h, s], 0.0)
            acc = acc + coef[None] * x[:, s:s + 1, :]
        mixed = acc + bc_ref[h][None]
        u = u_ref[:, cs].astype(F32).reshape(nseq, t_new, GM_HEAD_DIM)
        o_ref[:, cs] = (u * mixed).reshape(rows, GM_HEAD_DIM).astype(o_ref.dtype)


def gmlp_sample(gm, w_s, b_s, g_norm, *, m_prompt, m_sample, t_new):
    lanes = GM_HEAD_DIM
    w8 = w_s[:, :t_new, :t_new]
    wc = jnp.broadcast_to(
        w8.transpose(0, 2, 1)[..., None], (GM_HEADS, t_new, t_new, lanes)
    )
    bc = jnp.broadcast_to(b_s[:, :t_new, None], (GM_HEADS, t_new, lanes))
    blk = m_prompt // m_sample
    return pl.pallas_call(
        functools.partial(_gmlp_sample_kernel, t_new=t_new),
        grid=(1,),
        in_specs=[
            pl.BlockSpec((m_sample, GM_WIDTH), lambda i: (blk, 0)),
            pl.BlockSpec((m_sample, GM_WIDTH), lambda i: (blk, 1)),
            pl.BlockSpec((GM_HEADS, t_new, t_new, lanes), lambda i: (0, 0, 0, 0)),
            pl.BlockSpec((GM_HEADS, t_new, lanes), lambda i: (0, 0, 0)),
            pl.BlockSpec((1, GM_WIDTH), lambda i: (0, 0)),
        ],
        out_specs=[
            pl.BlockSpec((m_sample, GM_WIDTH), lambda i: (0, 0)),
            pl.BlockSpec((m_sample, GM_WIDTH), lambda i: (0, 0)),
        ],
        out_shape=[
            jax.ShapeDtypeStruct((m_sample, GM_WIDTH), BF16),
            jax.ShapeDtypeStruct((m_sample, GM_WIDTH), F32),
        ],
        compiler_params=_params(("arbitrary",), 40),
        name="gmlp_sample",
    )(gm, gm, wc, bc, g_norm)


def _conv_gelu_gate(g, g1, g2, up, cw_ref, cb_ref):
    c = cb_ref[...] + cw_ref[0:1, :] * g2
    c = c + cw_ref[1:2, :] * g1
    c = c + cw_ref[2:3, :] * g
    return _gelu_times(c, up)


def _ffn_a_kernel(h_ref, hs_ref, wg_ref, wu_ref, cw_ref, cb_ref, st_ref, wd_ref, a_ref,
                  tail_ref, as_ref, gs_ref, wdb_ref, gx_ref, *, chunk, t_new,
                  cast_blocks):
    i = pl.program_id(0)
    _side_cast(wd_ref, wdb_ref, cast_blocks)
    tm = h_ref.shape[0]
    tf = wg_ref.shape[1]
    wg = wg_ref[...].astype(BF16)
    wu = wu_ref[...].astype(BF16)

    prev8 = jnp.zeros((8, tf), F32)
    for r0 in range(0, tm, chunk):
        h = h_ref[r0:r0 + chunk, :]
        g = jnp.dot(h, wg, preferred_element_type=F32)
        up = jnp.dot(h, wu, preferred_element_type=F32)
        gx_ref[0:8, :] = prev8
        gx_ref[8:8 + chunk, :] = g
        g1 = gx_ref[7:7 + chunk, :]
        g2 = gx_ref[6:6 + chunk, :]
        a_ref[r0:r0 + chunk, :] = _conv_gelu_gate(g, g1, g2, up, cw_ref, cb_ref).astype(
            a_ref.dtype
        )
        prev8 = g[chunk - 8:, :]
    tail_ref[...] = prev8

    @pl.when(i == pl.num_programs(0) - 1)
    def _():
        hs = hs_ref[...]
        rows = hs.shape[0]
        g = jnp.dot(hs, wg, preferred_element_type=F32)
        up = jnp.dot(hs, wu, preferred_element_type=F32)
        gs_ref[...] = g
        t = lax.broadcasted_iota(jnp.int32, g.shape, 0) % t_new
        st2 = st_ref[...]
        st1 = pltpu.roll(st2, rows - 1, 0)
        g1 = jnp.where(t >= 1, pltpu.roll(g, 1, 0), st1)
        g2 = jnp.where(t >= 2, pltpu.roll(g, 2, 0), st2)
        as_ref[...] = _conv_gelu_gate(g, g1, g2, up, cw_ref, cb_ref).astype(as_ref.dtype)


def ffn_a(h, w_gate, w_up, w_down, conv_w, conv_b, conv_state, layer, *, m_prompt,
          m_sample, seq, t_new, tf, chunk, cast_rows):
    m_total, d = h.shape
    f = w_gate.shape[-1]
    nf = f // tf
    ni = m_prompt // seq
    assert t_new % 8 == 0 and t_new >= CONV_W - 1
    st = jnp.pad(conv_state, ((0, 0), (0, t_new - (CONV_W - 1)), (0, 0)))
    st = st.reshape(m_sample, f)
    sample_col = lambda i, j: (0, jnp.where(i == ni - 1, j, 0))
    wd_in, wd_out, wd_shape, cast_blocks = _side_cast_specs(
        w_down, layer, cast_rows, ni * nf, nf
    )
    return pl.pallas_call(
        functools.partial(_ffn_a_kernel, chunk=chunk, t_new=t_new,
                          cast_blocks=cast_blocks),
        grid=(ni, nf),
        in_specs=[
            pl.BlockSpec((seq, d), lambda i, j: (i, 0), pipeline_mode=pl.Buffered(1)),
            pl.BlockSpec((m_sample, d), lambda i, j: (m_prompt // m_sample, 0)),
            pl.BlockSpec((None, d, tf), lambda i, j: (layer, 0, j)),
            pl.BlockSpec((None, d, tf), lambda i, j: (layer, 0, j)),
            pl.BlockSpec((None, CONV_W, tf), lambda i, j: (layer, 0, j)),
            pl.BlockSpec((None, 1, tf), lambda i, j: (layer, 0, j)),
            pl.BlockSpec((m_sample, tf), lambda i, j: (0, j)),
            wd_in,
        ],
        out_specs=[
            pl.BlockSpec((seq, tf), lambda i, j: (i, j)),
            pl.BlockSpec((None, 8, tf), lambda i, j: (i, 0, j)),
            pl.BlockSpec((m_sample, tf), sample_col),
            pl.BlockSpec((m_sample, tf), sample_col),
            wd_out,
        ],
        out_shape=[
            jax.ShapeDtypeStruct((m_total, f), BF16),
            jax.ShapeDtypeStruct((ni, 8, f), F32),
            jax.ShapeDtypeStruct((m_sample, f), BF16),
            jax.ShapeDtypeStruct((m_sample, f), F32),
            wd_shape,
        ],
        scratch_shapes=[pltpu.VMEM((chunk + 8, tf), F32)],
        compiler_params=_params(("arbitrary", "arbitrary"), 56),
        name="ffn_a",
    )(h, h, w_gate, w_up, conv_w, conv_b, st, w_down)


def kernel(x_prompt, x_sample, state_swa_k, state_swa_v, state_conv, w_in, w_out,
           attn_sinks, gm_spatial, gm_bias, gm_norm, norm_pre_mix, norm_post_mix,
           norm_pre_ffn, norm_post_ffn, w_ffn_gate, w_ffn_up, w_ffn_down, conv_w,
           conv_b):
    batch, seq, d = x_prompt.shape
    dec_batch, dec_seq, _ = x_sample.shape
    depth = w_in.shape[0]
    f = w_ffn_gate.shape[-1]
    mp = batch * seq
    ms = dec_batch * dec_seq
    m = mp + ms
    keep = min(WINDOW, seq)

    tm_mm = m // 8
    tm_down = m // 12
    tf = 256
    tm_rows = ms
    assert mp % tm_rows == 0

    w_in_b = w_in[0].astype(BF16)[None]
    conv_b3 = conv_b.reshape(depth, 1, f)
    cast_rows = 64

    x = (x_prompt.reshape(mp, d), x_sample.reshape(ms, d))
    h = rms_rows(x, norm_pre_mix[0].reshape(1, d), tm=tm_rows)

    p_k, p_v, p_c, s_k, s_v, s_c, s_g = [], [], [], [], [], [], []
    for l in range(depth):
        last = l + 1 == depth
        tn = 512
        q, _ = matmul_cols(h, w_in_b, 0, col_block0=0, n=Q_COLS, tm=tm_mm, tn=2 * tn,
                           out_dtype=BF16, epilogue=_scale_q, name="mm_q")
        kv, _ = matmul_cols(h, w_in_b, 0, col_block0=Q_COLS // tn, n=2 * KV_COLS,
                            tm=tm_mm, tn=tn, out_dtype=F32, epilogue=_identity,
                            name="mm_kv")
        gm, w_out_b = matmul_cols(
            h, w_in_b, 0, col_block0=(Q_COLS + 2 * KV_COLS) // tn, n=2 * GM_WIDTH,
            tm=tm_mm, tn=tn, out_dtype=BF16, epilogue=_gelu, name="mm_gm",
            cast=(w_out, l, cast_rows))
        w_out_b = w_out_b.reshape(1, 2, Q_COLS, d)

        kv_p = jnp.stack([kv[(b + 1) * seq - keep:(b + 1) * seq] for b in range(batch)])
        kv_s = kv[mp:]
        k_s, v_s = kv_s[:, :KV_COLS], kv_s[:, KV_COLS:]
        p_k.append(kv_p[..., :KV_COLS].reshape(batch, keep, N_KV_HEADS, HEAD_DIM))
        p_v.append(kv_p[..., KV_COLS:].reshape(batch, keep, N_KV_HEADS, HEAD_DIM))
        s_k.append(k_s.reshape(dec_batch, dec_seq, N_KV_HEADS, HEAD_DIM))
        s_v.append(v_s.reshape(dec_batch, dec_seq, N_KV_HEADS, HEAD_DIM))

        att_s = attn_sample(q[mp:], k_s, v_s, state_swa_k[l], state_swa_v[l],
                            attn_sinks[l], nb=8)
        att = attn_prompt(q, kv, attn_sinks[l], att_s, m_total=m, m_prompt=mp, seq=seq,
                          per_step=2)

        gmo_s, gv_s = gmlp_sample(gm, gm_spatial[l], gm_bias[l], gm_norm[l].reshape(1, -1),
                                  m_prompt=mp, m_sample=ms, t_new=dec_seq)
        gmo = gmlp_prompt(gm, gm_spatial[l], gm_bias[l].T, gm_norm[l].reshape(1, -1),
                          gmo_s, m_total=m, m_prompt=mp, per_step=2)
        s_g.append(gv_s.reshape(dec_batch, dec_seq, GM_HEADS, GM_HEAD_DIM))

        mix, w_in_next = matmul_two(att, gmo, w_out_b, 0, tm=tm_mm, tn=1024,
                                    out_dtype=BF16, name="mm_out",
                                    cast=None if last else (w_in, l + 1, 2 * cast_rows))
        x, h = post_rows(mix, x, norm_post_mix[l].reshape(1, d),
                         norm_pre_ffn[l].reshape(1, d), tm=tm_rows)

        a, tail, a_s, g_s, w_down_b = ffn_a(
            h, w_ffn_gate, w_ffn_up, w_ffn_down, conv_w, conv_b3, state_conv[l], l,
            m_prompt=mp, m_sample=ms, seq=seq, t_new=dec_seq, tf=tf, chunk=256,
            cast_rows=2 * cast_rows)
        a = place_rows(a, a_s, row_block=mp // ms)
        p_c.append(tail[:, 8 - (CONV_W - 1):])
        s_c.append(g_s.reshape(dec_batch, dec_seq, f)[:, dec_seq - (CONV_W - 1):])

        ffn, _ = matmul_cols(a, w_down_b[None], 0, col_block0=0, n=d, tm=tm_down, tn=512,
                             out_dtype=BF16, epilogue=_identity, name="mm_down",
                             vmem_mib=58)
        x, h = post_rows(ffn, x, norm_post_ffn[l].reshape(1, d),
                         None if last else norm_pre_mix[l + 1].reshape(1, d),
                         tm=tm_rows, split_out=(mp, ms) if last else None)
        if not last:
            w_in_b = w_in_next[None]

    x_p, x_s = x
    return (x_p.reshape(batch, seq, d), x_s.reshape(dec_batch, dec_seq, d),
            jnp.stack(p_k), jnp.stack(p_v), jnp.stack(p_c),
            jnp.stack(s_k), jnp.stack(s_v), jnp.stack(s_c), jnp.stack(s_g))
```

```python
import functools
import math

import jax
import jax.numpy as jnp
from jax import lax
from jax.experimental import pallas as pl
from jax.experimental.pallas import tpu as pltpu

F32 = jnp.float32
BF16 = jnp.bfloat16

D_MODEL = 4096
HEAD_DIM = 64
N_HEADS = 32
N_KV_HEADS = 4
GQA_GROUP = 8
WINDOW = 128
CHUNK = 128
GM_HEAD_DIM = 128
GM_HEADS = 16
GM_WIDTH = 2048
Q_COLS = 2048
KV_COLS = 256
CONV_W = 3
EPS = 1e-6
LANES = 128
LOG2_E = math.log2(math.e)

MIB = 1024 * 1024


def _params(semantics, vmem_mib):
    return pltpu.CompilerParams(
        dimension_semantics=semantics, vmem_limit_bytes=vmem_mib * MIB
    )


def _rms(x, g):
    ms = jnp.mean(x * x, axis=-1, keepdims=True)
    return x * lax.rsqrt(ms + EPS) * g


def _split_specs(tm, d, n_prompt_tiles):
    return [
        pl.BlockSpec((tm, d), lambda i: (jnp.minimum(i, n_prompt_tiles - 1), 0)),
        pl.BlockSpec((tm, d), lambda i: (0, 0)),
    ]


def _load_rows(refs, n_prompt_tiles):
    if len(refs) == 1:
        return refs[0][...]
    is_prompt = pl.program_id(0) < n_prompt_tiles
    return jnp.where(is_prompt, refs[0][...], refs[1][...])


def _store_rows(refs, value, n_prompt_tiles):
    if len(refs) == 1:
        refs[0][...] = value
        return

    @pl.when(pl.program_id(0) < n_prompt_tiles)
    def _():
        refs[0][...] = value

    @pl.when(pl.program_id(0) >= n_prompt_tiles)
    def _():
        refs[1][...] = value


def _rms_kernel(*refs, n_in, n_prompt_tiles):
    x = _load_rows(refs[:n_in], n_prompt_tiles)
    g_ref, o_ref = refs[n_in:]
    o_ref[...] = _rms(x, g_ref[...]).astype(o_ref.dtype)


def rms_rows(x_parts, g, *, tm):
    d = x_parts[0].shape[1]
    m = sum(p.shape[0] for p in x_parts)
    npt = x_parts[0].shape[0] // tm
    row = pl.BlockSpec((tm, d), lambda i: (i, 0))
    x_specs = [row] if len(x_parts) == 1 else _split_specs(tm, d, npt)
    return pl.pallas_call(
        functools.partial(_rms_kernel, n_in=len(x_parts), n_prompt_tiles=npt),
        grid=(m // tm,),
        in_specs=x_specs + [pl.BlockSpec((1, d), lambda i: (0, 0))],
        out_specs=row,
        out_shape=jax.ShapeDtypeStruct((m, d), BF16),
        compiler_params=_params(("arbitrary",), 40),
        name="rms_rows",
    )(*x_parts, g)


def _post_kernel(*refs, n_in, n_out, has_next, n_prompt_tiles):
    m_ref = refs[0]
    x = _load_rows(refs[1:1 + n_in], n_prompt_tiles)
    rest = refs[1 + n_in:]
    xn = x + _rms(m_ref[...].astype(F32), rest[0][...])
    rest = rest[1:]
    if has_next:
        gnext_ref, rest = rest[0], rest[1:]
    _store_rows(rest[:n_out], xn, n_prompt_tiles)
    if has_next:
        ho_ref = rest[n_out]
        ho_ref[...] = _rms(xn, gnext_ref[...]).astype(ho_ref.dtype)


def post_rows(m_in, x_parts, g_post, g_next, *, tm, split_out=None):
    m, d = m_in.shape
    npt = (x_parts[0].shape[0] if split_out is None else split_out[0]) // tm
    if len(x_parts) == 1 and split_out is None:
        npt = m // tm
    row = pl.BlockSpec((tm, d), lambda i: (i, 0))
    vec = pl.BlockSpec((1, d), lambda i: (0, 0))
    x_specs = [row] if len(x_parts) == 1 else _split_specs(tm, d, npt)
    if split_out is None:
        xo_specs = [row]
        xo_shapes = [jax.ShapeDtypeStruct((m, d), F32)]
    else:
        xo_specs = _split_specs(tm, d, npt)
        xo_shapes = [jax.ShapeDtypeStruct((r, d), F32) for r in split_out]
    has_next = g_next is not None
    gains = [g_post, g_next] if has_next else [g_post]
    out = pl.pallas_call(
        functools.partial(_post_kernel, n_in=len(x_parts), n_out=len(xo_specs),
                          has_next=has_next, n_prompt_tiles=npt),
        grid=(m // tm,),
        in_specs=[row] + x_specs + [vec] * len(gains),
        out_specs=xo_specs + ([row] if has_next else []),
        out_shape=xo_shapes
        + ([jax.ShapeDtypeStruct((m, d), BF16)] if has_next else []),
        compiler_params=_params(("arbitrary",), 48),
        name="post_rows",
    )(m_in, *x_parts, *gains)
    x_new = out[:len(xo_specs)]
    return (x_new, out[-1]) if has_next else (x_new, None)


def _grid_step():
    return pl.program_id(0) * pl.num_programs(1) + pl.program_id(1)


def _side_cast(src_ref, dst_ref, n_blocks):
    @pl.when(_grid_step() < n_blocks)
    def _():
        dst_ref[...] = src_ref[...].astype(dst_ref.dtype)


def _side_cast_specs(src, layer, block_rows, n_steps, n_cols_grid):
    _, rows, cols = src.shape
    n_blocks = rows // block_rows
    assert rows % block_rows == 0 and n_blocks <= n_steps

    def block(i, j):
        return jnp.minimum(i * n_cols_grid + j, n_blocks - 1)

    in_spec = pl.BlockSpec((None, block_rows, cols), lambda i, j: (layer, block(i, j), 0))
    out_spec = pl.BlockSpec((block_rows, cols), lambda i, j: (block(i, j), 0))
    return in_spec, out_spec, jax.ShapeDtypeStruct((rows, cols), BF16), n_blocks


def _mm_kernel(a_ref, b_ref, *rest, epilogue, cast_blocks):
    o_ref = rest[-2] if cast_blocks else rest[-1]
    acc = jnp.dot(a_ref[...], b_ref[...].astype(BF16), preferred_element_type=F32)
    o_ref[...] = epilogue(acc).astype(o_ref.dtype)
    if cast_blocks:
        _side_cast(rest[0], rest[-1], cast_blocks)


def _mm2_kernel(a1_ref, a2_ref, b1_ref, b2_ref, *rest, cast_blocks):
    o_ref = rest[-2] if cast_blocks else rest[-1]
    acc = jnp.dot(a1_ref[...], b1_ref[...], preferred_element_type=F32)
    acc = acc + jnp.dot(a2_ref[...], b2_ref[...], preferred_element_type=F32)
    o_ref[...] = acc.astype(o_ref.dtype)
    if cast_blocks:
        _side_cast(rest[0], rest[-1], cast_blocks)


def _with_side_cast(cast, grid, in_specs, out_specs, out_shapes, operands):
    if cast is None:
        return 0
    src, layer, block_rows = cast
    i_spec, o_spec, o_shape, n_blocks = _side_cast_specs(
        src, layer, block_rows, grid[0] * grid[1], grid[1]
    )
    in_specs.append(i_spec)
    out_specs.append(o_spec)
    out_shapes.append(o_shape)
    operands.append(src)
    return n_blocks


def matmul_cols(a, w, layer, *, col_block0, n, tm, tn, out_dtype, epilogue, name,
                cast=None, vmem_mib=56):
    m, k = a.shape
    grid = (m // tm, n // tn)
    in_specs = [
        pl.BlockSpec((tm, k), lambda i, j: (i, 0)),
        pl.BlockSpec((None, k, tn), lambda i, j: (layer, 0, j + col_block0)),
    ]
    out_specs = [pl.BlockSpec((tm, tn), lambda i, j: (i, j))]
    out_shapes = [jax.ShapeDtypeStruct((m, n), out_dtype)]
    operands = [a, w]
    cast_blocks = _with_side_cast(cast, grid, in_specs, out_specs, out_shapes, operands)
    out = pl.pallas_call(
        functools.partial(_mm_kernel, epilogue=epilogue, cast_blocks=cast_blocks),
        grid=grid,
        in_specs=in_specs,
        out_specs=out_specs,
        out_shape=out_shapes,
        compiler_params=_params(("arbitrary", "arbitrary"), vmem_mib),
        name=name,
    )(*operands)
    return out[0], (out[1] if cast_blocks else None)


def matmul_two(a1, a2, w, layer, *, tm, tn, out_dtype, name, cast=None):
    m, k1 = a1.shape
    k2 = a2.shape[1]
    assert k1 == k2
    n = w.shape[-1]
    grid = (m // tm, n // tn)
    in_specs = [
        pl.BlockSpec((tm, k1), lambda i, j: (i, 0)),
        pl.BlockSpec((tm, k2), lambda i, j: (i, 0)),
        pl.BlockSpec((None, None, k1, tn), lambda i, j: (layer, 0, 0, j)),
        pl.BlockSpec((None, None, k2, tn), lambda i, j: (layer, 1, 0, j)),
    ]
    out_specs = [pl.BlockSpec((tm, tn), lambda i, j: (i, j))]
    out_shapes = [jax.ShapeDtypeStruct((m, n), out_dtype)]
    operands = [a1, a2, w, w]
    cast_blocks = _with_side_cast(cast, grid, in_specs, out_specs, out_shapes, operands)
    out = pl.pallas_call(
        functools.partial(_mm2_kernel, cast_blocks=cast_blocks),
        grid=grid,
        in_specs=in_specs,
        out_specs=out_specs,
        out_shape=out_shapes,
        compiler_params=_params(("arbitrary", "arbitrary"), 56),
        name=name,
    )(*operands)
    return out[0], (out[1] if cast_blocks else None)


def _identity(x):
    return x


def _scale_q(x):
    return x * (HEAD_DIM ** -0.5 * LOG2_E)


_GELU_A = -2.0 * LOG2_E * math.sqrt(2.0 / math.pi)
_GELU_B = _GELU_A * 0.044715


def _gelu_times(x, y):
    t = x * x * _GELU_B + _GELU_A
    return (x * y) / (1.0 + jnp.exp2(t * x))


def _gelu(x):
    t = x * x * _GELU_B + _GELU_A
    return x / (1.0 + jnp.exp2(t * x))


def _softmax2_with_sink(s, sink):
    m = jnp.maximum(jnp.max(s, axis=-1, keepdims=True), sink)
    e = jnp.exp2(s - m)
    den = jnp.sum(e, axis=-1, keepdims=True) + jnp.exp2(sink - m)
    return e / den


def _attn_prompt_kernel(sink_ref, q_ref, kvc_ref, kvp_ref, os_ref, o_ref, *,
                        blocks_per_seq, n_prompt_steps, per_step):
    step = pl.program_id(0)
    w = WINDOW

    @pl.when(step >= n_prompt_steps)
    def _():
        o_ref[...] = os_ref[...]

    @pl.when(step < n_prompt_steps)
    def _():
        for b in range(per_step):
            rows = pl.ds(b * w, w)
            prev_ref = kvp_ref if b == 0 else kvc_ref.at[pl.ds((b - 1) * w, w)]
            _attn_prompt_block(sink_ref, q_ref.at[rows], kvc_ref.at[rows], prev_ref,
                               o_ref.at[rows], (step * per_step + b) % blocks_per_seq)


def _attn_prompt_block(sink_ref, q_ref, kvc_ref, kvp_ref, o_ref, n):
    w = WINDOW
    row = lax.broadcasted_iota(jnp.int32, (w, 2 * w), 0)
    col = lax.broadcasted_iota(jnp.int32, (w, 2 * w), 1)
    prev_ok = jnp.logical_and(jnp.logical_and(col < w, col > row), n > 0)
    own_ok = jnp.logical_and(col >= w, col - w <= row)
    valid = jnp.logical_or(prev_ok, own_ok)
    for kvh in range(N_KV_HEADS):
        ks = slice(kvh * HEAD_DIM, (kvh + 1) * HEAD_DIM)
        vs = slice(KV_COLS + kvh * HEAD_DIM, KV_COLS + (kvh + 1) * HEAD_DIM)
        k = jnp.concatenate([kvp_ref[:, ks], kvc_ref[:, ks]], axis=0).astype(BF16)
        v = jnp.concatenate([kvp_ref[:, vs], kvc_ref[:, vs]], axis=0).astype(BF16)
        v_ext = jnp.concatenate(
            [v, jnp.zeros((2 * w, LANES - HEAD_DIM), BF16), jnp.ones((2 * w, LANES), BF16)],
            axis=1,
        )
        heads = [kvh * GQA_GROUP + g for g in range(GQA_GROUP)]
        qs = jnp.concatenate(
            [q_ref[:, h * HEAD_DIM:(h + 1) * HEAD_DIM] for h in heads], axis=0
        )
        s = lax.dot_general(
            qs, k, (((1,), (1,)), ((), ())), preferred_element_type=F32
        )
        es, sink_terms = [], []
        for g, h in enumerate(heads):
            sink = sink_ref[h] * LOG2_E
            sg = jnp.where(valid, s[g * w:(g + 1) * w], -jnp.inf)
            m = jnp.maximum(jnp.max(sg, axis=-1, keepdims=True), sink)
            es.append(jnp.exp2(sg - m).astype(BF16))
            sink_terms.append(jnp.exp2(sink - m))
        o = jnp.dot(jnp.concatenate(es, axis=0), v_ext, preferred_element_type=F32)
        for g, h in enumerate(heads):
            og = o[g * w:(g + 1) * w]
            out = og[:, :LANES] / (og[:, LANES:] + sink_terms[g])
            o_ref[:, h * HEAD_DIM:(h + 1) * HEAD_DIM] = out[:, :HEAD_DIM].astype(
                o_ref.dtype
            )


def attn_prompt(q, kv, sinks, att_sample, *, m_total, m_prompt, seq, per_step):
    w = WINDOW
    rows = per_step * w
    assert m_prompt % rows == 0 and att_sample.shape[0] % rows == 0
    nsteps = m_prompt // rows
    last = nsteps - 1
    last_prev = m_prompt // w - 1
    return pl.pallas_call(
        functools.partial(_attn_prompt_kernel, blocks_per_seq=seq // w,
                          n_prompt_steps=nsteps, per_step=per_step),
        grid=(m_total // rows,),
        in_specs=[
            pl.BlockSpec(memory_space=pltpu.SMEM),
            pl.BlockSpec((rows, Q_COLS), lambda r: (jnp.minimum(r, last), 0)),
            pl.BlockSpec((rows, 2 * KV_COLS), lambda r: (jnp.minimum(r, last), 0)),
            pl.BlockSpec((w, 2 * KV_COLS),
                         lambda r: (jnp.clip(per_step * r - 1, 0, last_prev), 0)),
            pl.BlockSpec((rows, Q_COLS), lambda r: (jnp.maximum(r - nsteps, 0), 0)),
        ],
        out_specs=pl.BlockSpec((rows, Q_COLS), lambda r: (r, 0)),
        out_shape=jax.ShapeDtypeStruct((m_total, Q_COLS), BF16),
        compiler_params=_params(("arbitrary",), 40),
        name="attn_prompt",
    )(sinks, q, kv, kv, att_sample)


def _attn_sample_kernel(sink_ref, q_ref, k_ref, v_ref, o_ref, *, t_new):
    nb, nkv, rows, d = q_ref.shape
    keys = k_ref.shape[2]
    q = q_ref[...].reshape(nb * nkv, rows, d)
    k = k_ref[...].reshape(nb * nkv, keys, d)
    v = v_ref[...].reshape(nb * nkv, keys, d)
    s = jnp.einsum("bqd,bkd->bqk", q, k, preferred_element_type=F32)
    t = lax.broadcasted_iota(jnp.int32, (rows, keys), 0) // GQA_GROUP
    col = lax.broadcasted_iota(jnp.int32, (rows, keys), 1)
    valid = jnp.logical_and(col > t, col <= WINDOW + t)
    s = jnp.where(valid[None], s, -jnp.inf)
    sink = jnp.broadcast_to(sink_ref[...][None], (nb, nkv, rows, 1)).reshape(
        nb * nkv, rows, 1
    )
    p = _softmax2_with_sink(s, sink * LOG2_E).astype(BF16)
    o = jnp.einsum("bqk,bkd->bqd", p, v, preferred_element_type=F32)
    o_ref[...] = o.reshape(nb, nkv, rows, d).astype(o_ref.dtype)


def attn_sample(q_s, k_new, v_new, k_buf, v_buf, sinks, *, nb):
    b, l = k_buf.shape[0], k_buf.shape[1]
    t_new = q_s.shape[0] // b
    rows = t_new * GQA_GROUP
    keys = 2 * WINDOW
    q4 = q_s.reshape(b, t_new, N_KV_HEADS, GQA_GROUP, HEAD_DIM)
    q4 = q4.transpose(0, 2, 1, 3, 4).reshape(b, N_KV_HEADS, rows, HEAD_DIM)

    def keys_layout(buf, new):
        cat = jnp.concatenate(
            [buf, new.reshape(b, t_new, N_KV_HEADS, HEAD_DIM)], axis=1
        )
        cat = jnp.pad(cat, ((0, 0), (0, keys - l - t_new), (0, 0), (0, 0)))
        return cat.transpose(0, 2, 1, 3).astype(BF16)

    k4 = keys_layout(k_buf, k_new)
    v4 = keys_layout(v_buf, v_new)
    sink_rows = jnp.broadcast_to(
        sinks.reshape(N_KV_HEADS, 1, GQA_GROUP), (N_KV_HEADS, t_new, GQA_GROUP)
    ).reshape(N_KV_HEADS, rows, 1)
    o4 = pl.pallas_call(
        functools.partial(_attn_sample_kernel, t_new=t_new),
        grid=(b // nb,),
        in_specs=[
            pl.BlockSpec((N_KV_HEADS, rows, 1), lambda i: (0, 0, 0)),
            pl.BlockSpec((nb, N_KV_HEADS, rows, HEAD_DIM), lambda i: (i, 0, 0, 0)),
            pl.BlockSpec((nb, N_KV_HEADS, keys, HEAD_DIM), lambda i: (i, 0, 0, 0)),
            pl.BlockSpec((nb, N_KV_HEADS, keys, HEAD_DIM), lambda i: (i, 0, 0, 0)),
        ],
        out_specs=pl.BlockSpec(
            (nb, N_KV_HEADS, rows, HEAD_DIM), lambda i: (i, 0, 0, 0)
        ),
        out_shape=jax.ShapeDtypeStruct((b, N_KV_HEADS, rows, HEAD_DIM), BF16),
        compiler_params=_params(("parallel",), 40),
        name="attn_sample",
    )(sink_rows, q4, k4, v4)
    o = o4.reshape(b, N_KV_HEADS, t_new, GQA_GROUP, HEAD_DIM).transpose(0, 2, 1, 3, 4)
    return o.reshape(b * t_new, Q_COLS)


def _place_rows_kernel(rows_ref, buf_ref, o_ref):
    del buf_ref
    o_ref[...] = rows_ref[...]


def place_rows(buf, rows, *, row_block):
    r, c = rows.shape
    return pl.pallas_call(
        _place_rows_kernel,
        grid=(1,),
        in_specs=[
            pl.BlockSpec((r, c), lambda i: (0, 0)),
            pl.BlockSpec(memory_space=pl.ANY),
        ],
        out_specs=pl.BlockSpec((r, c), lambda i: (row_block, 0)),
        out_shape=jax.ShapeDtypeStruct(buf.shape, buf.dtype),
        input_output_aliases={1: 0},
        compiler_params=_params(("arbitrary",), 16),
        name="place_rows",
    )(rows, buf)


def _gmlp_prompt_kernel(u_ref, v_ref, w_ref, bt_ref, gn_ref, os_ref, o_ref, *,
                        n_prompt_steps, per_step):
    @pl.when(pl.program_id(0) >= n_prompt_steps)
    def _():
        o_ref[...] = os_ref[...]

    @pl.when(pl.program_id(0) < n_prompt_steps)
    def _():
        for b in range(per_step):
            rows = pl.ds(b * CHUNK, CHUNK)
            _gmlp_prompt_chunk(u_ref.at[rows], v_ref.at[rows], w_ref, bt_ref, gn_ref,
                               o_ref.at[rows])


def _gmlp_prompt_chunk(u_ref, v_ref, w_ref, bt_ref, gn_ref, o_ref):
    gv = _rms(v_ref[...].astype(F32), gn_ref[...])
    c = CHUNK
    row = lax.broadcasted_iota(jnp.int32, (c, c), 0)
    col = lax.broadcasted_iota(jnp.int32, (c, c), 1)
    causal = col <= row
    for h in range(GM_HEADS):
        cs = slice(h * GM_HEAD_DIM, (h + 1) * GM_HEAD_DIM)
        w = jnp.where(causal, w_ref[h], 0.0).astype(BF16)
        mixed = jnp.dot(w, gv[:, cs].astype(BF16), preferred_element_type=F32)
        mixed = mixed + bt_ref[:, h:h + 1]
        o_ref[:, cs] = (u_ref[:, cs].astype(F32) * mixed).astype(o_ref.dtype)


def gmlp_prompt(gm, w_s, b_t, g_norm, gated_sample, *, m_total, m_prompt, per_step):
    c = CHUNK
    rows = per_step * c
    assert m_prompt % rows == 0 and gated_sample.shape[0] % rows == 0
    nsteps = m_prompt // rows
    last = nsteps - 1
    return pl.pallas_call(
        functools.partial(_gmlp_prompt_kernel, n_prompt_steps=nsteps, per_step=per_step),
        grid=(m_total // rows,),
        in_specs=[
            pl.BlockSpec((rows, GM_WIDTH), lambda r: (jnp.minimum(r, last), 0)),
            pl.BlockSpec((rows, GM_WIDTH), lambda r: (jnp.minimum(r, last), 1)),
            pl.BlockSpec((GM_HEADS, c, c), lambda r: (0, 0, 0)),
            pl.BlockSpec((c, GM_HEADS), lambda r: (0, 0)),
            pl.BlockSpec((1, GM_WIDTH), lambda r: (0, 0)),
            pl.BlockSpec((rows, GM_WIDTH), lambda r: (jnp.maximum(r - nsteps, 0), 0)),
        ],
        out_specs=pl.BlockSpec((rows, GM_WIDTH), lambda r: (r, 0)),
        out_shape=jax.ShapeDtypeStruct((m_total, GM_WIDTH), BF16),
        compiler_params=_params(("arbitrary",), 40),
        name="gmlp_prompt",
    )(gm, gm, w_s, b_t, g_norm, gated_sample)


def _gmlp_sample_kernel(u_ref, v_ref, wc_ref, bc_ref, gn_ref, o_ref, gv_ref, *, t_new):
    rows = u_ref.shape[0]
    nseq = rows // t_new
    gv = _rms(v_ref[...].astype(F32), gn_ref[...])
    gv_ref[...] = gv
    t = lax.broadcasted_iota(jnp.int32, (t_new, GM_HEAD_DIM), 0)
    for h in range(GM_HEADS):
        cs = slice(h * GM_HEAD_DIM, (h + 1) * GM_HEAD_DIM)
        x = gv[:, cs].reshape(nseq, t_new, GM_HEAD_DIM)
        acc = jnp.zeros((nseq, t_new, GM_HEAD_DIM), F32)
        for s in range(t_new):
            coef = jnp.where(t >= s, wc_ref[h, s], 0.0)
            acc = acc + coef[None] * x[:, s:s + 1, :]
        mixed = acc + bc_ref[h][None]
        u = u_ref[:, cs].astype(F32).reshape(nseq, t_new, GM_HEAD_DIM)
        o_ref[:, cs] = (u * mixed).reshape(rows, GM_HEAD_DIM).astype(o_ref.dtype)


def gmlp_sample(gm, w_s, b_s, g_norm, *, m_prompt, m_sample, t_new):
    lanes = GM_HEAD_DIM
    w8 = w_s[:, :t_new, :t_new]
    wc = jnp.broadcast_to(
        w8.transpose(0, 2, 1)[..., None], (GM_HEADS, t_new, t_new, lanes)
    )
    bc = jnp.broadcast_to(b_s[:, :t_new, None], (GM_HEADS, t_new, lanes))
    blk = m_prompt // m_sample
    return pl.pallas_call(
        functools.partial(_gmlp_sample_kernel, t_new=t_new),
        grid=(1,),
        in_specs=[
            pl.BlockSpec((m_sample, GM_WIDTH), lambda i: (blk, 0)),
            pl.BlockSpec((m_sample, GM_WIDTH), lambda i: (blk, 1)),
            pl.BlockSpec((GM_HEADS, t_new, t_new, lanes), lambda i: (0, 0, 0, 0)),
            pl.BlockSpec((GM_HEADS, t_new, lanes), lambda i: (0, 0, 0)),
            pl.BlockSpec((1, GM_WIDTH), lambda i: (0, 0)),
        ],
        out_specs=[
            pl.BlockSpec((m_sample, GM_WIDTH), lambda i: (0, 0)),
            pl.BlockSpec((m_sample, GM_WIDTH), lambda i: (0, 0)),
        ],
        out_shape=[
            jax.ShapeDtypeStruct((m_sample, GM_WIDTH), BF16),
            jax.ShapeDtypeStruct((m_sample, GM_WIDTH), F32),
        ],
        compiler_params=_params(("arbitrary",), 40),
        name="gmlp_sample",
    )(gm, gm, wc, bc, g_norm)


def _conv_gelu_gate(g, g1, g2, up, cw_ref, cb_ref):
    c = cb_ref[...] + cw_ref[0:1, :] * g2
    c = c + cw_ref[1:2, :] * g1
    c = c + cw_ref[2:3, :] * g
    return _gelu_times(c, up)


def _ffn_a_kernel(h_ref, hs_ref, wg_ref, wu_ref, cw_ref, cb_ref, st_ref, wd_ref, a_ref,
                  tail_ref, as_ref, gs_ref, wdb_ref, gx_ref, *, chunk, t_new,
                  cast_blocks):
    i = pl.program_id(0)
    _side_cast(wd_ref, wdb_ref, cast_blocks)
    tm = h_ref.shape[0]
    tf = wg_ref.shape[1]
    wg = wg_ref[...].astype(BF16)
    wu = wu_ref[...].astype(BF16)

    prev8 = jnp.zeros((8, tf), F32)
    for r0 in range(0, tm, chunk):
        h = h_ref[r0:r0 + chunk, :]
        g = jnp.dot(h, wg, preferred_element_type=F32)
        up = jnp.dot(h, wu, preferred_element_type=F32)
        gx_ref[0:8, :] = prev8
        gx_ref[8:8 + chunk, :] = g
        g1 = gx_ref[7:7 + chunk, :]
        g2 = gx_ref[6:6 + chunk, :]
        a_ref[r0:r0 + chunk, :] = _conv_gelu_gate(g, g1, g2, up, cw_ref, cb_ref).astype(
            a_ref.dtype
        )
        prev8 = g[chunk - 8:, :]
    tail_ref[...] = prev8

    @pl.when(i == pl.num_programs(0) - 1)
    def _():
        hs = hs_ref[...]
        rows = hs.shape[0]
        g = jnp.dot(hs, wg, preferred_element_type=F32)
        up = jnp.dot(hs, wu, preferred_element_type=F32)
        gs_ref[...] = g
        t = lax.broadcasted_iota(jnp.int32, g.shape, 0) % t_new
        st2 = st_ref[...]
        st1 = pltpu.roll(st2, rows - 1, 0)
        g1 = jnp.where(t >= 1, pltpu.roll(g, 1, 0), st1)
        g2 = jnp.where(t >= 2, pltpu.roll(g, 2, 0), st2)
        as_ref[...] = _conv_gelu_gate(g, g1, g2, up, cw_ref, cb_ref).astype(as_ref.dtype)


def ffn_a(h, w_gate, w_up, w_down, conv_w, conv_b, conv_state, layer, *, m_prompt,
          m_sample, seq, t_new, tf, chunk, cast_rows):
    m_total, d = h.shape
    f = w_gate.shape[-1]
    nf = f // tf
    ni = m_prompt // seq
    assert t_new % 8 == 0 and t_new >= CONV_W - 1
    st = jnp.pad(conv_state, ((0, 0), (0, t_new - (CONV_W - 1)), (0, 0)))
    st = st.reshape(m_sample, f)
    sample_col = lambda i, j: (0, jnp.where(i == ni - 1, j, 0))
    wd_in, wd_out, wd_shape, cast_blocks = _side_cast_specs(
        w_down, layer, cast_rows, ni * nf, nf
    )
    return pl.pallas_call(
        functools.partial(_ffn_a_kernel, chunk=chunk, t_new=t_new,
                          cast_blocks=cast_blocks),
        grid=(ni, nf),
        in_specs=[
            pl.BlockSpec((seq, d), lambda i, j: (i, 0), pipeline_mode=pl.Buffered(1)),
            pl.BlockSpec((m_sample, d), lambda i, j: (m_prompt // m_sample, 0)),
            pl.BlockSpec((None, d, tf), lambda i, j: (layer, 0, j)),
            pl.BlockSpec((None, d, tf), lambda i, j: (layer, 0, j)),
            pl.BlockSpec((None, CONV_W, tf), lambda i, j: (layer, 0, j)),
            pl.BlockSpec((None, 1, tf), lambda i, j: (layer, 0, j)),
            pl.BlockSpec((m_sample, tf), lambda i, j: (0, j)),
            wd_in,
        ],
        out_specs=[
            pl.BlockSpec((seq, tf), lambda i, j: (i, j)),
            pl.BlockSpec((None, 8, tf), lambda i, j: (i, 0, j)),
            pl.BlockSpec((m_sample, tf), sample_col),
            pl.BlockSpec((m_sample, tf), sample_col),
            wd_out,
        ],
        out_shape=[
            jax.ShapeDtypeStruct((m_total, f), BF16),
            jax.ShapeDtypeStruct((ni, 8, f), F32),
            jax.ShapeDtypeStruct((m_sample, f), BF16),
            jax.ShapeDtypeStruct((m_sample, f), F32),
            wd_shape,
        ],
        scratch_shapes=[pltpu.VMEM((chunk + 8, tf), F32)],
        compiler_params=_params(("arbitrary", "arbitrary"), 56),
        name="ffn_a",
    )(h, h, w_gate, w_up, conv_w, conv_b, st, w_down)


def kernel(x_prompt, x_sample, state_swa_k, state_swa_v, state_conv, w_in, w_out,
           attn_sinks, gm_spatial, gm_bias, gm_norm, norm_pre_mix, norm_post_mix,
           norm_pre_ffn, norm_post_ffn, w_ffn_gate, w_ffn_up, w_ffn_down, conv_w,
           conv_b):
    batch, seq, d = x_prompt.shape
    dec_batch, dec_seq, _ = x_sample.shape
    depth = w_in.shape[0]
    f = w_ffn_gate.shape[-1]
    mp = batch * seq
    ms = dec_batch * dec_seq
    m = mp + ms
    keep = min(WINDOW, seq)

    tm_mm = m // 8
    tm_down = m // 12
    tf = 256
    tm_rows = ms
    assert mp % tm_rows == 0

    w_in_b = w_in
    conv_b3 = conv_b.reshape(depth, 1, f)
    cast_rows = 64

    x = (x_prompt.reshape(mp, d), x_sample.reshape(ms, d))
    h = rms_rows(x, norm_pre_mix[0].reshape(1, d), tm=tm_rows)

    p_k, p_v, p_c, s_k, s_v, s_c, s_g = [], [], [], [], [], [], []
    for l in range(depth):
        last = l + 1 == depth
        tn = 512
        tn_q = tn if w_in_b.dtype == F32 else 2 * tn
        q, _ = matmul_cols(h, w_in_b, 0, col_block0=0, n=Q_COLS, tm=tm_mm, tn=tn_q,
                           out_dtype=BF16, epilogue=_scale_q, name="mm_q")
        kv, _ = matmul_cols(h, w_in_b, 0, col_block0=Q_COLS // tn, n=2 * KV_COLS,
                            tm=tm_mm, tn=tn, out_dtype=F32, epilogue=_identity,
                            name="mm_kv")
        gm, w_out_b = matmul_cols(
            h, w_in_b, 0, col_block0=(Q_COLS + 2 * KV_COLS) // tn, n=2 * GM_WIDTH,
            tm=tm_mm, tn=tn, out_dtype=BF16, epilogue=_gelu, name="mm_gm",
            cast=(w_out, l, cast_rows))
        w_out_b = w_out_b.reshape(1, 2, Q_COLS, d)

        kv_p = jnp.stack([kv[(b + 1) * seq - keep:(b + 1) * seq] for b in range(batch)])
        kv_s = kv[mp:]
        k_s, v_s = kv_s[:, :KV_COLS], kv_s[:, KV_COLS:]
        p_k.append(kv_p[..., :KV_COLS].reshape(batch, keep, N_KV_HEADS, HEAD_DIM))
        p_v.append(kv_p[..., KV_COLS:].reshape(batch, keep, N_KV_HEADS, HEAD_DIM))
        s_k.append(k_s.reshape(dec_batch, dec_seq, N_KV_HEADS, HEAD_DIM))
        s_v.append(v_s.reshape(dec_batch, dec_seq, N_KV_HEADS, HEAD_DIM))

        att_s = attn_sample(q[mp:], k_s, v_s, state_swa_k[l], state_swa_v[l],
                            attn_sinks[l], nb=8)
        att = attn_prompt(q, kv, attn_sinks[l], att_s, m_total=m, m_prompt=mp, seq=seq,
                          per_step=2)

        gmo_s, gv_s = gmlp_sample(gm, gm_spatial[l], gm_bias[l], gm_norm[l].reshape(1, -1),
                                  m_prompt=mp, m_sample=ms, t_new=dec_seq)
        gmo = gmlp_prompt(gm, gm_spatial[l], gm_bias[l].T, gm_norm[l].reshape(1, -1),
                          gmo_s, m_total=m, m_prompt=mp, per_step=2)
        s_g.append(gv_s.reshape(dec_batch, dec_seq, GM_HEADS, GM_HEAD_DIM))

        mix, w_in_next = matmul_two(att, gmo, w_out_b, 0, tm=tm_mm, tn=1024,
                                    out_dtype=BF16, name="mm_out",
                                    cast=None if last else (w_in, l + 1, 2 * cast_rows))
        x, h = post_rows(mix, x, norm_post_mix[l].reshape(1, d),
                         norm_pre_ffn[l].reshape(1, d), tm=tm_rows)

        a, tail, a_s, g_s, w_down_b = ffn_a(
            h, w_ffn_gate, w_ffn_up, w_ffn_down, conv_w, conv_b3, state_conv[l], l,
            m_prompt=mp, m_sample=ms, seq=seq, t_new=dec_seq, tf=tf, chunk=256,
            cast_rows=2 * cast_rows)
        a = place_rows(a, a_s, row_block=mp // ms)
        p_c.append(tail[:, 8 - (CONV_W - 1):])
        s_c.append(g_s.reshape(dec_batch, dec_seq, f)[:, dec_seq - (CONV_W - 1):])

        ffn, _ = matmul_cols(a, w_down_b[None], 0, col_block0=0, n=d, tm=tm_down, tn=512,
                             out_dtype=BF16, epilogue=_identity, name="mm_down",
                             vmem_mib=58)
        x, h = post_rows(ffn, x, norm_post_ffn[l].reshape(1, d),
                         None if last else norm_pre_mix[l + 1].reshape(1, d),
                         tm=tm_rows, split_out=(mp, ms) if last else None)
        if not last:
            w_in_b = w_in_next[None]

    x_p, x_s = x
    return (x_p.reshape(batch, seq, d), x_s.reshape(dec_batch, dec_seq, d),
            jnp.stack(p_k), jnp.stack(p_v), jnp.stack(p_c),
            jnp.stack(s_k), jnp.stack(s_v), jnp.stack(s_c), jnp.stack(s_g))
```

```python
import functools
import math

import jax
import jax.numpy as jnp
from jax import lax
from jax.experimental import pallas as pl
from jax.experimental.pallas import tpu as pltpu

F32 = jnp.float32
BF16 = jnp.bfloat16

D_MODEL = 4096
HEAD_DIM = 64
N_HEADS = 32
N_KV_HEADS = 4
GQA_GROUP = 8
WINDOW = 128
CHUNK = 128
GM_HEAD_DIM = 128
GM_HEADS = 16
GM_WIDTH = 2048
Q_COLS = 2048
KV_COLS = 256
CONV_W = 3
EPS = 1e-6
LANES = 128
LOG2_E = math.log2(math.e)

MIB = 1024 * 1024


def _params(semantics, vmem_mib):
    return pltpu.CompilerParams(
        dimension_semantics=semantics, vmem_limit_bytes=vmem_mib * MIB
    )


def _rms(x, g):
    ms = jnp.mean(x * x, axis=-1, keepdims=True)
    return x * lax.rsqrt(ms + EPS) * g


def _split_specs(tm, d, n_prompt_tiles):
    return [
        pl.BlockSpec((tm, d), lambda i: (jnp.minimum(i, n_prompt_tiles - 1), 0)),
        pl.BlockSpec((tm, d), lambda i: (0, 0)),
    ]


def _load_rows(refs, n_prompt_tiles):
    if len(refs) == 1:
        return refs[0][...]
    is_prompt = pl.program_id(0) < n_prompt_tiles
    return jnp.where(is_prompt, refs[0][...], refs[1][...])


def _store_rows(refs, value, n_prompt_tiles):
    if len(refs) == 1:
        refs[0][...] = value
        return

    @pl.when(pl.program_id(0) < n_prompt_tiles)
    def _():
        refs[0][...] = value

    @pl.when(pl.program_id(0) >= n_prompt_tiles)
    def _():
        refs[1][...] = value


def _rms_kernel(*refs, n_in, n_prompt_tiles):
    x = _load_rows(refs[:n_in], n_prompt_tiles)
    g_ref, o_ref = refs[n_in:]
    o_ref[...] = _rms(x, g_ref[...]).astype(o_ref.dtype)


def rms_rows(x_parts, g, *, tm):
    d = x_parts[0].shape[1]
    m = sum(p.shape[0] for p in x_parts)
    npt = x_parts[0].shape[0] // tm
    row = pl.BlockSpec((tm, d), lambda i: (i, 0))
    x_specs = [row] if len(x_parts) == 1 else _split_specs(tm, d, npt)
    return pl.pallas_call(
        functools.partial(_rms_kernel, n_in=len(x_parts), n_prompt_tiles=npt),
        grid=(m // tm,),
        in_specs=x_specs + [pl.BlockSpec((1, d), lambda i: (0, 0))],
        out_specs=row,
        out_shape=jax.ShapeDtypeStruct((m, d), BF16),
        compiler_params=_params(("arbitrary",), 40),
        name="rms_rows",
    )(*x_parts, g)


def _post_kernel(*refs, n_in, n_out, has_next, n_prompt_tiles):
    m_ref = refs[0]
    x = _load_rows(refs[1:1 + n_in], n_prompt_tiles)
    rest = refs[1 + n_in:]
    xn = x + _rms(m_ref[...].astype(F32), rest[0][...])
    rest = rest[1:]
    if has_next:
        gnext_ref, rest = rest[0], rest[1:]
    _store_rows(rest[:n_out], xn, n_prompt_tiles)
    if has_next:
        ho_ref = rest[n_out]
        ho_ref[...] = _rms(xn, gnext_ref[...]).astype(ho_ref.dtype)


def post_rows(m_in, x_parts, g_post, g_next, *, tm, split_out=None):
    m, d = m_in.shape
    npt = (x_parts[0].shape[0] if split_out is None else split_out[0]) // tm
    if len(x_parts) == 1 and split_out is None:
        npt = m // tm
    row = pl.BlockSpec((tm, d), lambda i: (i, 0))
    vec = pl.BlockSpec((1, d), lambda i: (0, 0))
    x_specs = [row] if len(x_parts) == 1 else _split_specs(tm, d, npt)
    if split_out is None:
        xo_specs = [row]
        xo_shapes = [jax.ShapeDtypeStruct((m, d), F32)]
    else:
        xo_specs = _split_specs(tm, d, npt)
        xo_shapes = [jax.ShapeDtypeStruct((r, d), F32) for r in split_out]
    has_next = g_next is not None
    gains = [g_post, g_next] if has_next else [g_post]
    out = pl.pallas_call(
        functools.partial(_post_kernel, n_in=len(x_parts), n_out=len(xo_specs),
                          has_next=has_next, n_prompt_tiles=npt),
        grid=(m // tm,),
        in_specs=[row] + x_specs + [vec] * len(gains),
        out_specs=xo_specs + ([row] if has_next else []),
        out_shape=xo_shapes
        + ([jax.ShapeDtypeStruct((m, d), BF16)] if has_next else []),
        compiler_params=_params(("arbitrary",), 48),
        name="post_rows",
    )(m_in, *x_parts, *gains)
    x_new = out[:len(xo_specs)]
    return (x_new, out[-1]) if has_next else (x_new, None)


def _grid_step():
    return pl.program_id(0) * pl.num_programs(1) + pl.program_id(1)


def _side_cast(src_ref, dst_ref, n_blocks):
    @pl.when(_grid_step() < n_blocks)
    def _():
        dst_ref[...] = src_ref[...].astype(dst_ref.dtype)


def _side_cast_specs(src, layer, block_rows, n_steps, n_cols_grid):
    _, rows, cols = src.shape
    n_blocks = rows // block_rows
    assert rows % block_rows == 0 and n_blocks <= n_steps

    def block(i, j):
        return jnp.minimum(i * n_cols_grid + j, n_blocks - 1)

    in_spec = pl.BlockSpec((None, block_rows, cols), lambda i, j: (layer, block(i, j), 0))
    out_spec = pl.BlockSpec((block_rows, cols), lambda i, j: (block(i, j), 0))
    return in_spec, out_spec, jax.ShapeDtypeStruct((rows, cols), BF16), n_blocks


def _mm_kernel(a_ref, b_ref, *rest, epilogue, cast_blocks):
    o_ref = rest[-2] if cast_blocks else rest[-1]
    acc = jnp.dot(a_ref[...], b_ref[...].astype(BF16), preferred_element_type=F32)
    o_ref[...] = epilogue(acc).astype(o_ref.dtype)
    if cast_blocks:
        _side_cast(rest[0], rest[-1], cast_blocks)


def _mm2_kernel(a1_ref, a2_ref, b1_ref, b2_ref, *rest, cast_blocks):
    o_ref = rest[-2] if cast_blocks else rest[-1]
    acc = jnp.dot(a1_ref[...], b1_ref[...], preferred_element_type=F32)
    acc = acc + jnp.dot(a2_ref[...], b2_ref[...], preferred_element_type=F32)
    o_ref[...] = acc.astype(o_ref.dtype)
    if cast_blocks:
        _side_cast(rest[0], rest[-1], cast_blocks)


def _with_side_cast(cast, grid, in_specs, out_specs, out_shapes, operands):
    if cast is None:
        return 0
    src, layer, block_rows = cast
    i_spec, o_spec, o_shape, n_blocks = _side_cast_specs(
        src, layer, block_rows, grid[0] * grid[1], grid[1]
    )
    in_specs.append(i_spec)
    out_specs.append(o_spec)
    out_shapes.append(o_shape)
    operands.append(src)
    return n_blocks


def matmul_cols(a, w, layer, *, col_block0, n, tm, tn, out_dtype, epilogue, name,
                cast=None, vmem_mib=56):
    m, k = a.shape
    grid = (m // tm, n // tn)
    in_specs = [
        pl.BlockSpec((tm, k), lambda i, j: (i, 0)),
        pl.BlockSpec((None, k, tn), lambda i, j: (layer, 0, j + col_block0)),
    ]
    out_specs = [pl.BlockSpec((tm, tn), lambda i, j: (i, j))]
    out_shapes = [jax.ShapeDtypeStruct((m, n), out_dtype)]
    operands = [a, w]
    cast_blocks = _with_side_cast(cast, grid, in_specs, out_specs, out_shapes, operands)
    out = pl.pallas_call(
        functools.partial(_mm_kernel, epilogue=epilogue, cast_blocks=cast_blocks),
        grid=grid,
        in_specs=in_specs,
        out_specs=out_specs,
        out_shape=out_shapes,
        compiler_params=_params(("arbitrary", "arbitrary"), vmem_mib),
        name=name,
    )(*operands)
    return out[0], (out[1] if cast_blocks else None)


def matmul_two(a1, a2, w, layer, *, tm, tn, out_dtype, name, cast=None):
    m, k1 = a1.shape
    k2 = a2.shape[1]
    assert k1 == k2
    n = w.shape[-1]
    grid = (m // tm, n // tn)
    in_specs = [
        pl.BlockSpec((tm, k1), lambda i, j: (i, 0)),
        pl.BlockSpec((tm, k2), lambda i, j: (i, 0)),
        pl.BlockSpec((None, None, k1, tn), lambda i, j: (layer, 0, 0, j)),
        pl.BlockSpec((None, None, k2, tn), lambda i, j: (layer, 1, 0, j)),
    ]
    out_specs = [pl.BlockSpec((tm, tn), lambda i, j: (i, j))]
    out_shapes = [jax.ShapeDtypeStruct((m, n), out_dtype)]
    operands = [a1, a2, w, w]
    cast_blocks = _with_side_cast(cast, grid, in_specs, out_specs, out_shapes, operands)
    out = pl.pallas_call(
        functools.partial(_mm2_kernel, cast_blocks=cast_blocks),
        grid=grid,
        in_specs=in_specs,
        out_specs=out_specs,
        out_shape=out_shapes,
        compiler_params=_params(("arbitrary", "arbitrary"), 56),
        name=name,
    )(*operands)
    return out[0], (out[1] if cast_blocks else None)


def _identity(x):
    return x


def _scale_q(x):
    return x * (HEAD_DIM ** -0.5 * LOG2_E)


_GELU_A = -2.0 * LOG2_E * math.sqrt(2.0 / math.pi)
_GELU_B = _GELU_A * 0.044715


def _gelu_times(x, y):
    t = x * x * _GELU_B + _GELU_A
    return (x * y) / (1.0 + jnp.exp2(t * x))


def _gelu(x):
    t = x * x * _GELU_B + _GELU_A
    return x / (1.0 + jnp.exp2(t * x))


def _softmax2_with_sink(s, sink):
    m = jnp.maximum(jnp.max(s, axis=-1, keepdims=True), sink)
    e = jnp.exp2(s - m)
    den = jnp.sum(e, axis=-1, keepdims=True) + jnp.exp2(sink - m)
    return e / den


def _attn_prompt_kernel(sink_ref, q_ref, kvc_ref, kvp_ref, os_ref, o_ref, *,
                        blocks_per_seq, n_prompt_steps, per_step):
    step = pl.program_id(0)
    w = WINDOW

    @pl.when(step >= n_prompt_steps)
    def _():
        o_ref[...] = os_ref[...]

    @pl.when(step < n_prompt_steps)
    def _():
        for b in range(per_step):
            rows = pl.ds(b * w, w)
            prev_ref = kvp_ref if b == 0 else kvc_ref.at[pl.ds((b - 1) * w, w)]
            _attn_prompt_block(sink_ref, q_ref.at[rows], kvc_ref.at[rows], prev_ref,
                               o_ref.at[rows], (step * per_step + b) % blocks_per_seq)


def _attn_prompt_block(sink_ref, q_ref, kvc_ref, kvp_ref, o_ref, n):
    w = WINDOW
    row = lax.broadcasted_iota(jnp.int32, (w, 2 * w), 0)
    col = lax.broadcasted_iota(jnp.int32, (w, 2 * w), 1)
    prev_ok = jnp.logical_and(jnp.logical_and(col < w, col > row), n > 0)
    own_ok = jnp.logical_and(col >= w, col - w <= row)
    valid = jnp.logical_or(prev_ok, own_ok)
    for kvh in range(N_KV_HEADS):
        ks = slice(kvh * HEAD_DIM, (kvh + 1) * HEAD_DIM)
        vs = slice(KV_COLS + kvh * HEAD_DIM, KV_COLS + (kvh + 1) * HEAD_DIM)
        k = jnp.concatenate([kvp_ref[:, ks], kvc_ref[:, ks]], axis=0).astype(BF16)
        v = jnp.concatenate([kvp_ref[:, vs], kvc_ref[:, vs]], axis=0).astype(BF16)
        zeros = jnp.zeros((2 * w, HEAD_DIM), BF16)
        ones = jnp.ones((2 * w, LANES), BF16)
        pairs = [kvh * (GQA_GROUP // 2) + j for j in range(GQA_GROUP // 2)]
        qp = jnp.concatenate([q_ref[:, t * LANES:(t + 1) * LANES] for t in pairs], axis=0)
        outs = []
        for half in range(2):
            k_pad = jnp.concatenate([k, zeros] if half == 0 else [zeros, k], axis=1)
            v_ext = jnp.concatenate(
                [v, zeros, ones] if half == 0 else [zeros, v, ones], axis=1
            )
            s = lax.dot_general(
                qp, k_pad, (((1,), (1,)), ((), ())), preferred_element_type=F32
            )
            es, sink_terms = [], []
            for j, t in enumerate(pairs):
                sink = sink_ref[2 * t + half] * LOG2_E
                sg = jnp.where(valid, s[j * w:(j + 1) * w], -jnp.inf)
                m = jnp.maximum(jnp.max(sg, axis=-1, keepdims=True), sink)
                es.append(jnp.exp2(sg - m).astype(BF16))
                sink_terms.append(jnp.exp2(sink - m))
            o = jnp.dot(jnp.concatenate(es, axis=0), v_ext, preferred_element_type=F32)
            outs.append([
                o[j * w:(j + 1) * w, :LANES] / (o[j * w:(j + 1) * w, LANES:] + sink_terms[j])
                for j in range(len(pairs))
            ])
        for j, t in enumerate(pairs):
            o_ref[:, t * LANES:(t + 1) * LANES] = (outs[0][j] + outs[1][j]).astype(o_ref.dtype)


def attn_prompt(q, kv, sinks, att_sample, *, m_total, m_prompt, seq, per_step):
    w = WINDOW
    rows = per_step * w
    assert m_prompt % rows == 0 and att_sample.shape[0] % rows == 0
    nsteps = m_prompt // rows
    last = nsteps - 1
    last_prev = m_prompt // w - 1
    return pl.pallas_call(
        functools.partial(_attn_prompt_kernel, blocks_per_seq=seq // w,
                          n_prompt_steps=nsteps, per_step=per_step),
        grid=(m_total // rows,),
        in_specs=[
            pl.BlockSpec(memory_space=pltpu.SMEM),
            pl.BlockSpec((rows, Q_COLS), lambda r: (jnp.minimum(r, last), 0)),
            pl.BlockSpec((rows, 2 * KV_COLS), lambda r: (jnp.minimum(r, last), 0)),
            pl.BlockSpec((w, 2 * KV_COLS),
                         lambda r: (jnp.clip(per_step * r - 1, 0, last_prev), 0)),
            pl.BlockSpec((rows, Q_COLS), lambda r: (jnp.maximum(r - nsteps, 0), 0)),
        ],
        out_specs=pl.BlockSpec((rows, Q_COLS), lambda r: (r, 0)),
        out_shape=jax.ShapeDtypeStruct((m_total, Q_COLS), BF16),
        compiler_params=_params(("arbitrary",), 40),
        name="attn_prompt",
    )(sinks, q, kv, kv, att_sample)


def _attn_sample_kernel(sink_ref, q_ref, k_ref, v_ref, o_ref, *, t_new):
    nb, nkv, rows, d = q_ref.shape
    keys = k_ref.shape[2]
    q = q_ref[...].reshape(nb * nkv, rows, d)
    k = k_ref[...].reshape(nb * nkv, keys, d)
    v = v_ref[...].reshape(nb * nkv, keys, d)
    s = jnp.einsum("bqd,bkd->bqk", q, k, preferred_element_type=F32)
    t = lax.broadcasted_iota(jnp.int32, (rows, keys), 0) // GQA_GROUP
    col = lax.broadcasted_iota(jnp.int32, (rows, keys), 1)
    valid = jnp.logical_and(col > t, col <= WINDOW + t)
    s = jnp.where(valid[None], s, -jnp.inf)
    sink = jnp.broadcast_to(sink_ref[...][None], (nb, nkv, rows, 1)).reshape(
        nb * nkv, rows, 1
    )
    p = _softmax2_with_sink(s, sink * LOG2_E).astype(BF16)
    o = jnp.einsum("bqk,bkd->bqd", p, v, preferred_element_type=F32)
    o_ref[...] = o.reshape(nb, nkv, rows, d).astype(o_ref.dtype)


def attn_sample(q_s, k_new, v_new, k_buf, v_buf, sinks, *, nb):
    b, l = k_buf.shape[0], k_buf.shape[1]
    t_new = q_s.shape[0] // b
    rows = t_new * GQA_GROUP
    keys = 2 * WINDOW
    q4 = q_s.reshape(b, t_new, N_KV_HEADS, GQA_GROUP, HEAD_DIM)
    q4 = q4.transpose(0, 2, 1, 3, 4).reshape(b, N_KV_HEADS, rows, HEAD_DIM)

    def keys_layout(buf, new):
        cat = jnp.concatenate(
            [buf, new.reshape(b, t_new, N_KV_HEADS, HEAD_DIM)], axis=1
        )
        cat = jnp.pad(cat, ((0, 0), (0, keys - l - t_new), (0, 0), (0, 0)))
        return cat.transpose(0, 2, 1, 3).astype(BF16)

    k4 = keys_layout(k_buf, k_new)
    v4 = keys_layout(v_buf, v_new)
    sink_rows = jnp.broadcast_to(
        sinks.reshape(N_KV_HEADS, 1, GQA_GROUP), (N_KV_HEADS, t_new, GQA_GROUP)
    ).reshape(N_KV_HEADS, rows, 1)
    o4 = pl.pallas_call(
        functools.partial(_attn_sample_kernel, t_new=t_new),
        grid=(b // nb,),
        in_specs=[
            pl.BlockSpec((N_KV_HEADS, rows, 1), lambda i: (0, 0, 0)),
            pl.BlockSpec((nb, N_KV_HEADS, rows, HEAD_DIM), lambda i: (i, 0, 0, 0)),
            pl.BlockSpec((nb, N_KV_HEADS, keys, HEAD_DIM), lambda i: (i, 0, 0, 0)),
            pl.BlockSpec((nb, N_KV_HEADS, keys, HEAD_DIM), lambda i: (i, 0, 0, 0)),
        ],
        out_specs=pl.BlockSpec(
            (nb, N_KV_HEADS, rows, HEAD_DIM), lambda i: (i, 0, 0, 0)
        ),
        out_shape=jax.ShapeDtypeStruct((b, N_KV_HEADS, rows, HEAD_DIM), BF16),
        compiler_params=_params(("parallel",), 40),
        name="attn_sample",
    )(sink_rows, q4, k4, v4)
    o = o4.reshape(b, N_KV_HEADS, t_new, GQA_GROUP, HEAD_DIM).transpose(0, 2, 1, 3, 4)
    return o.reshape(b * t_new, Q_COLS)


def _place_rows_kernel(rows_ref, buf_ref, o_ref):
    del buf_ref
    o_ref[...] = rows_ref[...]


def place_rows(buf, rows, *, row_block):
    r, c = rows.shape
    return pl.pallas_call(
        _place_rows_kernel,
        grid=(1,),
        in_specs=[
            pl.BlockSpec((r, c), lambda i: (0, 0)),
            pl.BlockSpec(memory_space=pl.ANY),
        ],
        out_specs=pl.BlockSpec((r, c), lambda i: (row_block, 0)),
        out_shape=jax.ShapeDtypeStruct(buf.shape, buf.dtype),
        input_output_aliases={1: 0},
        compiler_params=_params(("arbitrary",), 16),
        name="place_rows",
    )(rows, buf)


def _gmlp_prompt_kernel(u_ref, v_ref, w_ref, bt_ref, gn_ref, os_ref, o_ref, *,
                        n_prompt_steps, per_step):
    @pl.when(pl.program_id(0) >= n_prompt_steps)
    def _():
        o_ref[...] = os_ref[...]

    @pl.when(pl.program_id(0) < n_prompt_steps)
    def _():
        for b in range(per_step):
            rows = pl.ds(b * CHUNK, CHUNK)
            _gmlp_prompt_chunk(u_ref.at[rows], v_ref.at[rows], w_ref, bt_ref, gn_ref,
                               o_ref.at[rows])


def _gmlp_prompt_chunk(u_ref, v_ref, w_ref, bt_ref, gn_ref, o_ref):
    gv = _rms(v_ref[...].astype(F32), gn_ref[...])
    c = CHUNK
    row = lax.broadcasted_iota(jnp.int32, (c, c), 0)
    col = lax.broadcasted_iota(jnp.int32, (c, c), 1)
    causal = col <= row
    for h in range(GM_HEADS):
        cs = slice(h * GM_HEAD_DIM, (h + 1) * GM_HEAD_DIM)
        w = jnp.where(causal, w_ref[h], 0.0).astype(BF16)
        mixed = jnp.dot(w, gv[:, cs].astype(BF16), preferred_element_type=F32)
        mixed = mixed + bt_ref[:, h:h + 1]
        o_ref[:, cs] = (u_ref[:, cs].astype(F32) * mixed).astype(o_ref.dtype)


def gmlp_prompt(gm, w_s, b_t, g_norm, gated_sample, *, m_total, m_prompt, per_step):
    c = CHUNK
    rows = per_step * c
    assert m_prompt % rows == 0 and gated_sample.shape[0] % rows == 0
    nsteps = m_prompt // rows
    last = nsteps - 1
    return pl.pallas_call(
        functools.partial(_gmlp_prompt_kernel, n_prompt_steps=nsteps, per_step=per_step),
        grid=(m_total // rows,),
        in_specs=[
            pl.BlockSpec((rows, GM_WIDTH), lambda r: (jnp.minimum(r, last), 0)),
            pl.BlockSpec((rows, GM_WIDTH), lambda r: (jnp.minimum(r, last), 1)),
            pl.BlockSpec((GM_HEADS, c, c), lambda r: (0, 0, 0)),
            pl.BlockSpec((c, GM_HEADS), lambda r: (0, 0)),
            pl.BlockSpec((1, GM_WIDTH), lambda r: (0, 0)),
            pl.BlockSpec((rows, GM_WIDTH), lambda r: (jnp.maximum(r - nsteps, 0), 0)),
        ],
        out_specs=pl.BlockSpec((rows, GM_WIDTH), lambda r: (r, 0)),
        out_shape=jax.ShapeDtypeStruct((m_total, GM_WIDTH), BF16),
        compiler_params=_params(("arbitrary",), 40),
        name="gmlp_prompt",
    )(gm, gm, w_s, b_t, g_norm, gated_sample)


def _gmlp_sample_kernel(u_ref, v_ref, wc_ref, bc_ref, gn_ref, o_ref, gv_ref, *, t_new):
    rows = u_ref.shape[0]
    nseq = rows // t_new
    gv = _rms(v_ref[...].astype(F32), gn_ref[...])
    gv_ref[...] = gv
    t = lax.broadcasted_iota(jnp.int32, (t_new, GM_HEAD_DIM), 0)
    for h in range(GM_HEADS):
        cs = slice(h * GM_HEAD_DIM, (h + 1) * GM_HEAD_DIM)
        x = gv[:, cs].reshape(nseq, t_new, GM_HEAD_DIM)
        acc = jnp.zeros((nseq, t_new, GM_HEAD_DIM), F32)
        for s in range(t_new):
            coef = jnp.where(t >= s, wc_ref[h, s], 0.0)
            acc = acc + coef[None] * x[:, s:s + 1, :]
        mixed = acc + bc_ref[h][None]
        u = u_ref[:, cs].astype(F32).reshape(nseq, t_new, GM_HEAD_DIM)
        o_ref[:, cs] = (u * mixed).reshape(rows, GM_HEAD_DIM).astype(o_ref.dtype)


def gmlp_sample(gm, w_s, b_s, g_norm, *, m_prompt, m_sample, t_new):
    lanes = GM_HEAD_DIM
    w8 = w_s[:, :t_new, :t_new]
    wc = jnp.broadcast_to(
        w8.transpose(0, 2, 1)[..., None], (GM_HEADS, t_new, t_new, lanes)
    )
    bc = jnp.broadcast_to(b_s[:, :t_new, None], (GM_HEADS, t_new, lanes))
    blk = m_prompt // m_sample
    return pl.pallas_call(
        functools.partial(_gmlp_sample_kernel, t_new=t_new),
        grid=(1,),
        in_specs=[
            pl.BlockSpec((m_sample, GM_WIDTH), lambda i: (blk, 0)),
            pl.BlockSpec((m_sample, GM_WIDTH), lambda i: (blk, 1)),
            pl.BlockSpec((GM_HEADS, t_new, t_new, lanes), lambda i: (0, 0, 0, 0)),
            pl.BlockSpec((GM_HEADS, t_new, lanes), lambda i: (0, 0, 0)),
            pl.BlockSpec((1, GM_WIDTH), lambda i: (0, 0)),
        ],
        out_specs=[
            pl.BlockSpec((m_sample, GM_WIDTH), lambda i: (0, 0)),
            pl.BlockSpec((m_sample, GM_WIDTH), lambda i: (0, 0)),
        ],
        out_shape=[
            jax.ShapeDtypeStruct((m_sample, GM_WIDTH), BF16),
            jax.ShapeDtypeStruct((m_sample, GM_WIDTH), F32),
        ],
        compiler_params=_params(("arbitrary",), 40),
        name="gmlp_sample",
    )(gm, gm, wc, bc, g_norm)


def _conv_gelu_gate(g, g1, g2, up, cw_ref, cb_ref):
    c = cb_ref[...] + cw_ref[0:1, :] * g2
    c = c + cw_ref[1:2, :] * g1
    c = c + cw_ref[2:3, :] * g
    return _gelu_times(c, up)


def _ffn_a_kernel(h_ref, hs_ref, wg_ref, wu_ref, cw_ref, cb_ref, st_ref, wd_ref, a_ref,
                  tail_ref, as_ref, gs_ref, wdb_ref, gx_ref, *, chunk, t_new,
                  cast_blocks):
    i = pl.program_id(0)
    _side_cast(wd_ref, wdb_ref, cast_blocks)
    tm = h_ref.shape[0]
    tf = wg_ref.shape[1]
    wg = wg_ref[...].astype(BF16)
    wu = wu_ref[...].astype(BF16)

    prev8 = jnp.zeros((8, tf), F32)
    for r0 in range(0, tm, chunk):
        h = h_ref[r0:r0 + chunk, :]
        g = jnp.dot(h, wg, preferred_element_type=F32)
        up = jnp.dot(h, wu, preferred_element_type=F32)
        gx_ref[0:8, :] = prev8
        gx_ref[8:8 + chunk, :] = g
        g1 = gx_ref[7:7 + chunk, :]
        g2 = gx_ref[6:6 + chunk, :]
        a_ref[r0:r0 + chunk, :] = _conv_gelu_gate(g, g1, g2, up, cw_ref, cb_ref).astype(
            a_ref.dtype
        )
        prev8 = g[chunk - 8:, :]
    tail_ref[...] = prev8

    @pl.when(i == pl.num_programs(0) - 1)
    def _():
        hs = hs_ref[...]
        rows = hs.shape[0]
        g = jnp.dot(hs, wg, preferred_element_type=F32)
        up = jnp.dot(hs, wu, preferred_element_type=F32)
        gs_ref[...] = g
        t = lax.broadcasted_iota(jnp.int32, g.shape, 0) % t_new
        st2 = st_ref[...]
        st1 = pltpu.roll(st2, rows - 1, 0)
        g1 = jnp.where(t >= 1, pltpu.roll(g, 1, 0), st1)
        g2 = jnp.where(t >= 2, pltpu.roll(g, 2, 0), st2)
        as_ref[...] = _conv_gelu_gate(g, g1, g2, up, cw_ref, cb_ref).astype(as_ref.dtype)


def ffn_a(h, w_gate, w_up, w_down, conv_w, conv_b, conv_state, layer, *, m_prompt,
          m_sample, seq, t_new, tf, chunk, cast_rows):
    m_total, d = h.shape
    f = w_gate.shape[-1]
    nf = f // tf
    ni = m_prompt // seq
    assert t_new % 8 == 0 and t_new >= CONV_W - 1
    st = jnp.pad(conv_state, ((0, 0), (0, t_new - (CONV_W - 1)), (0, 0)))
    st = st.reshape(m_sample, f)
    sample_col = lambda i, j: (0, jnp.where(i == ni - 1, j, 0))
    wd_in, wd_out, wd_shape, cast_blocks = _side_cast_specs(
        w_down, layer, cast_rows, ni * nf, nf
    )
    return pl.pallas_call(
        functools.partial(_ffn_a_kernel, chunk=chunk, t_new=t_new,
                          cast_blocks=cast_blocks),
        grid=(ni, nf),
        in_specs=[
            pl.BlockSpec((seq, d), lambda i, j: (i, 0), pipeline_mode=pl.Buffered(1)),
            pl.BlockSpec((m_sample, d), lambda i, j: (m_prompt // m_sample, 0)),
            pl.BlockSpec((None, d, tf), lambda i, j: (layer, 0, j)),
            pl.BlockSpec((None, d, tf), lambda i, j: (layer, 0, j)),
            pl.BlockSpec((None, CONV_W, tf), lambda i, j: (layer, 0, j)),
            pl.BlockSpec((None, 1, tf), lambda i, j: (layer, 0, j)),
            pl.BlockSpec((m_sample, tf), lambda i, j: (0, j)),
            wd_in,
        ],
        out_specs=[
            pl.BlockSpec((seq, tf), lambda i, j: (i, j)),
            pl.BlockSpec((None, 8, tf), lambda i, j: (i, 0, j)),
            pl.BlockSpec((m_sample, tf), sample_col),
            pl.BlockSpec((m_sample, tf), sample_col),
            wd_out,
        ],
        out_shape=[
            jax.ShapeDtypeStruct((m_total, f), BF16),
            jax.ShapeDtypeStruct((ni, 8, f), F32),
            jax.ShapeDtypeStruct((m_sample, f), BF16),
            jax.ShapeDtypeStruct((m_sample, f), F32),
            wd_shape,
        ],
        scratch_shapes=[pltpu.VMEM((chunk + 8, tf), F32)],
        compiler_params=_params(("arbitrary", "arbitrary"), 56),
        name="ffn_a",
    )(h, h, w_gate, w_up, conv_w, conv_b, st, w_down)


def kernel(x_prompt, x_sample, state_swa_k, state_swa_v, state_conv, w_in, w_out,
           attn_sinks, gm_spatial, gm_bias, gm_norm, norm_pre_mix, norm_post_mix,
           norm_pre_ffn, norm_post_ffn, w_ffn_gate, w_ffn_up, w_ffn_down, conv_w,
           conv_b):
    batch, seq, d = x_prompt.shape
    dec_batch, dec_seq, _ = x_sample.shape
    depth = w_in.shape[0]
    f = w_ffn_gate.shape[-1]
    mp = batch * seq
    ms = dec_batch * dec_seq
    m = mp + ms
    keep = min(WINDOW, seq)

    tm_mm = m // 8
    tm_down = m // 12
    tf = 256
    tm_rows = ms
    assert mp % tm_rows == 0

    w_in_b = w_in
    conv_b3 = conv_b.reshape(depth, 1, f)
    cast_rows = 64

    x = (x_prompt.reshape(mp, d), x_sample.reshape(ms, d))
    h = rms_rows(x, norm_pre_mix[0].reshape(1, d), tm=tm_rows)

    p_k, p_v, p_c, s_k, s_v, s_c, s_g = [], [], [], [], [], [], []
    for l in range(depth):
        last = l + 1 == depth
        tn = 512
        tn_q = tn if w_in_b.dtype == F32 else 2 * tn
        q, _ = matmul_cols(h, w_in_b, 0, col_block0=0, n=Q_COLS, tm=tm_mm, tn=tn_q,
                           out_dtype=BF16, epilogue=_scale_q, name="mm_q")
        kv, _ = matmul_cols(h, w_in_b, 0, col_block0=Q_COLS // tn, n=2 * KV_COLS,
                            tm=tm_mm, tn=tn, out_dtype=F32, epilogue=_identity,
                            name="mm_kv")
        gm, w_out_b = matmul_cols(
            h, w_in_b, 0, col_block0=(Q_COLS + 2 * KV_COLS) // tn, n=2 * GM_WIDTH,
            tm=tm_mm, tn=tn, out_dtype=BF16, epilogue=_gelu, name="mm_gm",
            cast=(w_out, l, cast_rows))
        w_out_b = w_out_b.reshape(1, 2, Q_COLS, d)

        kv_p = jnp.stack([kv[(b + 1) * seq - keep:(b + 1) * seq] for b in range(batch)])
        kv_s = kv[mp:]
        k_s, v_s = kv_s[:, :KV_COLS], kv_s[:, KV_COLS:]
        p_k.append(kv_p[..., :KV_COLS].reshape(batch, keep, N_KV_HEADS, HEAD_DIM))
        p_v.append(kv_p[..., KV_COLS:].reshape(batch, keep, N_KV_HEADS, HEAD_DIM))
        s_k.append(k_s.reshape(dec_batch, dec_seq, N_KV_HEADS, HEAD_DIM))
        s_v.append(v_s.reshape(dec_batch, dec_seq, N_KV_HEADS, HEAD_DIM))

        att_s = attn_sample(q[mp:], k_s, v_s, state_swa_k[l], state_swa_v[l],
                            attn_sinks[l], nb=8)
        att = attn_prompt(q, kv, attn_sinks[l], att_s, m_total=m, m_prompt=mp, seq=seq,
                          per_step=2)

        gmo_s, gv_s = gmlp_sample(gm, gm_spatial[l], gm_bias[l], gm_norm[l].reshape(1, -1),
                                  m_prompt=mp, m_sample=ms, t_new=dec_seq)
        gmo = gmlp_prompt(gm, gm_spatial[l], gm_bias[l].T, gm_norm[l].reshape(1, -1),
                          gmo_s, m_total=m, m_prompt=mp, per_step=2)
        s_g.append(gv_s.reshape(dec_batch, dec_seq, GM_HEADS, GM_HEAD_DIM))

        mix, w_in_next = matmul_two(att, gmo, w_out_b, 0, tm=tm_mm, tn=1024,
                                    out_dtype=BF16, name="mm_out",
                                    cast=None if last else (w_in, l + 1, 2 * cast_rows))
        x, h = post_rows(mix, x, norm_post_mix[l].reshape(1, d),
                         norm_pre_ffn[l].reshape(1, d), tm=tm_rows)

        a, tail, a_s, g_s, w_down_b = ffn_a(
            h, w_ffn_gate, w_ffn_up, w_ffn_down, conv_w, conv_b3, state_conv[l], l,
            m_prompt=mp, m_sample=ms, seq=seq, t_new=dec_seq, tf=tf, chunk=256,
            cast_rows=2 * cast_rows)
        a = place_rows(a, a_s, row_block=mp // ms)
        p_c.append(tail[:, 8 - (CONV_W - 1):])
        s_c.append(g_s.reshape(dec_batch, dec_seq, f)[:, dec_seq - (CONV_W - 1):])

        ffn, _ = matmul_cols(a, w_down_b[None], 0, col_block0=0, n=d, tm=tm_down, tn=512,
                             out_dtype=BF16, epilogue=_identity, name="mm_down",
                             vmem_mib=58)
        x, h = post_rows(ffn, x, norm_post_ffn[l].reshape(1, d),
                         None if last else norm_pre_mix[l + 1].reshape(1, d),
                         tm=tm_rows, split_out=(mp, ms) if last else None)
        if not last:
            w_in_b = w_in_next[None]

    x_p, x_s = x
    return (x_p.reshape(batch, seq, d), x_s.reshape(dec_batch, dec_seq, d),
            jnp.stack(p_k), jnp.stack(p_v), jnp.stack(p_c),
            jnp.stack(s_k), jnp.stack(s_v), jnp.stack(s_c), jnp.stack(s_g))
```

```python
import functools
import math

import jax
import jax.numpy as jnp
from jax import lax
from jax.experimental import pallas as pl
from jax.experimental.pallas import tpu as pltpu

F32 = jnp.float32
BF16 = jnp.bfloat16

D_MODEL = 4096
HEAD_DIM = 64
N_HEADS = 32
N_KV_HEADS = 4
GQA_GROUP = 8
WINDOW = 128
CHUNK = 128
GM_HEAD_DIM = 128
GM_HEADS = 16
GM_WIDTH = 2048
Q_COLS = 2048
KV_COLS = 256
CONV_W = 3
EPS = 1e-6
LANES = 128
LOG2_E = math.log2(math.e)

MIB = 1024 * 1024


def _params(semantics, vmem_mib):
    return pltpu.CompilerParams(
        dimension_semantics=semantics, vmem_limit_bytes=vmem_mib * MIB
    )


def _rms(x, g):
    ms = jnp.mean(x * x, axis=-1, keepdims=True)
    return x * lax.rsqrt(ms + EPS) * g


def _split_specs(tm, d, n_prompt_tiles):
    return [
        pl.BlockSpec((tm, d), lambda i: (jnp.minimum(i, n_prompt_tiles - 1), 0)),
        pl.BlockSpec((tm, d), lambda i: (0, 0)),
    ]


def _load_rows(refs, n_prompt_tiles):
    if len(refs) == 1:
        return refs[0][...]
    is_prompt = pl.program_id(0) < n_prompt_tiles
    return jnp.where(is_prompt, refs[0][...], refs[1][...])


def _store_rows(refs, value, n_prompt_tiles):
    if len(refs) == 1:
        refs[0][...] = value
        return

    @pl.when(pl.program_id(0) < n_prompt_tiles)
    def _():
        refs[0][...] = value

    @pl.when(pl.program_id(0) >= n_prompt_tiles)
    def _():
        refs[1][...] = value


def _rms_kernel(*refs, n_in, n_prompt_tiles):
    x = _load_rows(refs[:n_in], n_prompt_tiles)
    g_ref, o_ref = refs[n_in:]
    o_ref[...] = _rms(x, g_ref[...]).astype(o_ref.dtype)


def rms_rows(x_parts, g, *, tm):
    d = x_parts[0].shape[1]
    m = sum(p.shape[0] for p in x_parts)
    npt = x_parts[0].shape[0] // tm
    row = pl.BlockSpec((tm, d), lambda i: (i, 0))
    x_specs = [row] if len(x_parts) == 1 else _split_specs(tm, d, npt)
    return pl.pallas_call(
        functools.partial(_rms_kernel, n_in=len(x_parts), n_prompt_tiles=npt),
        grid=(m // tm,),
        in_specs=x_specs + [pl.BlockSpec((1, d), lambda i: (0, 0))],
        out_specs=row,
        out_shape=jax.ShapeDtypeStruct((m, d), BF16),
        compiler_params=_params(("arbitrary",), 40),
        name="rms_rows",
    )(*x_parts, g)


def _post_kernel(*refs, n_in, n_out, has_next, n_prompt_tiles):
    m_ref = refs[0]
    x = _load_rows(refs[1:1 + n_in], n_prompt_tiles)
    rest = refs[1 + n_in:]
    xn = x + _rms(m_ref[...].astype(F32), rest[0][...])
    rest = rest[1:]
    if has_next:
        gnext_ref, rest = rest[0], rest[1:]
    _store_rows(rest[:n_out], xn, n_prompt_tiles)
    if has_next:
        ho_ref = rest[n_out]
        ho_ref[...] = _rms(xn, gnext_ref[...]).astype(ho_ref.dtype)


def post_rows(m_in, x_parts, g_post, g_next, *, tm, split_out=None):
    m, d = m_in.shape
    npt = (x_parts[0].shape[0] if split_out is None else split_out[0]) // tm
    if len(x_parts) == 1 and split_out is None:
        npt = m // tm
    row = pl.BlockSpec((tm, d), lambda i: (i, 0))
    vec = pl.BlockSpec((1, d), lambda i: (0, 0))
    x_specs = [row] if len(x_parts) == 1 else _split_specs(tm, d, npt)
    if split_out is None:
        xo_specs = [row]
        xo_shapes = [jax.ShapeDtypeStruct((m, d), F32)]
    else:
        xo_specs = _split_specs(tm, d, npt)
        xo_shapes = [jax.ShapeDtypeStruct((r, d), F32) for r in split_out]
    has_next = g_next is not None
    gains = [g_post, g_next] if has_next else [g_post]
    out = pl.pallas_call(
        functools.partial(_post_kernel, n_in=len(x_parts), n_out=len(xo_specs),
                          has_next=has_next, n_prompt_tiles=npt),
        grid=(m // tm,),
        in_specs=[row] + x_specs + [vec] * len(gains),
        out_specs=xo_specs + ([row] if has_next else []),
        out_shape=xo_shapes
        + ([jax.ShapeDtypeStruct((m, d), BF16)] if has_next else []),
        compiler_params=_params(("arbitrary",), 48),
        name="post_rows",
    )(m_in, *x_parts, *gains)
    x_new = out[:len(xo_specs)]
    return (x_new, out[-1]) if has_next else (x_new, None)


def _grid_step():
    return pl.program_id(0) * pl.num_programs(1) + pl.program_id(1)


def _side_cast(src_ref, dst_ref, n_blocks):
    @pl.when(_grid_step() < n_blocks)
    def _():
        dst_ref[...] = src_ref[...].astype(dst_ref.dtype)


def _side_cast_specs(src, layer, block_rows, n_steps, n_cols_grid):
    _, rows, cols = src.shape
    n_blocks = rows // block_rows
    assert rows % block_rows == 0 and n_blocks <= n_steps

    def block(i, j):
        return jnp.minimum(i * n_cols_grid + j, n_blocks - 1)

    in_spec = pl.BlockSpec((None, block_rows, cols), lambda i, j: (layer, block(i, j), 0))
    out_spec = pl.BlockSpec((block_rows, cols), lambda i, j: (block(i, j), 0))
    return in_spec, out_spec, jax.ShapeDtypeStruct((rows, cols), BF16), n_blocks


def _mm_kernel(a_ref, b_ref, *rest, epilogue, cast_blocks):
    o_ref = rest[-2] if cast_blocks else rest[-1]
    acc = jnp.dot(a_ref[...], b_ref[...].astype(BF16), preferred_element_type=F32)
    o_ref[...] = epilogue(acc).astype(o_ref.dtype)
    if cast_blocks:
        _side_cast(rest[0], rest[-1], cast_blocks)


def _mm2_kernel(a1_ref, a2_ref, b1_ref, b2_ref, *rest, cast_blocks):
    o_ref = rest[-2] if cast_blocks else rest[-1]
    acc = jnp.dot(a1_ref[...], b1_ref[...], preferred_element_type=F32)
    acc = acc + jnp.dot(a2_ref[...], b2_ref[...], preferred_element_type=F32)
    o_ref[...] = acc.astype(o_ref.dtype)
    if cast_blocks:
        _side_cast(rest[0], rest[-1], cast_blocks)


def _with_side_cast(cast, grid, in_specs, out_specs, out_shapes, operands):
    if cast is None:
        return 0
    src, layer, block_rows = cast
    i_spec, o_spec, o_shape, n_blocks = _side_cast_specs(
        src, layer, block_rows, grid[0] * grid[1], grid[1]
    )
    in_specs.append(i_spec)
    out_specs.append(o_spec)
    out_shapes.append(o_shape)
    operands.append(src)
    return n_blocks


def matmul_cols(a, w, layer, *, col_block0, n, tm, tn, out_dtype, epilogue, name,
                cast=None, vmem_mib=56):
    m, k = a.shape
    grid = (m // tm, n // tn)
    in_specs = [
        pl.BlockSpec((tm, k), lambda i, j: (i, 0)),
        pl.BlockSpec((None, k, tn), lambda i, j: (layer, 0, j + col_block0)),
    ]
    out_specs = [pl.BlockSpec((tm, tn), lambda i, j: (i, j))]
    out_shapes = [jax.ShapeDtypeStruct((m, n), out_dtype)]
    operands = [a, w]
    cast_blocks = _with_side_cast(cast, grid, in_specs, out_specs, out_shapes, operands)
    out = pl.pallas_call(
        functools.partial(_mm_kernel, epilogue=epilogue, cast_blocks=cast_blocks),
        grid=grid,
        in_specs=in_specs,
        out_specs=out_specs,
        out_shape=out_shapes,
        compiler_params=_params(("arbitrary", "arbitrary"), vmem_mib),
        name=name,
    )(*operands)
    return out[0], (out[1] if cast_blocks else None)


def _mm_in_kernel(a_ref, b_ref, src_ref, q_ref, kv_ref, gm_ref, dst_ref, *, nq, nkv,
                  cast_blocks):
    j = pl.program_id(1)

    def product():
        return jnp.dot(a_ref[...], b_ref[...].astype(BF16), preferred_element_type=F32)

    @pl.when(j < nq)
    def _():
        q_ref[...] = _scale_q(product()).astype(q_ref.dtype)

    @pl.when(jnp.logical_and(j >= nq, j < nq + nkv))
    def _():
        kv_ref[...] = product()

    @pl.when(j >= nq + nkv)
    def _():
        gm_ref[...] = _gelu(product()).astype(gm_ref.dtype)

    _side_cast(src_ref, dst_ref, cast_blocks)


def matmul_in(h, w, cast, *, tm, tn):
    m, k = h.shape
    nq, nkv, ngm = Q_COLS // tn, 2 * KV_COLS // tn, 2 * GM_WIDTH // tn
    grid = (m // tm, nq + nkv + ngm)
    in_specs = [
        pl.BlockSpec((tm, k), lambda i, j: (i, 0)),
        pl.BlockSpec((None, k, tn), lambda i, j: (0, 0, j)),
    ]
    out_specs = [
        pl.BlockSpec((tm, tn), lambda i, j: (i, jnp.minimum(j, nq - 1))),
        pl.BlockSpec((tm, tn), lambda i, j: (i, jnp.clip(j - nq, 0, nkv - 1))),
        pl.BlockSpec((tm, tn), lambda i, j: (i, jnp.clip(j - nq - nkv, 0, ngm - 1))),
    ]
    out_shapes = [
        jax.ShapeDtypeStruct((m, Q_COLS), BF16),
        jax.ShapeDtypeStruct((m, 2 * KV_COLS), F32),
        jax.ShapeDtypeStruct((m, 2 * GM_WIDTH), BF16),
    ]
    operands = [h, w]
    cast_blocks = _with_side_cast(cast, grid, in_specs, out_specs, out_shapes, operands)
    return pl.pallas_call(
        functools.partial(_mm_in_kernel, nq=nq, nkv=nkv, cast_blocks=cast_blocks),
        grid=grid,
        in_specs=in_specs,
        out_specs=out_specs,
        out_shape=out_shapes,
        compiler_params=_params(("arbitrary", "arbitrary"), 56),
        name="mm_in",
    )(*operands)


def matmul_two(a1, a2, w, layer, *, tm, tn, out_dtype, name, cast=None):
    m, k1 = a1.shape
    k2 = a2.shape[1]
    assert k1 == k2
    n = w.shape[-1]
    grid = (m // tm, n // tn)
    in_specs = [
        pl.BlockSpec((tm, k1), lambda i, j: (i, 0)),
        pl.BlockSpec((tm, k2), lambda i, j: (i, 0)),
        pl.BlockSpec((None, None, k1, tn), lambda i, j: (layer, 0, 0, j)),
        pl.BlockSpec((None, None, k2, tn), lambda i, j: (layer, 1, 0, j)),
    ]
    out_specs = [pl.BlockSpec((tm, tn), lambda i, j: (i, j))]
    out_shapes = [jax.ShapeDtypeStruct((m, n), out_dtype)]
    operands = [a1, a2, w, w]
    cast_blocks = _with_side_cast(cast, grid, in_specs, out_specs, out_shapes, operands)
    out = pl.pallas_call(
        functools.partial(_mm2_kernel, cast_blocks=cast_blocks),
        grid=grid,
        in_specs=in_specs,
        out_specs=out_specs,
        out_shape=out_shapes,
        compiler_params=_params(("arbitrary", "arbitrary"), 56),
        name=name,
    )(*operands)
    return out[0], (out[1] if cast_blocks else None)


def _identity(x):
    return x


def _scale_q(x):
    return x * (HEAD_DIM ** -0.5 * LOG2_E)


_GELU_A = -2.0 * LOG2_E * math.sqrt(2.0 / math.pi)
_GELU_B = _GELU_A * 0.044715


def _gelu_times(x, y):
    t = x * x * _GELU_B + _GELU_A
    return (x * y) / (1.0 + jnp.exp2(t * x))


def _gelu(x):
    t = x * x * _GELU_B + _GELU_A
    return x / (1.0 + jnp.exp2(t * x))


def _softmax2_with_sink(s, sink):
    m = jnp.maximum(jnp.max(s, axis=-1, keepdims=True), sink)
    e = jnp.exp2(s - m)
    den = jnp.sum(e, axis=-1, keepdims=True) + jnp.exp2(sink - m)
    return e / den


def _attn_prompt_kernel(sink_ref, q_ref, kvc_ref, kvp_ref, os_ref, o_ref, *,
                        blocks_per_seq, n_prompt_steps, per_step):
    step = pl.program_id(0)
    w = WINDOW

    @pl.when(step >= n_prompt_steps)
    def _():
        o_ref[...] = os_ref[...]

    @pl.when(step < n_prompt_steps)
    def _():
        for b in range(per_step):
            rows = pl.ds(b * w, w)
            prev_ref = kvp_ref if b == 0 else kvc_ref.at[pl.ds((b - 1) * w, w)]
            _attn_prompt_block(sink_ref, q_ref.at[rows], kvc_ref.at[rows], prev_ref,
                               o_ref.at[rows], (step * per_step + b) % blocks_per_seq)


def _attn_prompt_block(sink_ref, q_ref, kvc_ref, kvp_ref, o_ref, n):
    w = WINDOW
    row = lax.broadcasted_iota(jnp.int32, (w, 2 * w), 0)
    col = lax.broadcasted_iota(jnp.int32, (w, 2 * w), 1)
    prev_ok = jnp.logical_and(jnp.logical_and(col < w, col > row), n > 0)
    own_ok = jnp.logical_and(col >= w, col - w <= row)
    valid = jnp.logical_or(prev_ok, own_ok)
    for kvh in range(N_KV_HEADS):
        ks = slice(kvh * HEAD_DIM, (kvh + 1) * HEAD_DIM)
        vs = slice(KV_COLS + kvh * HEAD_DIM, KV_COLS + (kvh + 1) * HEAD_DIM)
        k = jnp.concatenate([kvp_ref[:, ks], kvc_ref[:, ks]], axis=0).astype(BF16)
        v = jnp.concatenate([kvp_ref[:, vs], kvc_ref[:, vs]], axis=0).astype(BF16)
        zeros = jnp.zeros((2 * w, HEAD_DIM), BF16)
        ones = jnp.ones((2 * w, LANES), BF16)
        pairs = [kvh * (GQA_GROUP // 2) + j for j in range(GQA_GROUP // 2)]
        qp = jnp.concatenate([q_ref[:, t * LANES:(t + 1) * LANES] for t in pairs], axis=0)
        outs = []
        for half in range(2):
            k_pad = jnp.concatenate([k, zeros] if half == 0 else [zeros, k], axis=1)
            v_ext = jnp.concatenate(
                [v, zeros, ones] if half == 0 else [zeros, v, ones], axis=1
            )
            s = lax.dot_general(
                qp, k_pad, (((1,), (1,)), ((), ())), preferred_element_type=F32
            )
            es, sink_terms = [], []
            for j, t in enumerate(pairs):
                sink = sink_ref[2 * t + half] * LOG2_E
                sg = jnp.where(valid, s[j * w:(j + 1) * w], -jnp.inf)
                m = jnp.maximum(jnp.max(sg, axis=-1, keepdims=True), sink)
                es.append(jnp.exp2(sg - m).astype(BF16))
                sink_terms.append(jnp.exp2(sink - m))
            o = jnp.dot(jnp.concatenate(es, axis=0), v_ext, preferred_element_type=F32)
            outs.append([
                o[j * w:(j + 1) * w, :LANES] / (o[j * w:(j + 1) * w, LANES:] + sink_terms[j])
                for j in range(len(pairs))
            ])
        for j, t in enumerate(pairs):
            o_ref[:, t * LANES:(t + 1) * LANES] = (outs[0][j] + outs[1][j]).astype(o_ref.dtype)


def attn_prompt(q, kv, sinks, att_sample, *, m_total, m_prompt, seq, per_step):
    w = WINDOW
    rows = per_step * w
    assert m_prompt % rows == 0 and att_sample.shape[0] % rows == 0
    nsteps = m_prompt // rows
    last = nsteps - 1
    last_prev = m_prompt // w - 1
    return pl.pallas_call(
        functools.partial(_attn_prompt_kernel, blocks_per_seq=seq // w,
                          n_prompt_steps=nsteps, per_step=per_step),
        grid=(m_total // rows,),
        in_specs=[
            pl.BlockSpec(memory_space=pltpu.SMEM),
            pl.BlockSpec((rows, Q_COLS), lambda r: (jnp.minimum(r, last), 0)),
            pl.BlockSpec((rows, 2 * KV_COLS), lambda r: (jnp.minimum(r, last), 0)),
            pl.BlockSpec((w, 2 * KV_COLS),
                         lambda r: (jnp.clip(per_step * r - 1, 0, last_prev), 0)),
            pl.BlockSpec((rows, Q_COLS), lambda r: (jnp.maximum(r - nsteps, 0), 0)),
        ],
        out_specs=pl.BlockSpec((rows, Q_COLS), lambda r: (r, 0)),
        out_shape=jax.ShapeDtypeStruct((m_total, Q_COLS), BF16),
        compiler_params=_params(("arbitrary",), 40),
        name="attn_prompt",
    )(sinks, q, kv, kv, att_sample)


def _attn_sample_kernel(sink_ref, q_ref, k_ref, v_ref, o_ref, *, t_new):
    nb, nkv, rows, d = q_ref.shape
    keys = k_ref.shape[2]
    q = q_ref[...].reshape(nb * nkv, rows, d)
    k = k_ref[...].reshape(nb * nkv, keys, d)
    v = v_ref[...].reshape(nb * nkv, keys, d)
    s = jnp.einsum("bqd,bkd->bqk", q, k, preferred_element_type=F32)
    t = lax.broadcasted_iota(jnp.int32, (rows, keys), 0) // GQA_GROUP
    col = lax.broadcasted_iota(jnp.int32, (rows, keys), 1)
    valid = jnp.logical_and(col > t, col <= WINDOW + t)
    s = jnp.where(valid[None], s, -jnp.inf)
    sink = jnp.broadcast_to(sink_ref[...][None], (nb, nkv, rows, 1)).reshape(
        nb * nkv, rows, 1
    )
    p = _softmax2_with_sink(s, sink * LOG2_E).astype(BF16)
    o = jnp.einsum("bqk,bkd->bqd", p, v, preferred_element_type=F32)
    o_ref[...] = o.reshape(nb, nkv, rows, d).astype(o_ref.dtype)


def attn_sample(q_s, k_new, v_new, k_buf, v_buf, sinks, *, nb):
    b, l = k_buf.shape[0], k_buf.shape[1]
    t_new = q_s.shape[0] // b
    rows = t_new * GQA_GROUP
    keys = 2 * WINDOW
    q4 = q_s.reshape(b, t_new, N_KV_HEADS, GQA_GROUP, HEAD_DIM)
    q4 = q4.transpose(0, 2, 1, 3, 4).reshape(b, N_KV_HEADS, rows, HEAD_DIM)

    def keys_layout(buf, new):
        cat = jnp.concatenate(
            [buf, new.reshape(b, t_new, N_KV_HEADS, HEAD_DIM)], axis=1
        )
        cat = jnp.pad(cat, ((0, 0), (0, keys - l - t_new), (0, 0), (0, 0)))
        return cat.transpose(0, 2, 1, 3).astype(BF16)

    k4 = keys_layout(k_buf, k_new)
    v4 = keys_layout(v_buf, v_new)
    sink_rows = jnp.broadcast_to(
        sinks.reshape(N_KV_HEADS, 1, GQA_GROUP), (N_KV_HEADS, t_new, GQA_GROUP)
    ).reshape(N_KV_HEADS, rows, 1)
    o4 = pl.pallas_call(
        functools.partial(_attn_sample_kernel, t_new=t_new),
        grid=(b // nb,),
        in_specs=[
            pl.BlockSpec((N_KV_HEADS, rows, 1), lambda i: (0, 0, 0)),
            pl.BlockSpec((nb, N_KV_HEADS, rows, HEAD_DIM), lambda i: (i, 0, 0, 0)),
            pl.BlockSpec((nb, N_KV_HEADS, keys, HEAD_DIM), lambda i: (i, 0, 0, 0)),
            pl.BlockSpec((nb, N_KV_HEADS, keys, HEAD_DIM), lambda i: (i, 0, 0, 0)),
        ],
        out_specs=pl.BlockSpec(
            (nb, N_KV_HEADS, rows, HEAD_DIM), lambda i: (i, 0, 0, 0)
        ),
        out_shape=jax.ShapeDtypeStruct((b, N_KV_HEADS, rows, HEAD_DIM), BF16),
        compiler_params=_params(("parallel",), 40),
        name="attn_sample",
    )(sink_rows, q4, k4, v4)
    o = o4.reshape(b, N_KV_HEADS, t_new, GQA_GROUP, HEAD_DIM).transpose(0, 2, 1, 3, 4)
    return o.reshape(b * t_new, Q_COLS)


def _place_rows_kernel(rows_ref, buf_ref, o_ref):
    del buf_ref
    o_ref[...] = rows_ref[...]


def place_rows(buf, rows, *, row_block):
    r, c = rows.shape
    return pl.pallas_call(
        _place_rows_kernel,
        grid=(1,),
        in_specs=[
            pl.BlockSpec((r, c), lambda i: (0, 0)),
            pl.BlockSpec(memory_space=pl.ANY),
        ],
        out_specs=pl.BlockSpec((r, c), lambda i: (row_block, 0)),
        out_shape=jax.ShapeDtypeStruct(buf.shape, buf.dtype),
        input_output_aliases={1: 0},
        compiler_params=_params(("arbitrary",), 16),
        name="place_rows",
    )(rows, buf)


def _gmlp_prompt_kernel(u_ref, v_ref, w_ref, bt_ref, gn_ref, os_ref, o_ref, *,
                        n_prompt_steps, per_step):
    @pl.when(pl.program_id(0) >= n_prompt_steps)
    def _():
        o_ref[...] = os_ref[...]

    @pl.when(pl.program_id(0) < n_prompt_steps)
    def _():
        for b in range(per_step):
            rows = pl.ds(b * CHUNK, CHUNK)
            _gmlp_prompt_chunk(u_ref.at[rows], v_ref.at[rows], w_ref, bt_ref, gn_ref,
                               o_ref.at[rows])


def _gmlp_prompt_chunk(u_ref, v_ref, w_ref, bt_ref, gn_ref, o_ref):
    gv = _rms(v_ref[...].astype(F32), gn_ref[...])
    c = CHUNK
    row = lax.broadcasted_iota(jnp.int32, (c, c), 0)
    col = lax.broadcasted_iota(jnp.int32, (c, c), 1)
    causal = col <= row
    for h in range(GM_HEADS):
        cs = slice(h * GM_HEAD_DIM, (h + 1) * GM_HEAD_DIM)
        w = jnp.where(causal, w_ref[h], 0.0).astype(BF16)
        mixed = jnp.dot(w, gv[:, cs].astype(BF16), preferred_element_type=F32)
        mixed = mixed + bt_ref[:, h:h + 1]
        o_ref[:, cs] = (u_ref[:, cs].astype(F32) * mixed).astype(o_ref.dtype)


def gmlp_prompt(gm, w_s, b_t, g_norm, gated_sample, *, m_total, m_prompt, per_step):
    c = CHUNK
    rows = per_step * c
    assert m_prompt % rows == 0 and gated_sample.shape[0] % rows == 0
    nsteps = m_prompt // rows
    last = nsteps - 1
    return pl.pallas_call(
        functools.partial(_gmlp_prompt_kernel, n_prompt_steps=nsteps, per_step=per_step),
        grid=(m_total // rows,),
        in_specs=[
            pl.BlockSpec((rows, GM_WIDTH), lambda r: (jnp.minimum(r, last), 0)),
            pl.BlockSpec((rows, GM_WIDTH), lambda r: (jnp.minimum(r, last), 1)),
            pl.BlockSpec((GM_HEADS, c, c), lambda r: (0, 0, 0)),
            pl.BlockSpec((c, GM_HEADS), lambda r: (0, 0)),
            pl.BlockSpec((1, GM_WIDTH), lambda r: (0, 0)),
            pl.BlockSpec((rows, GM_WIDTH), lambda r: (jnp.maximum(r - nsteps, 0), 0)),
        ],
        out_specs=pl.BlockSpec((rows, GM_WIDTH), lambda r: (r, 0)),
        out_shape=jax.ShapeDtypeStruct((m_total, GM_WIDTH), BF16),
        compiler_params=_params(("arbitrary",), 40),
        name="gmlp_prompt",
    )(gm, gm, w_s, b_t, g_norm, gated_sample)


def _gmlp_sample_kernel(u_ref, v_ref, wc_ref, bc_ref, gn_ref, o_ref, gv_ref, *, t_new):
    rows = u_ref.shape[0]
    nseq = rows // t_new
    gv = _rms(v_ref[...].astype(F32), gn_ref[...])
    gv_ref[...] = gv
    t = lax.broadcasted_iota(jnp.int32, (t_new, GM_HEAD_DIM), 0)
    for h in range(GM_HEADS):
        cs = slice(h * GM_HEAD_DIM, (h + 1) * GM_HEAD_DIM)
        x = gv[:, cs].reshape(nseq, t_new, GM_HEAD_DIM)
        acc = jnp.zeros((nseq, t_new, GM_HEAD_DIM), F32)
        for s in range(t_new):
            coef = jnp.where(t >= s, wc_ref[h, s], 0.0)
            acc = acc + coef[None] * x[:, s:s + 1, :]
        mixed = acc + bc_ref[h][None]
        u = u_ref[:, cs].astype(F32).reshape(nseq, t_new, GM_HEAD_DIM)
        o_ref[:, cs] = (u * mixed).reshape(rows, GM_HEAD_DIM).astype(o_ref.dtype)


def gmlp_sample(gm, w_s, b_s, g_norm, *, m_prompt, m_sample, t_new):
    lanes = GM_HEAD_DIM
    w8 = w_s[:, :t_new, :t_new]
    wc = jnp.broadcast_to(
        w8.transpose(0, 2, 1)[..., None], (GM_HEADS, t_new, t_new, lanes)
    )
    bc = jnp.broadcast_to(b_s[:, :t_new, None], (GM_HEADS, t_new, lanes))
    blk = m_prompt // m_sample
    return pl.pallas_call(
        functools.partial(_gmlp_sample_kernel, t_new=t_new),
        grid=(1,),
        in_specs=[
            pl.BlockSpec((m_sample, GM_WIDTH), lambda i: (blk, 0)),
            pl.BlockSpec((m_sample, GM_WIDTH), lambda i: (blk, 1)),
            pl.BlockSpec((GM_HEADS, t_new, t_new, lanes), lambda i: (0, 0, 0, 0)),
            pl.BlockSpec((GM_HEADS, t_new, lanes), lambda i: (0, 0, 0)),
            pl.BlockSpec((1, GM_WIDTH), lambda i: (0, 0)),
        ],
        out_specs=[
            pl.BlockSpec((m_sample, GM_WIDTH), lambda i: (0, 0)),
            pl.BlockSpec((m_sample, GM_WIDTH), lambda i: (0, 0)),
        ],
        out_shape=[
            jax.ShapeDtypeStruct((m_sample, GM_WIDTH), BF16),
            jax.ShapeDtypeStruct((m_sample, GM_WIDTH), F32),
        ],
        compiler_params=_params(("arbitrary",), 40),
        name="gmlp_sample",
    )(gm, gm, wc, bc, g_norm)


def _conv_gelu_gate(g, g1, g2, up, cw_ref, cb_ref):
    c = cb_ref[...] + cw_ref[0:1, :] * g2
    c = c + cw_ref[1:2, :] * g1
    c = c + cw_ref[2:3, :] * g
    return _gelu_times(c, up)


def _ffn_a_kernel(h_ref, hs_ref, wg_ref, wu_ref, cw_ref, cb_ref, st_ref, wd_ref, a_ref,
                  tail_ref, as_ref, gs_ref, wdb_ref, gx_ref, *, chunk, t_new,
                  cast_blocks):
    i = pl.program_id(0)
    _side_cast(wd_ref, wdb_ref, cast_blocks)
    tm = h_ref.shape[0]
    tf = wg_ref.shape[1]
    wg = wg_ref[...].astype(BF16)
    wu = wu_ref[...].astype(BF16)

    prev8 = jnp.zeros((8, tf), F32)
    for r0 in range(0, tm, chunk):
        h = h_ref[r0:r0 + chunk, :]
        g = jnp.dot(h, wg, preferred_element_type=F32)
        up = jnp.dot(h, wu, preferred_element_type=F32)
        gx_ref[0:8, :] = prev8
        gx_ref[8:8 + chunk, :] = g
        g1 = gx_ref[7:7 + chunk, :]
        g2 = gx_ref[6:6 + chunk, :]
        a_ref[r0:r0 + chunk, :] = _conv_gelu_gate(g, g1, g2, up, cw_ref, cb_ref).astype(
            a_ref.dtype
        )
        prev8 = g[chunk - 8:, :]
    tail_ref[...] = prev8

    @pl.when(i == pl.num_programs(0) - 1)
    def _():
        hs = hs_ref[...]
        rows = hs.shape[0]
        g = jnp.dot(hs, wg, preferred_element_type=F32)
        up = jnp.dot(hs, wu, preferred_element_type=F32)
        gs_ref[...] = g
        t = lax.broadcasted_iota(jnp.int32, g.shape, 0) % t_new
        st2 = st_ref[...]
        st1 = pltpu.roll(st2, rows - 1, 0)
        g1 = jnp.where(t >= 1, pltpu.roll(g, 1, 0), st1)
        g2 = jnp.where(t >= 2, pltpu.roll(g, 2, 0), st2)
        as_ref[...] = _conv_gelu_gate(g, g1, g2, up, cw_ref, cb_ref).astype(as_ref.dtype)


def ffn_a(h, w_gate, w_up, w_down, conv_w, conv_b, conv_state, layer, *, m_prompt,
          m_sample, seq, t_new, tf, chunk, cast_rows):
    m_total, d = h.shape
    f = w_gate.shape[-1]
    nf = f // tf
    ni = m_prompt // seq
    assert t_new % 8 == 0 and t_new >= CONV_W - 1
    st = jnp.pad(conv_state, ((0, 0), (0, t_new - (CONV_W - 1)), (0, 0)))
    st = st.reshape(m_sample, f)
    sample_col = lambda i, j: (0, jnp.where(i == ni - 1, j, 0))
    wd_in, wd_out, wd_shape, cast_blocks = _side_cast_specs(
        w_down, layer, cast_rows, ni * nf, nf
    )
    return pl.pallas_call(
        functools.partial(_ffn_a_kernel, chunk=chunk, t_new=t_new,
                          cast_blocks=cast_blocks),
        grid=(ni, nf),
        in_specs=[
            pl.BlockSpec((seq, d), lambda i, j: (i, 0), pipeline_mode=pl.Buffered(1)),
            pl.BlockSpec((m_sample, d), lambda i, j: (m_prompt // m_sample, 0)),
            pl.BlockSpec((None, d, tf), lambda i, j: (layer, 0, j)),
            pl.BlockSpec((None, d, tf), lambda i, j: (layer, 0, j)),
            pl.BlockSpec((None, CONV_W, tf), lambda i, j: (layer, 0, j)),
            pl.BlockSpec((None, 1, tf), lambda i, j: (layer, 0, j)),
            pl.BlockSpec((m_sample, tf), lambda i, j: (0, j)),
            wd_in,
        ],
        out_specs=[
            pl.BlockSpec((seq, tf), lambda i, j: (i, j)),
            pl.BlockSpec((None, 8, tf), lambda i, j: (i, 0, j)),
            pl.BlockSpec((m_sample, tf), sample_col),
            pl.BlockSpec((m_sample, tf), sample_col),
            wd_out,
        ],
        out_shape=[
            jax.ShapeDtypeStruct((m_total, f), BF16),
            jax.ShapeDtypeStruct((ni, 8, f), F32),
            jax.ShapeDtypeStruct((m_sample, f), BF16),
            jax.ShapeDtypeStruct((m_sample, f), F32),
            wd_shape,
        ],
        scratch_shapes=[pltpu.VMEM((chunk + 8, tf), F32)],
        compiler_params=_params(("arbitrary", "arbitrary"), 56),
        name="ffn_a",
    )(h, h, w_gate, w_up, conv_w, conv_b, st, w_down)


def kernel(x_prompt, x_sample, state_swa_k, state_swa_v, state_conv, w_in, w_out,
           attn_sinks, gm_spatial, gm_bias, gm_norm, norm_pre_mix, norm_post_mix,
           norm_pre_ffn, norm_post_ffn, w_ffn_gate, w_ffn_up, w_ffn_down, conv_w,
           conv_b):
    batch, seq, d = x_prompt.shape
    dec_batch, dec_seq, _ = x_sample.shape
    depth = w_in.shape[0]
    f = w_ffn_gate.shape[-1]
    mp = batch * seq
    ms = dec_batch * dec_seq
    m = mp + ms
    keep = min(WINDOW, seq)

    tm_mm = m // 8
    tm_down = m // 12
    tf = 256
    tm_rows = ms
    assert mp % tm_rows == 0

    w_in_b = w_in
    conv_b3 = conv_b.reshape(depth, 1, f)
    cast_rows = 64

    x = (x_prompt.reshape(mp, d), x_sample.reshape(ms, d))
    h = rms_rows(x, norm_pre_mix[0].reshape(1, d), tm=tm_rows)

    p_k, p_v, p_c, s_k, s_v, s_c, s_g = [], [], [], [], [], [], []
    for l in range(depth):
        last = l + 1 == depth
        tn = 512
        q, kv, gm, w_out_b = matmul_in(h, w_in_b, (w_out, l, cast_rows), tm=tm_mm, tn=tn)
        w_out_b = w_out_b.reshape(1, 2, Q_COLS, d)

        kv_p = jnp.stack([kv[(b + 1) * seq - keep:(b + 1) * seq] for b in range(batch)])
        kv_s = kv[mp:]
        k_s, v_s = kv_s[:, :KV_COLS], kv_s[:, KV_COLS:]
        p_k.append(kv_p[..., :KV_COLS].reshape(batch, keep, N_KV_HEADS, HEAD_DIM))
        p_v.append(kv_p[..., KV_COLS:].reshape(batch, keep, N_KV_HEADS, HEAD_DIM))
        s_k.append(k_s.reshape(dec_batch, dec_seq, N_KV_HEADS, HEAD_DIM))
        s_v.append(v_s.reshape(dec_batch, dec_seq, N_KV_HEADS, HEAD_DIM))

        att_s = attn_sample(q[mp:], k_s, v_s, state_swa_k[l], state_swa_v[l],
                            attn_sinks[l], nb=8)
        att = attn_prompt(q, kv, attn_sinks[l], att_s, m_total=m, m_prompt=mp, seq=seq,
                          per_step=2)

        gmo_s, gv_s = gmlp_sample(gm, gm_spatial[l], gm_bias[l], gm_norm[l].reshape(1, -1),
                                  m_prompt=mp, m_sample=ms, t_new=dec_seq)
        gmo = gmlp_prompt(gm, gm_spatial[l], gm_bias[l].T, gm_norm[l].reshape(1, -1),
                          gmo_s, m_total=m, m_prompt=mp, per_step=2)
        s_g.append(gv_s.reshape(dec_batch, dec_seq, GM_HEADS, GM_HEAD_DIM))

        mix, w_in_next = matmul_two(att, gmo, w_out_b, 0, tm=tm_mm, tn=1024,
                                    out_dtype=BF16, name="mm_out",
                                    cast=None if last else (w_in, l + 1, 2 * cast_rows))
        x, h = post_rows(mix, x, norm_post_mix[l].reshape(1, d),
                         norm_pre_ffn[l].reshape(1, d), tm=tm_rows)

        a, tail, a_s, g_s, w_down_b = ffn_a(
            h, w_ffn_gate, w_ffn_up, w_ffn_down, conv_w, conv_b3, state_conv[l], l,
            m_prompt=mp, m_sample=ms, seq=seq, t_new=dec_seq, tf=tf, chunk=256,
            cast_rows=2 * cast_rows)
        a = place_rows(a, a_s, row_block=mp // ms)
        p_c.append(tail[:, 8 - (CONV_W - 1):])
        s_c.append(g_s.reshape(dec_batch, dec_seq, f)[:, dec_seq - (CONV_W - 1):])

        ffn, _ = matmul_cols(a, w_down_b[None], 0, col_block0=0, n=d, tm=tm_down, tn=512,
                             out_dtype=BF16, epilogue=_identity, name="mm_down",
                             vmem_mib=58)
        x, h = post_rows(ffn, x, norm_post_ffn[l].reshape(1, d),
                         None if last else norm_pre_mix[l + 1].reshape(1, d),
                         tm=tm_rows, split_out=(mp, ms) if last else None)
        if not last:
            w_in_b = w_in_next[None]

    x_p, x_s = x
    return (x_p.reshape(batch, seq, d), x_s.reshape(dec_batch, dec_seq, d),
            jnp.stack(p_k), jnp.stack(p_v), jnp.stack(p_c),
            jnp.stack(s_k), jnp.stack(s_v), jnp.stack(s_c), jnp.stack(s_g))
```

```python
import functools
import math

import jax
import jax.numpy as jnp
from jax import lax
from jax.experimental import pallas as pl
from jax.experimental.pallas import tpu as pltpu

F32 = jnp.float32
BF16 = jnp.bfloat16

D_MODEL = 4096
HEAD_DIM = 64
N_HEADS = 32
N_KV_HEADS = 4
GQA_GROUP = 8
WINDOW = 128
CHUNK = 128
GM_HEAD_DIM = 128
GM_HEADS = 16
GM_WIDTH = 2048
Q_COLS = 2048
KV_COLS = 256
CONV_W = 3
EPS = 1e-6
LANES = 128
LOG2_E = math.log2(math.e)

MIB = 1024 * 1024


def _params(semantics, vmem_mib):
    return pltpu.CompilerParams(
        dimension_semantics=semantics, vmem_limit_bytes=vmem_mib * MIB
    )


def _rms(x, g):
    ms = jnp.mean(x * x, axis=-1, keepdims=True)
    return x * lax.rsqrt(ms + EPS) * g


def _split_specs(tm, d, n_prompt_tiles):
    return [
        pl.BlockSpec((tm, d), lambda i: (jnp.minimum(i, n_prompt_tiles - 1), 0)),
        pl.BlockSpec((tm, d), lambda i: (0, 0)),
    ]


def _load_rows(refs, n_prompt_tiles):
    if len(refs) == 1:
        return refs[0][...]
    is_prompt = pl.program_id(0) < n_prompt_tiles
    return jnp.where(is_prompt, refs[0][...], refs[1][...])


def _store_rows(refs, value, n_prompt_tiles):
    if len(refs) == 1:
        refs[0][...] = value
        return

    @pl.when(pl.program_id(0) < n_prompt_tiles)
    def _():
        refs[0][...] = value

    @pl.when(pl.program_id(0) >= n_prompt_tiles)
    def _():
        refs[1][...] = value


def _rms_kernel(*refs, n_in, n_prompt_tiles):
    x = _load_rows(refs[:n_in], n_prompt_tiles)
    g_ref, o_ref = refs[n_in:]
    o_ref[...] = _rms(x, g_ref[...]).astype(o_ref.dtype)


def rms_rows(x_parts, g, *, tm):
    d = x_parts[0].shape[1]
    m = sum(p.shape[0] for p in x_parts)
    npt = x_parts[0].shape[0] // tm
    row = pl.BlockSpec((tm, d), lambda i: (i, 0))
    x_specs = [row] if len(x_parts) == 1 else _split_specs(tm, d, npt)
    return pl.pallas_call(
        functools.partial(_rms_kernel, n_in=len(x_parts), n_prompt_tiles=npt),
        grid=(m // tm,),
        in_specs=x_specs + [pl.BlockSpec((1, d), lambda i: (0, 0))],
        out_specs=row,
        out_shape=jax.ShapeDtypeStruct((m, d), BF16),
        compiler_params=_params(("arbitrary",), 40),
        name="rms_rows",
    )(*x_parts, g)


def _post_kernel(*refs, n_in, n_out, has_next, n_prompt_tiles):
    m_ref = refs[0]
    x = _load_rows(refs[1:1 + n_in], n_prompt_tiles)
    rest = refs[1 + n_in:]
    xn = x + _rms(m_ref[...].astype(F32), rest[0][...])
    rest = rest[1:]
    if has_next:
        gnext_ref, rest = rest[0], rest[1:]
    _store_rows(rest[:n_out], xn, n_prompt_tiles)
    if has_next:
        ho_ref = rest[n_out]
        ho_ref[...] = _rms(xn, gnext_ref[...]).astype(ho_ref.dtype)


def post_rows(m_in, x_parts, g_post, g_next, *, tm, split_out=None):
    m, d = m_in.shape
    npt = (x_parts[0].shape[0] if split_out is None else split_out[0]) // tm
    if len(x_parts) == 1 and split_out is None:
        npt = m // tm
    row = pl.BlockSpec((tm, d), lambda i: (i, 0))
    vec = pl.BlockSpec((1, d), lambda i: (0, 0))
    x_specs = [row] if len(x_parts) == 1 else _split_specs(tm, d, npt)
    if split_out is None:
        xo_specs = [row]
        xo_shapes = [jax.ShapeDtypeStruct((m, d), F32)]
    else:
        xo_specs = _split_specs(tm, d, npt)
        xo_shapes = [jax.ShapeDtypeStruct((r, d), F32) for r in split_out]
    has_next = g_next is not None
    gains = [g_post, g_next] if has_next else [g_post]
    out = pl.pallas_call(
        functools.partial(_post_kernel, n_in=len(x_parts), n_out=len(xo_specs),
                          has_next=has_next, n_prompt_tiles=npt),
        grid=(m // tm,),
        in_specs=[row] + x_specs + [vec] * len(gains),
        out_specs=xo_specs + ([row] if has_next else []),
        out_shape=xo_shapes
        + ([jax.ShapeDtypeStruct((m, d), BF16)] if has_next else []),
        compiler_params=_params(("arbitrary",), 48),
        name="post_rows",
    )(m_in, *x_parts, *gains)
    x_new = out[:len(xo_specs)]
    return (x_new, out[-1]) if has_next else (x_new, None)


def _grid_step():
    return pl.program_id(0) * pl.num_programs(1) + pl.program_id(1)


def _side_cast(src_ref, dst_ref, n_blocks):
    @pl.when(_grid_step() < n_blocks)
    def _():
        dst_ref[...] = src_ref[...].astype(dst_ref.dtype)


def _side_cast_specs(src, layer, block_rows, n_steps, n_cols_grid):
    _, rows, cols = src.shape
    n_blocks = rows // block_rows
    assert rows % block_rows == 0 and n_blocks <= n_steps

    def block(i, j):
        return jnp.minimum(i * n_cols_grid + j, n_blocks - 1)

    in_spec = pl.BlockSpec((None, block_rows, cols), lambda i, j: (layer, block(i, j), 0))
    out_spec = pl.BlockSpec((block_rows, cols), lambda i, j: (block(i, j), 0))
    return in_spec, out_spec, jax.ShapeDtypeStruct((rows, cols), BF16), n_blocks


def _mm_kernel(a_ref, b_ref, *rest, epilogue, cast_blocks):
    o_ref = rest[-2] if cast_blocks else rest[-1]
    acc = jnp.dot(a_ref[...], b_ref[...].astype(BF16), preferred_element_type=F32)
    o_ref[...] = epilogue(acc).astype(o_ref.dtype)
    if cast_blocks:
        _side_cast(rest[0], rest[-1], cast_blocks)


def _mm2_kernel(a1_ref, a2_ref, b1_ref, b2_ref, *rest, cast_blocks):
    o_ref = rest[-2] if cast_blocks else rest[-1]
    acc = jnp.dot(a1_ref[...], b1_ref[...], preferred_element_type=F32)
    acc = acc + jnp.dot(a2_ref[...], b2_ref[...], preferred_element_type=F32)
    o_ref[...] = acc.astype(o_ref.dtype)
    if cast_blocks:
        _side_cast(rest[0], rest[-1], cast_blocks)


def _with_side_cast(cast, grid, in_specs, out_specs, out_shapes, operands):
    if cast is None:
        return 0
    src, layer, block_rows = cast
    i_spec, o_spec, o_shape, n_blocks = _side_cast_specs(
        src, layer, block_rows, grid[0] * grid[1], grid[1]
    )
    in_specs.append(i_spec)
    out_specs.append(o_spec)
    out_shapes.append(o_shape)
    operands.append(src)
    return n_blocks


def matmul_cols(a, w, layer, *, col_block0, n, tm, tn, out_dtype, epilogue, name,
                cast=None, vmem_mib=56):
    m, k = a.shape
    grid = (m // tm, n // tn)
    in_specs = [
        pl.BlockSpec((tm, k), lambda i, j: (i, 0)),
        pl.BlockSpec((None, k, tn), lambda i, j: (layer, 0, j + col_block0)),
    ]
    out_specs = [pl.BlockSpec((tm, tn), lambda i, j: (i, j))]
    out_shapes = [jax.ShapeDtypeStruct((m, n), out_dtype)]
    operands = [a, w]
    cast_blocks = _with_side_cast(cast, grid, in_specs, out_specs, out_shapes, operands)
    out = pl.pallas_call(
        functools.partial(_mm_kernel, epilogue=epilogue, cast_blocks=cast_blocks),
        grid=grid,
        in_specs=in_specs,
        out_specs=out_specs,
        out_shape=out_shapes,
        compiler_params=_params(("arbitrary", "arbitrary"), vmem_mib),
        name=name,
    )(*operands)
    return out[0], (out[1] if cast_blocks else None)


def _mm_in_kernel(a_ref, b_ref, src_ref, q_ref, kv_ref, gm_ref, dst_ref, *, nq, nkv,
                  cast_blocks):
    j = pl.program_id(1)

    def product():
        return jnp.dot(a_ref[...], b_ref[...].astype(BF16), preferred_element_type=F32)

    @pl.when(j < nq)
    def _():
        q_ref[...] = _scale_q(product()).astype(q_ref.dtype)

    @pl.when(jnp.logical_and(j >= nq, j < nq + nkv))
    def _():
        kv_ref[...] = product()

    @pl.when(j >= nq + nkv)
    def _():
        gm_ref[...] = _gelu(product()).astype(gm_ref.dtype)

    _side_cast(src_ref, dst_ref, cast_blocks)


def matmul_in(h, w, cast, *, tm, tn):
    m, k = h.shape
    nq, nkv, ngm = Q_COLS // tn, 2 * KV_COLS // tn, 2 * GM_WIDTH // tn
    grid = (m // tm, nq + nkv + ngm)
    in_specs = [
        pl.BlockSpec((tm, k), lambda i, j: (i, 0)),
        pl.BlockSpec((None, k, tn), lambda i, j: (0, 0, j)),
    ]
    out_specs = [
        pl.BlockSpec((tm, tn), lambda i, j: (i, jnp.minimum(j, nq - 1))),
        pl.BlockSpec((tm, tn), lambda i, j: (i, jnp.clip(j - nq, 0, nkv - 1))),
        pl.BlockSpec((tm, tn), lambda i, j: (i, jnp.clip(j - nq - nkv, 0, ngm - 1))),
    ]
    out_shapes = [
        jax.ShapeDtypeStruct((m, Q_COLS), BF16),
        jax.ShapeDtypeStruct((m, 2 * KV_COLS), F32),
        jax.ShapeDtypeStruct((m, 2 * GM_WIDTH), BF16),
    ]
    operands = [h, w]
    cast_blocks = _with_side_cast(cast, grid, in_specs, out_specs, out_shapes, operands)
    return pl.pallas_call(
        functools.partial(_mm_in_kernel, nq=nq, nkv=nkv, cast_blocks=cast_blocks),
        grid=grid,
        in_specs=in_specs,
        out_specs=out_specs,
        out_shape=out_shapes,
        compiler_params=_params(("arbitrary", "arbitrary"), 56),
        name="mm_in",
    )(*operands)


def matmul_two(a1, a2, w, layer, *, tm, tn, out_dtype, name, cast=None):
    m, k1 = a1.shape
    k2 = a2.shape[1]
    assert k1 == k2
    n = w.shape[-1]
    grid = (m // tm, n // tn)
    in_specs = [
        pl.BlockSpec((tm, k1), lambda i, j: (i, 0)),
        pl.BlockSpec((tm, k2), lambda i, j: (i, 0)),
        pl.BlockSpec((None, None, k1, tn), lambda i, j: (layer, 0, 0, j)),
        pl.BlockSpec((None, None, k2, tn), lambda i, j: (layer, 1, 0, j)),
    ]
    out_specs = [pl.BlockSpec((tm, tn), lambda i, j: (i, j))]
    out_shapes = [jax.ShapeDtypeStruct((m, n), out_dtype)]
    operands = [a1, a2, w, w]
    cast_blocks = _with_side_cast(cast, grid, in_specs, out_specs, out_shapes, operands)
    out = pl.pallas_call(
        functools.partial(_mm2_kernel, cast_blocks=cast_blocks),
        grid=grid,
        in_specs=in_specs,
        out_specs=out_specs,
        out_shape=out_shapes,
        compiler_params=_params(("arbitrary", "arbitrary"), 56),
        name=name,
    )(*operands)
    return out[0], (out[1] if cast_blocks else None)


def _identity(x):
    return x


def _scale_q(x):
    return x * (HEAD_DIM ** -0.5 * LOG2_E)


_GELU_A = -2.0 * LOG2_E * math.sqrt(2.0 / math.pi)
_GELU_B = _GELU_A * 0.044715


def _gelu_times(x, y):
    t = x * x * _GELU_B + _GELU_A
    return (x * y) / (1.0 + jnp.exp2(t * x))


def _gelu(x):
    t = x * x * _GELU_B + _GELU_A
    return x / (1.0 + jnp.exp2(t * x))


def _softmax2_with_sink(s, sink):
    m = jnp.maximum(jnp.max(s, axis=-1, keepdims=True), sink)
    e = jnp.exp2(s - m)
    den = jnp.sum(e, axis=-1, keepdims=True) + jnp.exp2(sink - m)
    return e / den


def _attn_prompt_kernel(sink_ref, q_ref, kvc_ref, kvp_ref, os_ref, o_ref, *,
                        blocks_per_seq, n_prompt_steps, per_step):
    step = pl.program_id(0)
    w = WINDOW

    @pl.when(step >= n_prompt_steps)
    def _():
        o_ref[...] = os_ref[...]

    @pl.when(step < n_prompt_steps)
    def _():
        for b in range(per_step):
            rows = pl.ds(b * w, w)
            prev_ref = kvp_ref if b == 0 else kvc_ref.at[pl.ds((b - 1) * w, w)]
            _attn_prompt_block(sink_ref, q_ref.at[rows], kvc_ref.at[rows], prev_ref,
                               o_ref.at[rows], (step * per_step + b) % blocks_per_seq)


def _attn_prompt_block(sink_ref, q_ref, kvc_ref, kvp_ref, o_ref, n):
    w = WINDOW
    row = lax.broadcasted_iota(jnp.int32, (w, 2 * w), 0)
    col = lax.broadcasted_iota(jnp.int32, (w, 2 * w), 1)
    prev_ok = jnp.logical_and(jnp.logical_and(col < w, col > row), n > 0)
    own_ok = jnp.logical_and(col >= w, col - w <= row)
    valid = jnp.logical_or(prev_ok, own_ok)
    for kvh in range(N_KV_HEADS):
        ks = slice(kvh * HEAD_DIM, (kvh + 1) * HEAD_DIM)
        vs = slice(KV_COLS + kvh * HEAD_DIM, KV_COLS + (kvh + 1) * HEAD_DIM)
        k = jnp.concatenate([kvp_ref[:, ks], kvc_ref[:, ks]], axis=0).astype(BF16)
        v = jnp.concatenate([kvp_ref[:, vs], kvc_ref[:, vs]], axis=0).astype(BF16)
        zeros = jnp.zeros((2 * w, HEAD_DIM), BF16)
        ones = jnp.ones((2 * w, LANES), BF16)
        pairs = [kvh * (GQA_GROUP // 2) + j for j in range(GQA_GROUP // 2)]
        qp = jnp.concatenate([q_ref[:, t * LANES:(t + 1) * LANES] for t in pairs], axis=0)
        outs = []
        for half in range(2):
            k_pad = jnp.concatenate([k, zeros] if half == 0 else [zeros, k], axis=1)
            v_ext = jnp.concatenate(
                [v, zeros, ones] if half == 0 else [zeros, v, ones], axis=1
            )
            s = lax.dot_general(
                qp, k_pad, (((1,), (1,)), ((), ())), preferred_element_type=F32
            )
            es, sink_terms = [], []
            for j, t in enumerate(pairs):
                sink = sink_ref[2 * t + half] * LOG2_E
                sg = jnp.where(valid, s[j * w:(j + 1) * w], -jnp.inf)
                m = jnp.maximum(jnp.max(sg, axis=-1, keepdims=True), sink)
                es.append(jnp.exp2(sg - m).astype(BF16))
                sink_terms.append(jnp.exp2(sink - m))
            o = jnp.dot(jnp.concatenate(es, axis=0), v_ext, preferred_element_type=F32)
            outs.append([
                o[j * w:(j + 1) * w, :LANES] / (o[j * w:(j + 1) * w, LANES:] + sink_terms[j])
                for j in range(len(pairs))
            ])
        for j, t in enumerate(pairs):
            o_ref[:, t * LANES:(t + 1) * LANES] = (outs[0][j] + outs[1][j]).astype(o_ref.dtype)


def attn_prompt(q, kv, sinks, att_sample, *, m_total, m_prompt, seq, per_step):
    w = WINDOW
    rows = per_step * w
    assert m_prompt % rows == 0 and att_sample.shape[0] % rows == 0
    nsteps = m_prompt // rows
    last = nsteps - 1
    last_prev = m_prompt // w - 1
    return pl.pallas_call(
        functools.partial(_attn_prompt_kernel, blocks_per_seq=seq // w,
                          n_prompt_steps=nsteps, per_step=per_step),
        grid=(m_total // rows,),
        in_specs=[
            pl.BlockSpec(memory_space=pltpu.SMEM),
            pl.BlockSpec((rows, Q_COLS), lambda r: (jnp.minimum(r, last), 0)),
            pl.BlockSpec((rows, 2 * KV_COLS), lambda r: (jnp.minimum(r, last), 0)),
            pl.BlockSpec((w, 2 * KV_COLS),
                         lambda r: (jnp.clip(per_step * r - 1, 0, last_prev), 0)),
            pl.BlockSpec((rows, Q_COLS), lambda r: (jnp.maximum(r - nsteps, 0), 0)),
        ],
        out_specs=pl.BlockSpec((rows, Q_COLS), lambda r: (r, 0)),
        out_shape=jax.ShapeDtypeStruct((m_total, Q_COLS), BF16),
        compiler_params=_params(("arbitrary",), 40),
        name="attn_prompt",
    )(sinks, q, kv, kv, att_sample)


def _attn_sample_kernel(sink_ref, q_ref, k_ref, v_ref, o_ref, *, t_new):
    nb, nkv, rows, d = q_ref.shape
    keys = k_ref.shape[2]
    q = q_ref[...].reshape(nb * nkv, rows, d)
    k = k_ref[...].reshape(nb * nkv, keys, d)
    v = v_ref[...].reshape(nb * nkv, keys, d)
    s = jnp.einsum("bqd,bkd->bqk", q, k, preferred_element_type=F32)
    t = lax.broadcasted_iota(jnp.int32, (rows, keys), 0) // GQA_GROUP
    col = lax.broadcasted_iota(jnp.int32, (rows, keys), 1)
    valid = jnp.logical_and(col > t, col <= WINDOW + t)
    s = jnp.where(valid[None], s, -jnp.inf)
    sink = jnp.broadcast_to(sink_ref[...][None], (nb, nkv, rows, 1)).reshape(
        nb * nkv, rows, 1
    )
    p = _softmax2_with_sink(s, sink * LOG2_E).astype(BF16)
    o = jnp.einsum("bqk,bkd->bqd", p, v, preferred_element_type=F32)
    o_ref[...] = o.reshape(nb, nkv, rows, d).astype(o_ref.dtype)


def attn_sample(q_s, k_new, v_new, k_buf, v_buf, sinks, *, nb):
    b, l = k_buf.shape[0], k_buf.shape[1]
    t_new = q_s.shape[0] // b
    rows = t_new * GQA_GROUP
    keys = 2 * WINDOW
    q4 = q_s.reshape(b, t_new, N_KV_HEADS, GQA_GROUP, HEAD_DIM)
    q4 = q4.transpose(0, 2, 1, 3, 4).reshape(b, N_KV_HEADS, rows, HEAD_DIM)

    def keys_layout(buf, new):
        cat = jnp.concatenate(
            [buf, new.reshape(b, t_new, N_KV_HEADS, HEAD_DIM)], axis=1
        )
        cat = jnp.pad(cat, ((0, 0), (0, keys - l - t_new), (0, 0), (0, 0)))
        return cat.transpose(0, 2, 1, 3).astype(BF16)

    k4 = keys_layout(k_buf, k_new)
    v4 = keys_layout(v_buf, v_new)
    sink_rows = jnp.broadcast_to(
        sinks.reshape(N_KV_HEADS, 1, GQA_GROUP), (N_KV_HEADS, t_new, GQA_GROUP)
    ).reshape(N_KV_HEADS, rows, 1)
    o4 = pl.pallas_call(
        functools.partial(_attn_sample_kernel, t_new=t_new),
        grid=(b // nb,),
        in_specs=[
            pl.BlockSpec((N_KV_HEADS, rows, 1), lambda i: (0, 0, 0)),
            pl.BlockSpec((nb, N_KV_HEADS, rows, HEAD_DIM), lambda i: (i, 0, 0, 0)),
            pl.BlockSpec((nb, N_KV_HEADS, keys, HEAD_DIM), lambda i: (i, 0, 0, 0)),
            pl.BlockSpec((nb, N_KV_HEADS, keys, HEAD_DIM), lambda i: (i, 0, 0, 0)),
        ],
        out_specs=pl.BlockSpec(
            (nb, N_KV_HEADS, rows, HEAD_DIM), lambda i: (i, 0, 0, 0)
        ),
        out_shape=jax.ShapeDtypeStruct((b, N_KV_HEADS, rows, HEAD_DIM), BF16),
        compiler_params=_params(("parallel",), 40),
        name="attn_sample",
    )(sink_rows, q4, k4, v4)
    o = o4.reshape(b, N_KV_HEADS, t_new, GQA_GROUP, HEAD_DIM).transpose(0, 2, 1, 3, 4)
    return o.reshape(b * t_new, Q_COLS)


def _place_rows_kernel(rows_ref, buf_ref, o_ref):
    del buf_ref
    o_ref[...] = rows_ref[...]


def place_rows(buf, rows, *, row_block):
    r, c = rows.shape
    return pl.pallas_call(
        _place_rows_kernel,
        grid=(1,),
        in_specs=[
            pl.BlockSpec((r, c), lambda i: (0, 0)),
            pl.BlockSpec(memory_space=pl.ANY),
        ],
        out_specs=pl.BlockSpec((r, c), lambda i: (row_block, 0)),
        out_shape=jax.ShapeDtypeStruct(buf.shape, buf.dtype),
        input_output_aliases={1: 0},
        compiler_params=_params(("arbitrary",), 16),
        name="place_rows",
    )(rows, buf)


def _gmlp_prompt_kernel(u_ref, v_ref, w_ref, bt_ref, gn_ref, os_ref, o_ref, *,
                        n_prompt_steps, per_step):
    @pl.when(pl.program_id(0) >= n_prompt_steps)
    def _():
        o_ref[...] = os_ref[...]

    @pl.when(pl.program_id(0) < n_prompt_steps)
    def _():
        for b in range(per_step):
            rows = pl.ds(b * CHUNK, CHUNK)
            _gmlp_prompt_chunk(u_ref.at[rows], v_ref.at[rows], w_ref, bt_ref, gn_ref,
                               o_ref.at[rows])


def _gmlp_prompt_chunk(u_ref, v_ref, w_ref, bt_ref, gn_ref, o_ref):
    gv = _rms(v_ref[...].astype(F32), gn_ref[...])
    c = CHUNK
    row = lax.broadcasted_iota(jnp.int32, (c, c), 0)
    col = lax.broadcasted_iota(jnp.int32, (c, c), 1)
    causal = col <= row
    for h in range(GM_HEADS):
        cs = slice(h * GM_HEAD_DIM, (h + 1) * GM_HEAD_DIM)
        w = jnp.where(causal, w_ref[h], 0.0).astype(BF16)
        mixed = jnp.dot(w, gv[:, cs].astype(BF16), preferred_element_type=F32)
        mixed = mixed + bt_ref[:, h:h + 1]
        o_ref[:, cs] = (u_ref[:, cs].astype(F32) * mixed).astype(o_ref.dtype)


def gmlp_prompt(gm, w_s, b_t, g_norm, gated_sample, *, m_total, m_prompt, per_step):
    c = CHUNK
    rows = per_step * c
    assert m_prompt % rows == 0 and gated_sample.shape[0] % rows == 0
    nsteps = m_prompt // rows
    last = nsteps - 1
    return pl.pallas_call(
        functools.partial(_gmlp_prompt_kernel, n_prompt_steps=nsteps, per_step=per_step),
        grid=(m_total // rows,),
        in_specs=[
            pl.BlockSpec((rows, GM_WIDTH), lambda r: (jnp.minimum(r, last), 0)),
            pl.BlockSpec((rows, GM_WIDTH), lambda r: (jnp.minimum(r, last), 1)),
            pl.BlockSpec((GM_HEADS, c, c), lambda r: (0, 0, 0)),
            pl.BlockSpec((c, GM_HEADS), lambda r: (0, 0)),
            pl.BlockSpec((1, GM_WIDTH), lambda r: (0, 0)),
            pl.BlockSpec((rows, GM_WIDTH), lambda r: (jnp.maximum(r - nsteps, 0), 0)),
        ],
        out_specs=pl.BlockSpec((rows, GM_WIDTH), lambda r: (r, 0)),
        out_shape=jax.ShapeDtypeStruct((m_total, GM_WIDTH), BF16),
        compiler_params=_params(("arbitrary",), 40),
        name="gmlp_prompt",
    )(gm, gm, w_s, b_t, g_norm, gated_sample)


def _gmlp_sample_kernel(u_ref, v_ref, wc_ref, bc_ref, gn_ref, o_ref, gv_ref, *, t_new):
    rows = u_ref.shape[0]
    nseq = rows // t_new
    gv = _rms(v_ref[...].astype(F32), gn_ref[...])
    gv_ref[...] = gv
    t = lax.broadcasted_iota(jnp.int32, (t_new, GM_HEAD_DIM), 0)
    for h in range(GM_HEADS):
        cs = slice(h * GM_HEAD_DIM, (h + 1) * GM_HEAD_DIM)
        x = gv[:, cs].reshape(nseq, t_new, GM_HEAD_DIM)
        acc = jnp.zeros((nseq, t_new, GM_HEAD_DIM), F32)
        for s in range(t_new):
            coef = jnp.where(t >= s, wc_ref[h, s], 0.0)
            acc = acc + coef[None] * x[:, s:s + 1, :]
        mixed = acc + bc_ref[h][None]
        u = u_ref[:, cs].astype(F32).reshape(nseq, t_new, GM_HEAD_DIM)
        o_ref[:, cs] = (u * mixed).reshape(rows, GM_HEAD_DIM).astype(o_ref.dtype)


def gmlp_sample(gm, w_s, b_s, g_norm, *, m_prompt, m_sample, t_new):
    lanes = GM_HEAD_DIM
    w8 = w_s[:, :t_new, :t_new]
    wc = jnp.broadcast_to(
        w8.transpose(0, 2, 1)[..., None], (GM_HEADS, t_new, t_new, lanes)
    )
    bc = jnp.broadcast_to(b_s[:, :t_new, None], (GM_HEADS, t_new, lanes))
    blk = m_prompt // m_sample
    return pl.pallas_call(
        functools.partial(_gmlp_sample_kernel, t_new=t_new),
        grid=(1,),
        in_specs=[
            pl.BlockSpec((m_sample, GM_WIDTH), lambda i: (blk, 0)),
            pl.BlockSpec((m_sample, GM_WIDTH), lambda i: (blk, 1)),
            pl.BlockSpec((GM_HEADS, t_new, t_new, lanes), lambda i: (0, 0, 0, 0)),
            pl.BlockSpec((GM_HEADS, t_new, lanes), lambda i: (0, 0, 0)),
            pl.BlockSpec((1, GM_WIDTH), lambda i: (0, 0)),
        ],
        out_specs=[
            pl.BlockSpec((m_sample, GM_WIDTH), lambda i: (0, 0)),
            pl.BlockSpec((m_sample, GM_WIDTH), lambda i: (0, 0)),
        ],
        out_shape=[
            jax.ShapeDtypeStruct((m_sample, GM_WIDTH), BF16),
            jax.ShapeDtypeStruct((m_sample, GM_WIDTH), F32),
        ],
        compiler_params=_params(("arbitrary",), 40),
        name="gmlp_sample",
    )(gm, gm, wc, bc, g_norm)


def _conv_gelu_gate(g, g1, g2, up, cw_ref, cb_ref):
    c = cb_ref[...] + cw_ref[0:1, :] * g2
    c = c + cw_ref[1:2, :] * g1
    c = c + cw_ref[2:3, :] * g
    return _gelu_times(c, up)


def _ffn_a_kernel(h_ref, hs_ref, wg_ref, wu_ref, cw_ref, cb_ref, st_ref, wd_ref, a_ref,
                  tail_ref, as_ref, gs_ref, wdb_ref, gx_ref, *, chunk, t_new,
                  cast_blocks):
    i = pl.program_id(0)
    _side_cast(wd_ref, wdb_ref, cast_blocks)
    tm = h_ref.shape[0]
    tf = wg_ref.shape[1]
    wg = wg_ref[...].astype(BF16)
    wu = wu_ref[...].astype(BF16)

    prev8 = jnp.zeros((8, tf), F32)
    for r0 in range(0, tm, chunk):
        h = h_ref[r0:r0 + chunk, :]
        g = jnp.dot(h, wg, preferred_element_type=F32)
        up = jnp.dot(h, wu, preferred_element_type=F32)
        gx_ref[0:8, :] = prev8
        gx_ref[8:8 + chunk, :] = g
        g1 = gx_ref[7:7 + chunk, :]
        g2 = gx_ref[6:6 + chunk, :]
        a_ref[r0:r0 + chunk, :] = _conv_gelu_gate(g, g1, g2, up, cw_ref, cb_ref).astype(
            a_ref.dtype
        )
        prev8 = g[chunk - 8:, :]
    tail_ref[...] = prev8

    @pl.when(i == pl.num_programs(0) - 1)
    def _():
        hs = hs_ref[...]
        rows = hs.shape[0]
        g = jnp.dot(hs, wg, preferred_element_type=F32)
        up = jnp.dot(hs, wu, preferred_element_type=F32)
        gs_ref[...] = g
        t = lax.broadcasted_iota(jnp.int32, g.shape, 0) % t_new
        st2 = st_ref[...]
        st1 = pltpu.roll(st2, rows - 1, 0)
        g1 = jnp.where(t >= 1, pltpu.roll(g, 1, 0), st1)
        g2 = jnp.where(t >= 2, pltpu.roll(g, 2, 0), st2)
        as_ref[...] = _conv_gelu_gate(g, g1, g2, up, cw_ref, cb_ref).astype(as_ref.dtype)


def ffn_a(h, w_gate, w_up, w_down, conv_w, conv_b, conv_state, layer, *, m_prompt,
          m_sample, seq, t_new, tf, chunk, cast_rows):
    m_total, d = h.shape
    f = w_gate.shape[-1]
    nf = f // tf
    ni = m_prompt // seq
    assert t_new % 8 == 0 and t_new >= CONV_W - 1
    st = jnp.pad(conv_state, ((0, 0), (0, t_new - (CONV_W - 1)), (0, 0)))
    st = st.reshape(m_sample, f)
    sample_col = lambda i, j: (0, jnp.where(i == ni - 1, j, 0))
    wd_in, wd_out, wd_shape, cast_blocks = _side_cast_specs(
        w_down, layer, cast_rows, ni * nf, nf
    )
    return pl.pallas_call(
        functools.partial(_ffn_a_kernel, chunk=chunk, t_new=t_new,
                          cast_blocks=cast_blocks),
        grid=(ni, nf),
        in_specs=[
            pl.BlockSpec((seq, d), lambda i, j: (i, 0), pipeline_mode=pl.Buffered(1)),
            pl.BlockSpec((m_sample, d), lambda i, j: (m_prompt // m_sample, 0)),
            pl.BlockSpec((None, d, tf), lambda i, j: (layer, 0, j)),
            pl.BlockSpec((None, d, tf), lambda i, j: (layer, 0, j)),
            pl.BlockSpec((None, CONV_W, tf), lambda i, j: (layer, 0, j)),
            pl.BlockSpec((None, 1, tf), lambda i, j: (layer, 0, j)),
            pl.BlockSpec((m_sample, tf), lambda i, j: (0, j)),
            wd_in,
        ],
        out_specs=[
            pl.BlockSpec((seq, tf), lambda i, j: (i, j)),
            pl.BlockSpec((None, 8, tf), lambda i, j: (i, 0, j)),
            pl.BlockSpec((m_sample, tf), sample_col),
            pl.BlockSpec((m_sample, tf), sample_col),
            wd_out,
        ],
        out_shape=[
            jax.ShapeDtypeStruct((m_total, f), BF16),
            jax.ShapeDtypeStruct((ni, 8, f), F32),
            jax.ShapeDtypeStruct((m_sample, f), BF16),
            jax.ShapeDtypeStruct((m_sample, f), F32),
            wd_shape,
        ],
        scratch_shapes=[pltpu.VMEM((chunk + 8, tf), F32)],
        compiler_params=_params(("arbitrary", "arbitrary"), 56),
        name="ffn_a",
    )(h, h, w_gate, w_up, conv_w, conv_b, st, w_down)


def kernel(x_prompt, x_sample, state_swa_k, state_swa_v, state_conv, w_in, w_out,
           attn_sinks, gm_spatial, gm_bias, gm_norm, norm_pre_mix, norm_post_mix,
           norm_pre_ffn, norm_post_ffn, w_ffn_gate, w_ffn_up, w_ffn_down, conv_w,
           conv_b):
    batch, seq, d = x_prompt.shape
    dec_batch, dec_seq, _ = x_sample.shape
    depth = w_in.shape[0]
    f = w_ffn_gate.shape[-1]
    mp = batch * seq
    ms = dec_batch * dec_seq
    m = mp + ms
    keep = min(WINDOW, seq)

    tm_mm = m // 8
    tm_down = m // 12
    tf = 256
    tm_rows = ms
    assert mp % tm_rows == 0

    w_in_b = w_in
    conv_b3 = conv_b.reshape(depth, 1, f)
    cast_rows = 64

    x = (x_prompt.reshape(mp, d), x_sample.reshape(ms, d))
    h = rms_rows(x, norm_pre_mix[0].reshape(1, d), tm=tm_rows)

    p_k, p_v, p_c, s_k, s_v, s_c, s_g = [], [], [], [], [], [], []
    for l in range(depth):
        last = l + 1 == depth
        tn = 512
        tm_in = tm_mm if w_in_b.dtype == F32 else m // 6
        q, kv, gm, w_out_b = matmul_in(h, w_in_b, (w_out, l, cast_rows), tm=tm_in, tn=tn)
        w_out_b = w_out_b.reshape(1, 2, Q_COLS, d)

        kv_p = jnp.stack([kv[(b + 1) * seq - keep:(b + 1) * seq] for b in range(batch)])
        kv_s = kv[mp:]
        k_s, v_s = kv_s[:, :KV_COLS], kv_s[:, KV_COLS:]
        p_k.append(kv_p[..., :KV_COLS].reshape(batch, keep, N_KV_HEADS, HEAD_DIM))
        p_v.append(kv_p[..., KV_COLS:].reshape(batch, keep, N_KV_HEADS, HEAD_DIM))
        s_k.append(k_s.reshape(dec_batch, dec_seq, N_KV_HEADS, HEAD_DIM))
        s_v.append(v_s.reshape(dec_batch, dec_seq, N_KV_HEADS, HEAD_DIM))

        att_s = attn_sample(q[mp:], k_s, v_s, state_swa_k[l], state_swa_v[l],
                            attn_sinks[l], nb=8)
        att = attn_prompt(q, kv, attn_sinks[l], att_s, m_total=m, m_prompt=mp, seq=seq,
                          per_step=2)

        gmo_s, gv_s = gmlp_sample(gm, gm_spatial[l], gm_bias[l], gm_norm[l].reshape(1, -1),
                                  m_prompt=mp, m_sample=ms, t_new=dec_seq)
        gmo = gmlp_prompt(gm, gm_spatial[l], gm_bias[l].T, gm_norm[l].reshape(1, -1),
                          gmo_s, m_total=m, m_prompt=mp, per_step=2)
        s_g.append(gv_s.reshape(dec_batch, dec_seq, GM_HEADS, GM_HEAD_DIM))

        mix, w_in_next = matmul_two(att, gmo, w_out_b, 0, tm=tm_mm, tn=1024,
                                    out_dtype=BF16, name="mm_out",
                                    cast=None if last else (w_in, l + 1, 2 * cast_rows))
        x, h = post_rows(mix, x, norm_post_mix[l].reshape(1, d),
                         norm_pre_ffn[l].reshape(1, d), tm=tm_rows)

        a, tail, a_s, g_s, w_down_b = ffn_a(
            h, w_ffn_gate, w_ffn_up, w_ffn_down, conv_w, conv_b3, state_conv[l], l,
            m_prompt=mp, m_sample=ms, seq=seq, t_new=dec_seq, tf=tf, chunk=256,
            cast_rows=2 * cast_rows)
        a = place_rows(a, a_s, row_block=mp // ms)
        p_c.append(tail[:, 8 - (CONV_W - 1):])
        s_c.append(g_s.reshape(dec_batch, dec_seq, f)[:, dec_seq - (CONV_W - 1):])

        ffn, _ = matmul_cols(a, w_down_b[None], 0, col_block0=0, n=d, tm=tm_down, tn=512,
                             out_dtype=BF16, epilogue=_identity, name="mm_down",
                             vmem_mib=58)
        x, h = post_rows(ffn, x, norm_post_ffn[l].reshape(1, d),
                         None if last else norm_pre_mix[l + 1].reshape(1, d),
                         tm=tm_rows, split_out=(mp, ms) if last else None)
        if not last:
            w_in_b = w_in_next[None]

    x_p, x_s = x
    return (x_p.reshape(batch, seq, d), x_s.reshape(dec_batch, dec_seq, d),
            jnp.stack(p_k), jnp.stack(p_v), jnp.stack(p_c),
            jnp.stack(s_k), jnp.stack(s_v), jnp.stack(s_c), jnp.stack(s_g))
```
